```python
import math
import jax
import jax.numpy as jnp
from jax import lax
import numpy as np

D_MODEL = 1024
BATCH = 8
SEQ = 2048
DEPTH = 2
DEC_BATCH = 128
DEC_SEQ = 4
PAST_LEN = 2048
PAGE_SIZE = 128

HEAD_DIM = 64
NSA_HEADS = 8
NSA_KV_HEADS = 2
CMP_BLK = 32
CMP_STRIDE = 16
CMP_HIDDEN = 256
SEL_BLK = 64
SEL_TOPN = 16
WINDOW = 512
NSA_QBLOCK = 64
WIN_QBLOCK = 128
GLA_HEADS = 4
GLA_DK = 64
GLA_DV = 128
GLA_LOWRANK = 16
GLA_TAU = 16.0
GLA_CHUNK = 64
FOX_HEADS = 8
FOX_QBLOCK = 128
S5_GROUPS = 32
S5_CH = 16
S5_STATE = 64
N_EXPERTS = 32
TOP_K = 4
D_FF = D_MODEL
SWIGLU_LIMIT = 7.0
SWIGLU_ALPHA = 1.702
MOE_BLOCK = 128
EPS = 1e-6
NEG_INF = -1e30
BIG = 1e9
N_AB = (DEPTH + 1) // 2
N_CD = DEPTH // 2
NSA_WIDTH = NSA_HEADS * HEAD_DIM
NSA_KVW = NSA_KV_HEADS * HEAD_DIM
GLA_QK = GLA_HEADS * GLA_DK
GLA_WIDTH = GLA_HEADS * GLA_DV
FOX_WIDTH = FOX_HEADS * HEAD_DIM
S5_WIDTH = S5_GROUPS * S5_CH
AB_SPLITS = (NSA_WIDTH, 6 * NSA_KVW, 3 * NSA_HEADS, GLA_QK, GLA_QK, GLA_WIDTH, GLA_WIDTH, GLA_LOWRANK)
AB_IN = sum(AB_SPLITS)
AB_OUT = NSA_WIDTH + GLA_WIDTH
CD_SPLITS = (FOX_WIDTH, FOX_WIDTH, FOX_WIDTH, FOX_HEADS, S5_WIDTH)
CD_IN = sum(CD_SPLITS)
CD_OUT = FOX_WIDTH + S5_WIDTH

kernel_name = 'hybrid_nsa_gla_fox_s5_moe_step'


def _split(x, sizes):
    return jnp.split(x, np.cumsum(sizes)[:-1].tolist(), axis=-1)


def _rmsnorm(x, g):
    xf = x.astype(jnp.float32)
    y = xf * lax.rsqrt(jnp.mean(xf * xf, axis=-1, keepdims=True) + EPS)
    return (y * g.astype(jnp.float32)).astype(x.dtype)


def _modulated_norm(x, g, shift, scale):
    return _rmsnorm(x, g) * (1 + scale[:, None, :]) + shift[:, None, :]


def _n_blocks(n, block):
    return n // block if n % block == 0 else 1


def _split_q(a, nb, axis=1):
    a = jnp.moveaxis(a, axis, 0)
    a = a.reshape((nb, a.shape[0] // nb) + a.shape[1:])
    return jnp.moveaxis(a, 1, axis + 1)


def _merge_q(o):
    o = jnp.moveaxis(o, 0, 1)
    return o.reshape((o.shape[0], o.shape[1] * o.shape[2]) + o.shape[3:])


def _gather_pages(pool, page_table):
    g = pool[page_table]
    return g.reshape((g.shape[0], g.shape[1] * g.shape[2]) + g.shape[3:])


def _attend(q, k, v, mask, bias=None):
    B, Q, H, D = q.shape
    G = k.shape[2]
    s = jnp.einsum('bqgnd,bkgd->bgnqk', q.reshape(B, Q, G, H // G, D), k).astype(jnp.float32) * D ** -0.5
    if bias is not None:
        s = s + bias
    p = jax.nn.softmax(jnp.where(mask, s, NEG_INF), axis=-1).astype(v.dtype)
    return jnp.einsum('bgnqk,bkgd->bqgnd', p, v).reshape(B, Q, H, D)


def _window_mask(q_pos, k_pos):
    d = q_pos[:, None] - k_pos[None, :]
    return (d >= 0) & (d < WINDOW) & (k_pos[None, :] >= 0)


def _nsa_compress(k, pe, w1, w2):
    B, T, G, D = k.shape
    r = CMP_BLK // CMP_STRIDE
    nch = T // CMP_STRIDE
    n = nch - r + 1
    ch = k[:, :nch * CMP_STRIDE].reshape(B, nch, CMP_STRIDE, G, D)
    blocks = jnp.concatenate([ch[:, i:i + n] for i in range(r)], axis=2) + pe[:, None, :]
    flat = blocks.transpose(0, 1, 3, 2, 4).reshape(B, n, G, CMP_BLK * D)
    return jax.nn.gelu(flat @ w1) @ w2


def _cmp_attention(q, kc, vc, q_pos):
    B, Q, H, D = q.shape
    N, G = kc.shape[1], kc.shape[2]
    c_end = jnp.arange(N) * CMP_STRIDE + CMP_BLK
    valid = c_end[None, :] <= q_pos[:, None] + 1
    s = jnp.einsum('bqgnd,bcgd->bgnqc', q.reshape(B, Q, G, H // G, D), kc).astype(jnp.float32) * D ** -0.5
    p = jax.nn.softmax(jnp.where(valid, s, NEG_INF), axis=-1) * valid
    o = jnp.einsum('bgnqc,bcgd->bqgnd', p.astype(vc.dtype), vc).reshape(B, Q, H, D)
    return o, p


def _cmp_to_sel(n_cmp, n_sel):
    c0 = jnp.arange(n_cmp)[:, None] * CMP_STRIDE
    s0 = jnp.arange(n_sel)[None, :] * SEL_BLK
    ov = jnp.minimum(c0 + CMP_BLK, s0 + SEL_BLK) - jnp.maximum(c0, s0)
    return jnp.maximum(ov, 0).astype(jnp.float32) / CMP_BLK


def _selected_attention(q, k, v, p_cmp, q_pos):
    B, Q, H, D = q.shape
    T_all, G = k.shape[1], k.shape[2]
    n_sel = -(-T_all // SEL_BLK)
    imp = jnp.einsum('bgnqc,cs->bgqs', p_cmp, _cmp_to_sel(p_cmp.shape[-1], n_sel))
    blk = jnp.arange(n_sel)[None, :]
    cur = (q_pos // SEL_BLK)[:, None]
    valid = blk * SEL_BLK <= q_pos[:, None]
    forced = (blk == 0) | (blk == cur) | (blk == cur - 1)
    score = jnp.where(valid, jnp.where(forced, BIG, imp), -BIG)
    _, idx = lax.top_k(score, min(SEL_TOPN, n_sel))
    tok = (idx[..., None] * SEL_BLK + jnp.arange(SEL_BLK)).reshape(B, G, Q, -1)
    pad = ((0, 0), (0, n_sel * SEL_BLK - T_all), (0, 0), (0, 0))
    kt = jnp.pad(k, pad).transpose(0, 2, 1, 3)
    vt = jnp.pad(v, pad).transpose(0, 2, 1, 3)
    b_ix = jnp.arange(B)[:, None, None]
    g_ix = jnp.arange(G)[None, :, None]
    nb = _n_blocks(Q, NSA_QBLOCK)

    def block(args):
        qb, tb, pb = args
        qn, L = tb.shape[2], tb.shape[3]
        flat = tb.reshape(B, G, qn * L)
        ks = kt[b_ix, g_ix, flat].reshape(B, G, qn, L, D)
        vs = vt[b_ix, g_ix, flat].reshape(B, G, qn, L, D)
        s = jnp.einsum('bqgnd,bgqld->bgnql', qb.reshape(B, qn, G, H // G, D), ks).astype(jnp.float32) * D ** -0.5
        mask = (tb <= pb[None, None, :, None])[:, :, None]
        p = jax.nn.softmax(jnp.where(mask, s, NEG_INF), axis=-1).astype(vs.dtype)
        return jnp.einsum('bgnql,bgqld->bqgnd', p, vs).reshape(B, qn, H, D)

    return _merge_q(lax.map(block, (_split_q(q, nb), _split_q(tok, nb, 2), _split_q(q_pos, nb, 0))))


def _banded_window_attention(q, k, v):
    B, T, H, D = q.shape
    nb = _n_blocks(T, WIN_QBLOCK)
    blk = T // nb
    pad = ((0, 0), (WINDOW, 0), (0, 0), (0, 0))
    kp, vp = jnp.pad(k, pad), jnp.pad(v, pad)

    def block(args):
        i, qb = args
        kb = lax.dynamic_slice_in_dim(kp, i * blk, blk + WINDOW, axis=1)
        vb = lax.dynamic_slice_in_dim(vp, i * blk, blk + WINDOW, axis=1)
        q_pos = i * blk + jnp.arange(blk)
        k_pos = i * blk - WINDOW + jnp.arange(blk + WINDOW)
        return _attend(qb, kb, vb, _window_mask(q_pos, k_pos))

    return _merge_q(lax.map(block, (jnp.arange(nb), _split_q(q, nb))))


def _gla(q, k, v, log_a, s0, chunk):
    B, T, H, K = q.shape
    V = v.shape[-1]
    n = T // chunk

    def to_chunks(a):
        return jnp.moveaxis(a.reshape((B, n, chunk) + a.shape[2:]), 1, 0)

    qc, kc, vc, gc = to_chunks(q), to_chunks(k), to_chunks(v), to_chunks(log_a)
    b = jnp.cumsum(gc, axis=2)
    b_last = b[:, :, -1]
    q_dec = qc * jnp.exp(b)
    k_inv = kc * jnp.exp(-b)
    k_end = kc * jnp.exp(b_last[:, :, None] - b)
    causal = jnp.tril(jnp.ones((chunk, chunk), bool))
    attn = jnp.where(causal, jnp.einsum('nbthk,nbshk->nbhts', q_dec, k_inv), 0.0)
    o_intra = jnp.einsum('nbhts,nbshv->nbthv', attn, vc)

    def step(S, inp):
        q_i, ke_i, v_i, bl_i = inp
        o_i = jnp.einsum('bthk,bhkv->bthv', q_i, S)
        S = jnp.exp(bl_i)[..., None] * S + jnp.einsum('bshk,bshv->bhkv', ke_i, v_i)
        return S, o_i

    s_final, o_inter = lax.scan(step, s0, (q_dec, k_end, vc, b_last))
    o = jnp.moveaxis(o_intra + o_inter, 0, 1).reshape(B, T, H, V)
    return o, s_final


def _fox_block(q, k, v, Fq, Fk, q_pos, k_pos):
    bias = (Fq[:, :, None, :] - Fk[:, None, :, :]).transpose(0, 3, 1, 2)[:, :, None]
    return _attend(q, k, v, k_pos[None, :] <= q_pos[:, None], bias)


def _complex_affine_combine(e1, e2):
    a1r, a1i, b1r, b1i = e1
    a2r, a2i, b2r, b2i = e2
    return (a2r * a1r - a2i * a1i, a2r * a1i + a2i * a1r,
            a2r * b1r - a2i * b1i + b2r, a2r * b1i + a2i * b1r + b2i)


def _s5(u, a_re, a_im, b_re, b_im, c_re, c_im, d, log_dt, glu_w, glu_b, h0):
    B, T, _ = u.shape
    f32 = jnp.float32
    uf = u.astype(f32).reshape(B, T, S5_GROUPS, S5_CH)
    a_re, a_im = a_re.astype(f32), a_im.astype(f32)
    dt = jnp.exp(log_dt.astype(f32))[:, None]
    mag = jnp.exp(a_re * dt)
    ab_re, ab_im = mag * jnp.cos(a_im * dt), mag * jnp.sin(a_im * dt)
    den = a_re * a_re + a_im * a_im
    coef_re = ((ab_re - 1) * a_re + ab_im * a_im) / den
    coef_im = (ab_im * a_re - (ab_re - 1) * a_im) / den
    bu_re = jnp.einsum('gpc,btgc->btgp', b_re.astype(f32), uf)
    bu_im = jnp.einsum('gpc,btgc->btgp', b_im.astype(f32), uf)
    x_re = coef_re * bu_re - coef_im * bu_im
    x_im = coef_re * bu_im + coef_im * bu_re
    if h0 is not None:
        h0 = h0.astype(f32)
        x_re = x_re.at[:, 0].add(ab_re * h0[..., 0] - ab_im * h0[..., 1])
        x_im = x_im.at[:, 0].add(ab_re * h0[..., 1] + ab_im * h0[..., 0])
    A_re = jnp.broadcast_to(ab_re, x_re.shape)
    A_im = jnp.broadcast_to(ab_im, x_im.shape)
    _, _, h_re, h_im = lax.associative_scan(_complex_affine_combine, (A_re, A_im, x_re, x_im), axis=1)
    y = (jnp.einsum('gcp,btgp->btgc', c_re.astype(f32), h_re)
         - jnp.einsum('gcp,btgp->btgc', c_im.astype(f32), h_im) + d.astype(f32) * uf)
    y = jax.nn.gelu(y.reshape(B, T, S5_WIDTH)).astype(u.dtype)
    y = y * jax.nn.sigmoid(y @ glu_w + glu_b)
    return y, jnp.stack([h_re[:, -1], h_im[:, -1]], axis=-1)


def _moe(x, w_r, b_r, w1, b1, w2, b2):
    B, T, D = x.shape
    xf = x.reshape(B * T, D)
    n_rows = B * T * TOP_K
    logits = (xf @ w_r).astype(jnp.float32) + b_r.astype(jnp.float32)
    top_v, top_i = lax.top_k(logits, TOP_K)
    gate = jax.nn.softmax(top_v, axis=-1).reshape(-1)
    flat_e = top_i.reshape(-1)
    order = jnp.argsort(flat_e)
    sorted_e = flat_e[order]
    tok = order // TOP_K
    counts = jnp.bincount(flat_e, length=N_EXPERTS)
    padded = (counts + MOE_BLOCK - 1) // MOE_BLOCK * MOE_BLOCK
    pad_end = jnp.cumsum(padded)
    pad_start = pad_end - padded
    start = jnp.cumsum(counts) - counts
    dest = pad_start[sorted_e] + jnp.arange(n_rows) - start[sorted_e]
    n_blocks = -(-(n_rows + N_EXPERTS * (MOE_BLOCK - 1)) // MOE_BLOCK)
    xs = jnp.zeros((n_blocks * MOE_BLOCK, D), x.dtype).at[dest].set(xf[tok])
    blk_e = jnp.minimum(jnp.searchsorted(pad_end, jnp.arange(n_blocks) * MOE_BLOCK, side='right'), N_EXPERTS - 1)

    def expert_block(args):
        xb, e = args
        gu = xb @ w1[e] + b1[e]
        g = jnp.minimum(gu[:, :D_FF], SWIGLU_LIMIT)
        up = jnp.clip(gu[:, D_FF:], -SWIGLU_LIMIT, SWIGLU_LIMIT)
        return ((up + 1) * g * jax.nn.sigmoid(SWIGLU_ALPHA * g)) @ w2[e] + b2[e]

    ys = lax.map(expert_block, (xs.reshape(n_blocks, MOE_BLOCK, D), blk_e)).reshape(-1, D)
    y_rows = ys[dest] * gate[order][:, None].astype(ys.dtype)
    return jnp.zeros_like(xf).at[tok].add(y_rows).reshape(B, T, D)


def _mixer_ab(h, w_in, w_out, cmp_pe, cmp_w1, cmp_w2, gla_wa, gla_ba, gla_g, past):
    B, T, _ = h.shape
    f32 = jnp.float32
    q, kv6, gate_logit, gq, gk, gv, gr, ga = _split(h @ w_in, AB_SPLITS)
    q = q.reshape(B, T, NSA_HEADS, HEAD_DIM)
    kv6 = kv6.reshape(B, T, 6, NSA_KV_HEADS, HEAD_DIM)
    new_nsa, new_win = kv6[:, :, :4], kv6[:, :, 4:]
    s0 = jnp.zeros((B, GLA_HEADS, GLA_DK, GLA_DV), f32)
    if past is None:
        t0 = 0
        kv_full, kv_win = new_nsa, new_win
    else:
        past_nsa, win_buf, s0 = past
        t0 = past_nsa.shape[1]
        kv_full = jnp.concatenate([past_nsa, new_nsa], axis=1)
        kv_win = jnp.concatenate([win_buf, new_win], axis=1)
    q_pos = t0 + jnp.arange(T)
    kc = _nsa_compress(kv_full[:, :, 0], cmp_pe[0], cmp_w1[0], cmp_w2[0])
    vc = _nsa_compress(kv_full[:, :, 1], cmp_pe[1], cmp_w1[1], cmp_w2[1])
    o_cmp, p_cmp = _cmp_attention(q, kc, vc, q_pos)
    o_slc = _selected_attention(q, kv_full[:, :, 2], kv_full[:, :, 3], p_cmp, q_pos)
    if past is None:
        o_win = _banded_window_attention(q, kv_win[:, :, 0], kv_win[:, :, 1])
        win_state = kv_win[:, -min(WINDOW, T):]
    else:
        wb = past[1].shape[1]
        k_pos = t0 - wb + jnp.arange(kv_win.shape[1])
        o_win = _attend(q, kv_win[:, :, 0], kv_win[:, :, 1], _window_mask(q_pos, k_pos))
        win_state = kv_win[:, -wb:]
    g = jax.nn.sigmoid(gate_logit.astype(f32)).astype(q.dtype).reshape(B, T, NSA_HEADS, 3, 1)
    o_nsa = (g[:, :, :, 0] * o_cmp + g[:, :, :, 1] * o_slc + g[:, :, :, 2] * o_win).reshape(B, T, NSA_WIDTH)
    gq = gq.reshape(B, T, GLA_HEADS, GLA_DK).astype(f32) * GLA_DK ** -0.5
    gk = gk.reshape(B, T, GLA_HEADS, GLA_DK).astype(f32)
    gv = gv.reshape(B, T, GLA_HEADS, GLA_DV).astype(f32)
    log_a = jax.nn.log_sigmoid((ga @ gla_wa + gla_ba).astype(f32)).reshape(B, T, GLA_HEADS, GLA_DK) / GLA_TAU
    chunk = GLA_CHUNK if T % GLA_CHUNK == 0 else T
    o, s_new = _gla(gq, gk, gv, log_a, s0.astype(f32), chunk)
    o_gla = _rmsnorm(o, gla_g).reshape(B, T, GLA_WIDTH).astype(h.dtype) * jax.nn.silu(gr)
    y = jnp.concatenate([o_nsa, o_gla.astype(o_nsa.dtype)], axis=-1) @ w_out
    return y, (new_nsa, win_state, s_new)


def _mixer_cd(h, w_in, w_out, b_f, a_re, a_im, b_re, b_im, c_re, c_im, d, log_dt, glu_w, glu_b, past):
    B, T, _ = h.shape
    q, k, v, f, u = _split(h @ w_in, CD_SPLITS)
    q, k, v = (a.reshape(B, T, FOX_HEADS, HEAD_DIM) for a in (q, k, v))
    log_f = jax.nn.log_sigmoid(f.astype(jnp.float32) + b_f.astype(jnp.float32))
    new_kv = jnp.stack([k, v], axis=2)
    h0 = None
    if past is None:
        t0 = 0
        k_all, v_all, lf_all = k, v, log_f
    else:
        past_kv, past_lf, h0 = past
        t0 = past_kv.shape[1]
        k_all = jnp.concatenate([past_kv[:, :, 0], k], axis=1)
        v_all = jnp.concatenate([past_kv[:, :, 1], v], axis=1)
        lf_all = jnp.concatenate([past_lf.astype(jnp.float32), log_f], axis=1)
    F = jnp.cumsum(lf_all, axis=1)
    Fq = F[:, t0:]
    q_pos = t0 + jnp.arange(T)
    k_pos = jnp.arange(k_all.shape[1])
    nb = _n_blocks(T, FOX_QBLOCK)

    def block(args):
        qb, fb, pb = args
        return _fox_block(qb, k_all, v_all, fb, F, pb, k_pos)

    o_fox = _merge_q(lax.map(block, (_split_q(q, nb), _split_q(Fq, nb), _split_q(q_pos, nb, 0)))).reshape(B, T, FOX_WIDTH)
    y_s5, s5_state = _s5(u, a_re, a_im, b_re, b_im, c_re, c_im, d, log_dt, glu_w, glu_b, h0)
    y = jnp.concatenate([o_fox, y_s5.astype(o_fox.dtype)], axis=-1) @ w_out
    return y, (new_kv, log_f, s5_state)


def _trunk(x, c, p, past):
    new = {name: [] for name in ('nsa_kv', 'win_kv', 'gla', 'fox_kv', 'fox_logf', 's5')}
    for l in range(DEPTH):
        mod = jax.nn.silu(c) @ p['ada_w'][l] + p['ada_b'][l]
        sh1, sc1, g1, sh2, sc2, g2 = jnp.split(mod, 6, axis=-1)
        hn = _modulated_norm(x, p['norm_g'][l, 0], sh1, sc1)
        j = l // 2
        if l % 2 == 0:
            lp = None
            if past is not None:
                lp = (_gather_pages(past['nsa_kv'][j], past['page_table']), past['win_kv'][j], past['gla'][j])
            y, st = _mixer_ab(hn, p['ab_w_in'][j], p['ab_w_out'][j], p['nsa_cmp_pe'][j], p['nsa_cmp_w1'][j],
                              p['nsa_cmp_w2'][j], p['gla_w_alpha'][j], p['gla_b_alpha'][j], p['gla_norm_g'][j], lp)
            for name, s in zip(('nsa_kv', 'win_kv', 'gla'), st):
                new[name].append(s)
        else:
            lp = None
            if past is not None:
                lp = (_gather_pages(past['fox_kv'][j], past['page_table']),
                      _gather_pages(past['fox_logf'][j], past['page_table']), past['s5'][j])
            y, st = _mixer_cd(hn, p['cd_w_in'][j], p['cd_w_out'][j], p['fox_b_f'][j], p['s5_a_re'][j], p['s5_a_im'][j],
                              p['s5_b_re'][j], p['s5_b_im'][j], p['s5_c_re'][j], p['s5_c_im'][j], p['s5_d'][j],
                              p['s5_log_dt'][j], p['s5_glu_w'][j], p['s5_glu_b'][j], lp)
            for name, s in zip(('fox_kv', 'fox_logf', 's5'), st):
                new[name].append(s)
        x = x + g1[:, None, :] * y
        hn = _modulated_norm(x, p['norm_g'][l, 1], sh2, sc2)
        x = x + g2[:, None, :] * _moe(hn, p['router_w'][l], p['router_b'][l], p['moe_w1'][l], p['moe_b1'][l],
                                      p['moe_w2'][l], p['moe_b2'][l])
    return _rmsnorm(x, p['final_norm_g']), new


def setup_inputs(seed: int = 0) -> dict:
    key = jax.random.key(seed)
    keys = iter(jax.random.split(key, 64))

    def nrm(shape, scale):
        return jax.random.normal(next(keys), shape, jnp.float32) * scale

    n_pages = PAST_LEN // PAGE_SIZE
    n_used = DEC_BATCH * n_pages
    n_phys = n_used + max(n_used // 4, 1)
    win_buf = min(WINDOW, PAST_LEN)
    perm = jax.random.permutation(next(keys), n_phys)
    page_table = perm[:n_used].reshape(DEC_BATCH, n_pages).astype(jnp.int32)
    n_idx = jnp.arange(S5_STATE, dtype=jnp.float32)
    lo, hi = math.log(1e-3), math.log(1e-1)
    return {
        'x_prompt': nrm((BATCH, SEQ, D_MODEL), 1.0),
        'x_sample': nrm((DEC_BATCH, DEC_SEQ, D_MODEL), 1.0),
        'cache_ab_nsa_kv': nrm((N_AB, n_phys, PAGE_SIZE, 4, NSA_KV_HEADS, HEAD_DIM), 1.0),
        'cache_ab_win_kv': nrm((N_AB, DEC_BATCH, win_buf, 2, NSA_KV_HEADS, HEAD_DIM), 1.0),
        'state_ab_gla': nrm((N_AB, DEC_BATCH, GLA_HEADS, GLA_DK, GLA_DV), 0.5),
        'cache_cd_fox_kv': nrm((N_CD, n_phys, PAGE_SIZE, 2, FOX_HEADS, HEAD_DIM), 1.0),
        'cache_cd_fox_logf': jax.nn.log_sigmoid(2.0 + nrm((N_CD, n_phys, PAGE_SIZE, FOX_HEADS), 1.0)),
        'state_cd_s5': nrm((N_CD, DEC_BATCH, S5_GROUPS, S5_STATE, 2), 0.5),
        'page_table': page_table,
        'c_prompt': nrm((BATCH, D_MODEL), 1.0),
        'c_sample': nrm((DEC_BATCH, D_MODEL), 1.0),
        'ada_w': nrm((DEPTH, D_MODEL, 6 * D_MODEL), 0.5 * D_MODEL ** -0.5),
        'ada_b': nrm((DEPTH, 6 * D_MODEL), 0.02),
        'norm_g': 1.0 + nrm((DEPTH, 2, D_MODEL), 0.05),
        'ab_w_in': nrm((N_AB, D_MODEL, AB_IN), D_MODEL ** -0.5),
        'ab_w_out': nrm((N_AB, AB_OUT, D_MODEL), AB_OUT ** -0.5),
        'nsa_cmp_pe': nrm((N_AB, 2, CMP_BLK, HEAD_DIM), 0.1),
        'nsa_cmp_w1': nrm((N_AB, 2, CMP_BLK * HEAD_DIM, CMP_HIDDEN), (CMP_BLK * HEAD_DIM) ** -0.5),
        'nsa_cmp_w2': nrm((N_AB, 2, CMP_HIDDEN, HEAD_DIM), CMP_HIDDEN ** -0.5),
        'gla_w_alpha': nrm((N_AB, GLA_LOWRANK, GLA_QK), GLA_LOWRANK ** -0.5),
        'gla_b_alpha': nrm((N_AB, GLA_QK), 0.02),
        'gla_norm_g': 1.0 + nrm((N_AB, GLA_DV), 0.05),
        'cd_w_in': nrm((N_CD, D_MODEL, CD_IN), D_MODEL ** -0.5),
        'cd_w_out': nrm((N_CD, CD_OUT, D_MODEL), CD_OUT ** -0.5),
        'fox_b_f': 2.0 + nrm((N_CD, FOX_HEADS), 0.5),
        's5_a_re': -0.5 + nrm((N_CD, S5_GROUPS, S5_STATE), 0.01),
        's5_a_im': math.pi * n_idx + nrm((N_CD, S5_GROUPS, S5_STATE), 0.01),
        's5_b_re': nrm((N_CD, S5_GROUPS, S5_STATE, S5_CH), (2 * S5_CH) ** -0.5),
        's5_b_im': nrm((N_CD, S5_GROUPS, S5_STATE, S5_CH), (2 * S5_CH) ** -0.5),
        's5_c_re': nrm((N_CD, S5_GROUPS, S5_CH, S5_STATE), (2 * S5_STATE) ** -0.5),
        's5_c_im': nrm((N_CD, S5_GROUPS, S5_CH, S5_STATE), (2 * S5_STATE) ** -0.5),
        's5_d': nrm((N_CD, S5_GROUPS, S5_CH), 0.5),
        's5_log_dt': lo + (hi - lo) * jax.random.uniform(next(keys), (N_CD, S5_GROUPS), jnp.float32),
        's5_glu_w': nrm((N_CD, S5_WIDTH, S5_WIDTH), S5_WIDTH ** -0.5),
        's5_glu_b': nrm((N_CD, S5_WIDTH), 0.02),
        'router_w': nrm((DEPTH, D_MODEL, N_EXPERTS), D_MODEL ** -0.5),
        'router_b': nrm((DEPTH, N_EXPERTS), 0.01),
        'moe_w1': nrm((DEPTH, N_EXPERTS, D_MODEL, 2 * D_FF), D_MODEL ** -0.5),
        'moe_b1': nrm((DEPTH, N_EXPERTS, 2 * D_FF), 0.02),
        'moe_w2': nrm((DEPTH, N_EXPERTS, D_FF, D_MODEL), D_FF ** -0.5),
        'moe_b2': nrm((DEPTH, N_EXPERTS, D_MODEL), 0.02),
        'final_norm_g': 1.0 + nrm((D_MODEL,), 0.05),
    }


def reference(x_prompt, x_sample, cache_ab_nsa_kv, cache_ab_win_kv, state_ab_gla, cache_cd_fox_kv,
              cache_cd_fox_logf, state_cd_s5, page_table, c_prompt, c_sample, ada_w, ada_b, norm_g,
              ab_w_in, ab_w_out, nsa_cmp_pe, nsa_cmp_w1, nsa_cmp_w2, gla_w_alpha, gla_b_alpha, gla_norm_g,
              cd_w_in, cd_w_out, fox_b_f, s5_a_re, s5_a_im, s5_b_re, s5_b_im, s5_c_re, s5_c_im, s5_d,
              s5_log_dt, s5_glu_w, s5_glu_b, router_w, router_b, moe_w1, moe_b1, moe_w2, moe_b2, final_norm_g):
    p = dict(ada_w=ada_w, ada_b=ada_b, norm_g=norm_g, ab_w_in=ab_w_in, ab_w_out=ab_w_out,
             nsa_cmp_pe=nsa_cmp_pe, nsa_cmp_w1=nsa_cmp_w1, nsa_cmp_w2=nsa_cmp_w2, gla_w_alpha=gla_w_alpha,
             gla_b_alpha=gla_b_alpha, gla_norm_g=gla_norm_g, cd_w_in=cd_w_in, cd_w_out=cd_w_out,
             fox_b_f=fox_b_f, s5_a_re=s5_a_re, s5_a_im=s5_a_im, s5_b_re=s5_b_re, s5_b_im=s5_b_im,
             s5_c_re=s5_c_re, s5_c_im=s5_c_im, s5_d=s5_d, s5_log_dt=s5_log_dt, s5_glu_w=s5_glu_w,
             s5_glu_b=s5_glu_b, router_w=router_w, router_b=router_b, moe_w1=moe_w1, moe_b1=moe_b1,
             moe_w2=moe_w2, moe_b2=moe_b2, final_norm_g=final_norm_g)
    past = dict(nsa_kv=cache_ab_nsa_kv, win_kv=cache_ab_win_kv, gla=state_ab_gla, fox_kv=cache_cd_fox_kv,
                fox_logf=cache_cd_fox_logf, s5=state_cd_s5, page_table=page_table)
    y_prompt, new_p = _trunk(x_prompt, c_prompt, p, None)
    y_sample, new_s = _trunk(x_sample, c_sample, p, past)
    p_nsa_kv = jnp.stack(new_p['nsa_kv'])
    p_win_kv = jnp.stack(new_p['win_kv'])
    p_gla = jnp.stack(new_p['gla'])
    p_fox_kv = jnp.stack(new_p['fox_kv'])
    p_fox_logf = jnp.stack(new_p['fox_logf'])
    p_s5 = jnp.stack(new_p['s5'])
    s_nsa_kv = jnp.stack(new_s['nsa_kv'])
    s_win_kv = jnp.stack(new_s['win_kv'])
    s_gla = jnp.stack(new_s['gla'])
    s_fox_kv = jnp.stack(new_s['fox_kv'])
    s_fox_logf = jnp.stack(new_s['fox_logf'])
    s_s5 = jnp.stack(new_s['s5'])
    return (y_prompt, y_sample, p_nsa_kv, p_win_kv, p_gla, p_fox_kv, p_fox_logf, p_s5,
            s_nsa_kv, s_win_kv, s_gla, s_fox_kv, s_fox_logf, s_s5)
```

```python
import functools
import math

import numpy as np
import jax
import jax.numpy as jnp
from jax import lax
from jax.experimental import pallas as pl
from jax.experimental.pallas import tpu as pltpu

D_MODEL = 1024
DEPTH = 2
PAGE_SIZE = 128
HEAD_DIM = 64
NSA_HEADS = 8
NSA_KV_HEADS = 2
CMP_BLK = 32
CMP_STRIDE = 16
SEL_BLK = 64
SEL_TOPN = 16
WINDOW = 512
GLA_HEADS = 4
GLA_DK = 64
GLA_DV = 128
GLA_LOWRANK = 16
GLA_TAU = 16.0
GLA_CHUNK = 64
FOX_HEADS = 8
S5_GROUPS = 32
S5_CH = 16
S5_STATE = 64
N_EXPERTS = 32
TOP_K = 4
D_FF = D_MODEL
SWIGLU_LIMIT = 7.0
SWIGLU_ALPHA = 1.702
EPS = 1e-6
NEG_INF = -1e30
BIG = 1e9
NSA_WIDTH = NSA_HEADS * HEAD_DIM
NSA_KVW = NSA_KV_HEADS * HEAD_DIM
GLA_QK = GLA_HEADS * GLA_DK
GLA_WIDTH = GLA_HEADS * GLA_DV
FOX_WIDTH = FOX_HEADS * HEAD_DIM
S5_WIDTH = S5_GROUPS * S5_CH
AB_SPLITS = (NSA_WIDTH, 6 * NSA_KVW, 3 * NSA_HEADS, GLA_QK, GLA_QK, GLA_WIDTH, GLA_WIDTH, GLA_LOWRANK)
CD_SPLITS = (FOX_WIDTH, FOX_WIDTH, FOX_WIDTH, FOX_HEADS, S5_WIDTH)

LANE = 128
SUBLANE = 8
VMEM_LIMIT = 56 * 1024 * 1024
MOE_TM = 256

F32 = jnp.float32
BF16 = jnp.bfloat16


def _round_up(n, m):
    return -(-n // m) * m


def _pick_tile(n, candidates):
    for c in candidates:
        if n % c == 0:
            return c
    return n


def _mm_kernel(x_ref, w_ref, b_ref, o_ref, wb_ref, *, act, precise):
    if precise:
        acc = jnp.dot(x_ref[...], w_ref[...], preferred_element_type=F32, precision=lax.Precision.HIGHEST)
    else:
        @pl.when(pl.program_id(1) == 0)
        def _():
            wb_ref[...] = w_ref[...].astype(BF16)

        acc = jnp.dot(x_ref[...].astype(BF16), wb_ref[...], preferred_element_type=F32)
    acc = acc + b_ref[...]
    if act == 'gelu':
        acc = jax.nn.gelu(acc)
    o_ref[...] = acc.astype(o_ref.dtype)


def _mm(x, w, b=None, act=None, out_dtype=F32, precise=False):
    M, K = x.shape
    N = w.shape[1]
    Np = _round_up(N, LANE)
    Mp = _round_up(M, 512) if M > 256 else _round_up(M, SUBLANE)
    tm = min(Mp, 512)
    tn = _pick_tile(Np, (512, 384, 256, 128))
    if Np != N:
        w = jnp.pad(w, ((0, 0), (0, Np - N)))
    if b is None:
        b = jnp.zeros((N,), F32)
    b = jnp.pad(b.astype(F32), (0, Np - N)).reshape(1, Np)
    if Mp != M:
        x = jnp.pad(x, ((0, Mp - M), (0, 0)))
    out = pl.pallas_call(
        functools.partial(_mm_kernel, act=act, precise=precise),
        grid=(Np // tn, Mp // tm),
        in_specs=[pl.BlockSpec((tm, K), lambda j, i: (i, 0)),
                  pl.BlockSpec((K, tn), lambda j, i: (0, j)),
                  pl.BlockSpec((1, tn), lambda j, i: (0, j))],
        out_specs=pl.BlockSpec((tm, tn), lambda j, i: (i, j)),
        out_shape=jax.ShapeDtypeStruct((Mp, Np), out_dtype),
        scratch_shapes=[pltpu.VMEM((K, tn), BF16)],
        compiler_params=pltpu.CompilerParams(
            dimension_semantics=("arbitrary", "arbitrary"), vmem_limit_bytes=VMEM_LIMIT),
        name="mm",
    )(x, w, b)
    return out[:M, :N]


def _flash_kernel(*refs, tq, tk, R, window, use_bias, use_sel):
    q_ref, k_ref, v_ref = refs[:3]
    n = 3
    if use_bias:
        qb_ref, kb_ref = refs[n:n + 2]
        n += 2
    if use_sel:
        sel_ref = refs[n]
        n += 1
    o_ref = refs[n]
    i = pl.program_id(2)
    rows = R * tq
    q = q_ref[0, 0].reshape(rows, HEAD_DIM)
    q_pos = i * tq + jnp.bitwise_and(lax.broadcasted_iota(jnp.int32, (rows, 1), 0), tq - 1)
    hi = ((i + 1) * tq + tk - 1) // tk
    lo = jnp.maximum(i * tq - window + 1, 0) // tk if window else 0
    if use_bias:
        qb = qb_ref[0, 0]
        qb = jnp.concatenate([qb] * (tk // LANE), axis=1)
    if use_sel:
        sel = sel_ref[0, 0].astype(BF16)
        n_sel = sel.shape[1]

    def body(j, carry):
        m, l, acc = carry
        start = pl.multiple_of(j * tk, tk)
        k = k_ref[0, 0, pl.ds(start, tk), :]
        v = v_ref[0, 0, pl.ds(start, tk), :]
        s = lax.dot_general(q, k, (((1,), (1,)), ((), ())), preferred_element_type=F32)
        k_pos = j * tk + lax.broadcasted_iota(jnp.int32, (1, tk), 1)
        if use_bias:
            s = s + qb - kb_ref[0, 0, j]
        s = jnp.where(k_pos <= q_pos, s, NEG_INF)
        if window:
            s = jnp.where(q_pos - k_pos < window, s, NEG_INF)
        if use_sel:
            blk_of_key = (j * tk + lax.broadcasted_iota(jnp.int32, (n_sel, tk), 1)) // SEL_BLK
            expand = jnp.where(blk_of_key == lax.broadcasted_iota(jnp.int32, (n_sel, tk), 0), 1.0, 0.0).astype(BF16)
            chosen = jnp.dot(sel, expand, preferred_element_type=F32)
            chosen = jnp.concatenate([chosen] * R, axis=0)
            s = jnp.where(chosen > 0.5, s, NEG_INF)
        m_new = jnp.maximum(m, jnp.max(s, axis=1, keepdims=True))
        p = jnp.where(s > 0.5 * NEG_INF, jnp.exp(s - m_new), 0.0)
        alpha = jnp.exp(m - m_new)
        l = alpha * l + jnp.sum(p, axis=1, keepdims=True)
        acc = alpha * acc + jnp.dot(p.astype(BF16), v, preferred_element_type=F32)
        return m_new, l, acc

    m0 = jnp.full((rows, 1), NEG_INF, F32)
    l0 = jnp.zeros((rows, 1), F32)
    a0 = jnp.zeros((rows, HEAD_DIM), F32)
    m, l, acc = lax.fori_loop(lo, hi, body, (m0, l0, a0))
    o_ref[0, 0] = (acc / l).reshape(R, tq, HEAD_DIM)


def _flash(q, k, v, *, tq, tk, window=0, qbias=None, kbias=None, sel=None):
    B, G, R, T, _ = q.shape
    grid = (B, G, T // tq)
    in_specs = [pl.BlockSpec((1, 1, R, tq, HEAD_DIM), lambda b, g, i: (b, g, 0, i, 0)),
                pl.BlockSpec((1, 1, T, HEAD_DIM), lambda b, g, i: (b, g, 0, 0)),
                pl.BlockSpec((1, 1, T, HEAD_DIM), lambda b, g, i: (b, g, 0, 0))]
    args = [q, k, v]
    if qbias is not None:
        in_specs += [pl.BlockSpec((1, 1, tq, LANE), lambda b, g, i: (b, g, i, 0)),
                     pl.BlockSpec((1, 1, T // tk, 1, tk), lambda b, g, i: (b, g, 0, 0, 0))]
        args += [qbias, kbias.reshape(B, G, T // tk, 1, tk)]
    if sel is not None:
        in_specs += [pl.BlockSpec((1, 1, tq, sel.shape[-1]), lambda b, g, i: (b, g, i, 0))]
        args += [sel]
    return pl.pallas_call(
        functools.partial(_flash_kernel, tq=tq, tk=tk, R=R, window=window,
                          use_bias=qbias is not None, use_sel=sel is not None),
        grid=grid,
        in_specs=in_specs,
        out_specs=pl.BlockSpec((1, 1, R, tq, HEAD_DIM), lambda b, g, i: (b, g, 0, i, 0)),
        out_shape=jax.ShapeDtypeStruct((B, G, R, T, HEAD_DIM), F32),
        compiler_params=pltpu.CompilerParams(
            dimension_semantics=("arbitrary", "arbitrary", "arbitrary"), vmem_limit_bytes=VMEM_LIMIT),
        name="flash",
    )(*args)


def _moe_kernel(be_ref, nb_ref, x_ref, w1_ref, b1_ref, w2_ref, b2_ref, o_ref, w1b_ref, w2b_ref):
    i = pl.program_id(0)
    prev = be_ref[jnp.maximum(i - 1, 0)]

    @pl.when(jnp.logical_or(i == 0, be_ref[i] != prev))
    def _():
        w1b_ref[...] = w1_ref[0].astype(BF16)
        w2b_ref[...] = w2_ref[0].astype(BF16)

    @pl.when(i < nb_ref[0])
    def _():
        gu = jnp.dot(x_ref[...], w1b_ref[...], preferred_element_type=F32) + b1_ref[0]
        g = jnp.minimum(gu[:, :D_FF], SWIGLU_LIMIT)
        up = jnp.clip(gu[:, D_FF:], -SWIGLU_LIMIT, SWIGLU_LIMIT)
        h = (up + 1) * g * jax.nn.sigmoid(SWIGLU_ALPHA * g)
        o_ref[...] = jnp.dot(h.astype(BF16), w2b_ref[...], preferred_element_type=F32) + b2_ref[0]

    @pl.when(i >= nb_ref[0])
    def _():
        o_ref[...] = jnp.zeros_like(o_ref)


def _moe_experts(xs, blk_e, n_used, w1, b1, w2, b2):
    n_blocks = xs.shape[0] // MOE_TM
    grid_spec = pltpu.PrefetchScalarGridSpec(
        num_scalar_prefetch=2,
        grid=(n_blocks,),
        in_specs=[pl.BlockSpec((MOE_TM, D_MODEL), lambda i, be, nb: (i, 0)),
                  pl.BlockSpec((1, D_MODEL, 2 * D_FF), lambda i, be, nb: (be[i], 0, 0)),
                  pl.BlockSpec((1, 1, 2 * D_FF), lambda i, be, nb: (be[i], 0, 0)),
                  pl.BlockSpec((1, D_FF, D_MODEL), lambda i, be, nb: (be[i], 0, 0)),
                  pl.BlockSpec((1, 1, D_MODEL), lambda i, be, nb: (be[i], 0, 0))],
        out_specs=pl.BlockSpec((MOE_TM, D_MODEL), lambda i, be, nb: (i, 0)),
        scratch_shapes=[pltpu.VMEM((D_MODEL, 2 * D_FF), BF16), pltpu.VMEM((D_FF, D_MODEL), BF16)],
    )
    return pl.pallas_call(
        _moe_kernel,
        grid_spec=grid_spec,
        out_shape=jax.ShapeDtypeStruct((n_blocks * MOE_TM, D_MODEL), F32),
        compiler_params=pltpu.CompilerParams(
            dimension_semantics=("arbitrary",), vmem_limit_bytes=VMEM_LIMIT),
        name="moe_experts",
    )(blk_e, n_used, xs, w1, b1.reshape(N_EXPERTS, 1, 2 * D_FF), w2, b2.reshape(N_EXPERTS, 1, D_MODEL))


def _moe(xf, w_r, b_r, w1, b1, w2, b2):
    N = xf.shape[0]
    n_rows = N * TOP_K
    logits = _mm(xf, w_r, b_r, precise=True)
    top_v, top_i = lax.top_k(logits, TOP_K)
    gate = jax.nn.softmax(top_v, axis=-1)
    flat_e = top_i.reshape(-1)
    order = jnp.argsort(flat_e)
    sorted_e = flat_e[order]
    counts = jnp.bincount(flat_e, length=N_EXPERTS)
    padded = (counts + MOE_TM - 1) // MOE_TM * MOE_TM
    pad_end = jnp.cumsum(padded)
    pad_start = pad_end - padded
    start = jnp.cumsum(counts) - counts
    dest = (pad_start[sorted_e] + jnp.arange(n_rows) - start[sorted_e]).astype(jnp.int32)
    n_blocks = -(-(n_rows + N_EXPERTS * (MOE_TM - 1)) // MOE_TM)
    row_tok = jnp.zeros((n_blocks * MOE_TM,), jnp.int32).at[dest].set((order // TOP_K).astype(jnp.int32))
    xs = xf.astype(BF16)[row_tok]
    n_used = (pad_end[-1] // MOE_TM).astype(jnp.int32)
    blk = jnp.minimum(jnp.arange(n_blocks), n_used - 1) * MOE_TM
    blk_e = jnp.minimum(jnp.searchsorted(pad_end, blk, side='right'), N_EXPERTS - 1).astype(jnp.int32)
    ys = _moe_experts(xs, blk_e, n_used.reshape(1), w1, b1, w2, b2)
    pos = jnp.zeros((n_rows,), jnp.int32).at[order].set(dest)
    y = ys[pos].reshape(N, TOP_K, D_MODEL) * gate[:, :, None]
    return jnp.sum(y, axis=1)


def _rmsnorm(x, g):
    y = x * lax.rsqrt(jnp.mean(x * x, axis=-1, keepdims=True) + EPS)
    return y * g


def _modulated_norm(x, g, shift, scale):
    return _rmsnorm(x, g) * (1 + scale[:, None, :]) + shift[:, None, :]


def _split(x, sizes):
    return jnp.split(x, np.cumsum(sizes)[:-1].tolist(), axis=-1)


def _gather_pages(pool, page_table):
    g = pool[page_table]
    return g.reshape((g.shape[0], g.shape[1] * g.shape[2]) + g.shape[3:])


def _attend(q, k, v, mask, bias=None):
    B, Q, H, D = q.shape
    G = k.shape[2]
    s = jnp.einsum('bqgnd,bkgd->bgnqk', q.reshape(B, Q, G, H // G, D), k).astype(F32) * D ** -0.5
    if bias is not None:
        s = s + bias
    p = jax.nn.softmax(jnp.where(mask, s, NEG_INF), axis=-1).astype(v.dtype)
    return jnp.einsum('bgnqk,bkgd->bqgnd', p, v).reshape(B, Q, H, D)


def _window_mask(q_pos, k_pos):
    d = q_pos[:, None] - k_pos[None, :]
    return (d >= 0) & (d < WINDOW) & (k_pos[None, :] >= 0)


def _to_heads(x, G):
    B, T, _ = x.shape
    return x.astype(BF16).reshape(B, T, G, HEAD_DIM).transpose(0, 2, 1, 3)


def _q_to_heads(q, G):
    B, T, W = q.shape
    R = W // HEAD_DIM // G
    return (q * HEAD_DIM ** -0.5).astype(BF16).reshape(B, T, G, R, HEAD_DIM).transpose(0, 2, 3, 1, 4)


def _from_heads(o):
    B, G, R, T, D = o.shape
    return o.transpose(0, 3, 1, 2, 4).reshape(B, T, G * R * D)


def _nsa_compress(k, pe, w1, w2):
    B, T, G, D = k.shape
    r = CMP_BLK // CMP_STRIDE
    nch = T // CMP_STRIDE
    n = nch - r + 1
    ch = k[:, :nch * CMP_STRIDE].reshape(B, nch, CMP_STRIDE, G, D)
    blocks = jnp.concatenate([ch[:, i:i + n] for i in range(r)], axis=2) + pe[:, None, :]
    flat = blocks.transpose(0, 1, 3, 2, 4).reshape(B * n * G, CMP_BLK * D)
    hid = _mm(flat, w1, act='gelu', out_dtype=BF16)
    return _mm(hid, w2).reshape(B, n, G, D)


def _cmp_attention(q, kc, vc, q_pos):
    B, Q, H, D = q.shape
    N, G = kc.shape[1], kc.shape[2]
    c_end = jnp.arange(N) * CMP_STRIDE + CMP_BLK
    valid = c_end[None, :] <= q_pos[:, None] + 1
    s = jnp.einsum('bqgnd,bcgd->bgnqc', q.reshape(B, Q, G, H // G, D), kc).astype(F32) * D ** -0.5
    p = jax.nn.softmax(jnp.where(valid, s, NEG_INF), axis=-1) * valid
    o = jnp.einsum('bgnqc,bcgd->bqgnd', p.astype(vc.dtype), vc).reshape(B, Q, H, D)
    return o, p


def _cmp_to_sel(n_cmp, n_sel):
    c0 = jnp.arange(n_cmp)[:, None] * CMP_STRIDE
    s0 = jnp.arange(n_sel)[None, :] * SEL_BLK
    ov = jnp.minimum(c0 + CMP_BLK, s0 + SEL_BLK) - jnp.maximum(c0, s0)
    return jnp.maximum(ov, 0).astype(F32) / CMP_BLK


def _selection_scores(p_cmp, q_pos, n_sel):
    imp = jnp.einsum('bgnqc,cs->bgqs', p_cmp, _cmp_to_sel(p_cmp.shape[-1], n_sel))
    blk = jnp.arange(n_sel)[None, :]
    cur = (q_pos // SEL_BLK)[:, None]
    valid = blk * SEL_BLK <= q_pos[:, None]
    forced = (blk == 0) | (blk == cur) | (blk == cur - 1)
    return jnp.where(valid, jnp.where(forced, BIG, imp), -BIG)


def _selection_mask(score):
    n_sel = score.shape[-1]
    a = score[..., :, None]
    b = score[..., None, :]
    lower = jnp.arange(n_sel)[None, :] < jnp.arange(n_sel)[:, None]
    beats = (b > a) | ((b == a) & lower)
    rank = jnp.sum(beats, axis=-1)
    return (rank < min(SEL_TOPN, n_sel)).astype(F32)


def _selected_attention_gather(q, k, v, p_cmp, q_pos):
    B, Q, H, D = q.shape
    T_all, G = k.shape[1], k.shape[2]
    n_sel = -(-T_all // SEL_BLK)
    score = _selection_scores(p_cmp, q_pos, n_sel)
    _, idx = lax.top_k(score, min(SEL_TOPN, n_sel))
    tok = (idx[..., None] * SEL_BLK + jnp.arange(SEL_BLK)).reshape(B, G, Q, -1)
    pad = ((0, 0), (0, n_sel * SEL_BLK - T_all), (0, 0), (0, 0))
    kt = jnp.pad(k, pad).transpose(0, 2, 1, 3)
    vt = jnp.pad(v, pad).transpose(0, 2, 1, 3)
    b_ix = jnp.arange(B)[:, None, None]
    g_ix = jnp.arange(G)[None, :, None]
    L = tok.shape[3]
    flat = tok.reshape(B, G, Q * L)
    ks = kt[b_ix, g_ix, flat].reshape(B, G, Q, L, D)
    vs = vt[b_ix, g_ix, flat].reshape(B, G, Q, L, D)
    s = jnp.einsum('bqgnd,bgqld->bgnql', q.reshape(B, Q, G, H // G, D), ks).astype(F32) * D ** -0.5
    mask = (tok <= q_pos[None, None, :, None])[:, :, None]
    p = jax.nn.softmax(jnp.where(mask, s, NEG_INF), axis=-1)
    return jnp.einsum('bgnql,bgqld->bqgnd', p, vs).reshape(B, Q, H, D)


def _gla(q, k, v, log_a, s0, chunk):
    B, T, H, K = q.shape
    V = v.shape[-1]
    n = T // chunk

    def to_chunks(a):
        return jnp.moveaxis(a.reshape((B, n, chunk) + a.shape[2:]), 1, 0)

    qc, kc, vc, gc = to_chunks(q), to_chunks(k), to_chunks(v), to_chunks(log_a)
    b = jnp.cumsum(gc, axis=2)
    b_last = b[:, :, -1]
    q_dec = qc * jnp.exp(b)
    k_inv = kc * jnp.exp(-b)
    k_end = kc * jnp.exp(b_last[:, :, None] - b)
    causal = jnp.tril(jnp.ones((chunk, chunk), bool))
    attn = jnp.where(causal, jnp.einsum('nbthk,nbshk->nbhts', q_dec, k_inv), 0.0)
    o_intra = jnp.einsum('nbhts,nbshv->nbthv', attn, vc)

    def step(S, inp):
        q_i, ke_i, v_i, bl_i = inp
        o_i = jnp.einsum('bthk,bhkv->bthv', q_i, S)
        S = jnp.exp(bl_i)[..., None] * S + jnp.einsum('bshk,bshv->bhkv', ke_i, v_i)
        return S, o_i

    s_final, o_inter = lax.scan(step, s0, (q_dec, k_end, vc, b_last))
    o = jnp.moveaxis(o_intra + o_inter, 0, 1).reshape(B, T, H, V)
    return o, s_final


def _complex_affine_combine(e1, e2):
    a1r, a1i, b1r, b1i = e1
    a2r, a2i, b2r, b2i = e2
    return (a2r * a1r - a2i * a1i, a2r * a1i + a2i * a1r,
            a2r * b1r - a2i * b1i + b2r, a2r * b1i + a2i * b1r + b2i)


def _s5(u, a_re, a_im, b_re, b_im, c_re, c_im, d, log_dt, glu_w, glu_b, h0):
    B, T, _ = u.shape
    uf = u.reshape(B, T, S5_GROUPS, S5_CH)
    dt = jnp.exp(log_dt)[:, None]
    mag = jnp.exp(a_re * dt)
    ab_re, ab_im = mag * jnp.cos(a_im * dt), mag * jnp.sin(a_im * dt)
    den = a_re * a_re + a_im * a_im
    coef_re = ((ab_re - 1) * a_re + ab_im * a_im) / den
    coef_im = (ab_im * a_re - (ab_re - 1) * a_im) / den
    bu_re = jnp.einsum('gpc,btgc->btgp', b_re, uf)
    bu_im = jnp.einsum('gpc,btgc->btgp', b_im, uf)
    x_re = coef_re * bu_re - coef_im * bu_im
    x_im = coef_re * bu_im + coef_im * bu_re
    if h0 is not None:
        x_re = x_re.at[:, 0].add(ab_re * h0[..., 0] - ab_im * h0[..., 1])
        x_im = x_im.at[:, 0].add(ab_re * h0[..., 1] + ab_im * h0[..., 0])
    A_re = jnp.broadcast_to(ab_re, x_re.shape)
    A_im = jnp.broadcast_to(ab_im, x_im.shape)
    _, _, h_re, h_im = lax.associative_scan(_complex_affine_combine, (A_re, A_im, x_re, x_im), axis=1)
    y = (jnp.einsum('gcp,btgp->btgc', c_re, h_re) - jnp.einsum('gcp,btgp->btgc', c_im, h_im) + d * uf)
    y = jax.nn.gelu(y.reshape(B, T, S5_WIDTH))
    y = y * jax.nn.sigmoid(_mm(y.reshape(B * T, S5_WIDTH), glu_w, glu_b).reshape(B, T, S5_WIDTH))
    return y, jnp.stack([h_re[:, -1], h_im[:, -1]], axis=-1)


def _mixer_ab(h, w_in, w_out, cmp_pe, cmp_w1, cmp_w2, gla_wa, gla_ba, gla_g, past):
    B, T, _ = h.shape
    proj = _mm(h.reshape(B * T, D_MODEL), w_in).reshape(B, T, -1)
    q, kv6, gate_logit, gq, gk, gv, gr, ga = _split(proj, AB_SPLITS)
    kv6 = kv6.reshape(B, T, 6, NSA_KV_HEADS, HEAD_DIM)
    new_nsa, new_win = kv6[:, :, :4], kv6[:, :, 4:]
    s0 = jnp.zeros((B, GLA_HEADS, GLA_DK, GLA_DV), F32)
    if past is None:
        t0 = 0
        kv_full, kv_win = new_nsa, new_win
    else:
        past_nsa, win_buf, s0 = past
        t0 = past_nsa.shape[1]
        kv_full = jnp.concatenate([past_nsa, new_nsa], axis=1)
        kv_win = jnp.concatenate([win_buf, new_win], axis=1)
    q_pos = t0 + jnp.arange(T)
    q4 = q.reshape(B, T, NSA_HEADS, HEAD_DIM)
    kc = _nsa_compress(kv_full[:, :, 0], cmp_pe[0], cmp_w1[0], cmp_w2[0])
    vc = _nsa_compress(kv_full[:, :, 1], cmp_pe[1], cmp_w1[1], cmp_w2[1])
    o_cmp, p_cmp = _cmp_attention(q4, kc, vc, q_pos)
    if past is None:
        n_sel = -(-T // SEL_BLK)
        sel = _selection_mask(_selection_scores(p_cmp, q_pos, n_sel))
        qh = _q_to_heads(q, NSA_KV_HEADS)
        slc_k = _to_heads(kv_full[:, :, 2].reshape(B, T, NSA_KVW), NSA_KV_HEADS)
        slc_v = _to_heads(kv_full[:, :, 3].reshape(B, T, NSA_KVW), NSA_KV_HEADS)
        win_k = _to_heads(kv_win[:, :, 0].reshape(B, T, NSA_KVW), NSA_KV_HEADS)
        win_v = _to_heads(kv_win[:, :, 1].reshape(B, T, NSA_KVW), NSA_KV_HEADS)
        o_slc = _from_heads(_flash(qh, slc_k, slc_v, tq=128, tk=512, sel=sel)).reshape(B, T, NSA_HEADS, HEAD_DIM)
        o_win = _from_heads(_flash(qh, win_k, win_v, tq=128, tk=512, window=WINDOW)).reshape(B, T, NSA_HEADS, HEAD_DIM)
        win_state = kv_win[:, -min(WINDOW, T):]
    else:
        o_slc = _selected_attention_gather(q4, kv_full[:, :, 2], kv_full[:, :, 3], p_cmp, q_pos)
        wb = past[1].shape[1]
        k_pos = t0 - wb + jnp.arange(kv_win.shape[1])
        o_win = _attend(q4, kv_win[:, :, 0], kv_win[:, :, 1], _window_mask(q_pos, k_pos))
        win_state = kv_win[:, -wb:]
    g = jax.nn.sigmoid(gate_logit).reshape(B, T, NSA_HEADS, 3, 1)
    o_nsa = (g[:, :, :, 0] * o_cmp + g[:, :, :, 1] * o_slc + g[:, :, :, 2] * o_win).reshape(B, T, NSA_WIDTH)
    gq = gq.reshape(B, T, GLA_HEADS, GLA_DK) * GLA_DK ** -0.5
    gk = gk.reshape(B, T, GLA_HEADS, GLA_DK)
    gv = gv.reshape(B, T, GLA_HEADS, GLA_DV)
    log_a = jax.nn.log_sigmoid(_mm(ga.reshape(B * T, GLA_LOWRANK), gla_wa, gla_ba)).reshape(B, T, GLA_HEADS, GLA_DK) / GLA_TAU
    chunk = GLA_CHUNK if T % GLA_CHUNK == 0 else T
    o, s_new = _gla(gq, gk, gv, log_a, s0, chunk)
    o_gla = _rmsnorm(o, gla_g).reshape(B, T, GLA_WIDTH) * jax.nn.silu(gr)
    y = _mm(jnp.concatenate([o_nsa, o_gla], axis=-1).reshape(B * T, -1), w_out).reshape(B, T, D_MODEL)
    return y, (new_nsa, win_state, s_new)


def _mixer_cd(h, w_in, w_out, b_f, a_re, a_im, b_re, b_im, c_re, c_im, d, log_dt, glu_w, glu_b, past):
    B, T, _ = h.shape
    proj = _mm(h.reshape(B * T, D_MODEL), w_in).reshape(B, T, -1)
    q, k, v, f, u = _split(proj, CD_SPLITS)
    log_f = jax.nn.log_sigmoid(f + b_f)
    new_kv = jnp.stack([k.reshape(B, T, FOX_HEADS, HEAD_DIM), v.reshape(B, T, FOX_HEADS, HEAD_DIM)], axis=2)
    h0 = None
    if past is None:
        F = jnp.cumsum(log_f, axis=1)
        Ft = F.transpose(0, 2, 1)
        qb = jnp.broadcast_to(Ft[..., None], (B, FOX_HEADS, T, LANE))
        o = _flash(_q_to_heads(q, FOX_HEADS), _to_heads(k, FOX_HEADS), _to_heads(v, FOX_HEADS),
                   tq=256, tk=512, qbias=qb, kbias=Ft)
        o_fox = _from_heads(o)
    else:
        past_kv, past_lf, h0 = past
        t0 = past_kv.shape[1]
        k_all = jnp.concatenate([past_kv[:, :, 0], k.reshape(B, T, FOX_HEADS, HEAD_DIM)], axis=1)
        v_all = jnp.concatenate([past_kv[:, :, 1], v.reshape(B, T, FOX_HEADS, HEAD_DIM)], axis=1)
        lf_all = jnp.concatenate([past_lf, log_f], axis=1)
        F = jnp.cumsum(lf_all, axis=1)
        Fq = F[:, t0:]
        q_pos = t0 + jnp.arange(T)
        k_pos = jnp.arange(k_all.shape[1])
        bias = (Fq[:, :, None, :] - F[:, None, :, :]).transpose(0, 3, 1, 2)[:, :, None]
        o_fox = _attend(q.reshape(B, T, FOX_HEADS, HEAD_DIM), k_all, v_all,
                        k_pos[None, :] <= q_pos[:, None], bias).reshape(B, T, FOX_WIDTH)
    y_s5, s5_state = _s5(u, a_re, a_im, b_re, b_im, c_re, c_im, d, log_dt, glu_w, glu_b, h0)
    y = _mm(jnp.concatenate([o_fox, y_s5], axis=-1).reshape(B * T, -1), w_out).reshape(B, T, D_MODEL)
    return y, (new_kv, log_f, s5_state)


def kernel(x_prompt, x_sample, cache_ab_nsa_kv, cache_ab_win_kv, state_ab_gla, cache_cd_fox_kv, cache_cd_fox_logf, state_cd_s5, page_table, c_prompt, c_sample, ada_w, ada_b, norm_g, ab_w_in, ab_w_out, nsa_cmp_pe, nsa_cmp_w1, nsa_cmp_w2, gla_w_alpha, gla_b_alpha, gla_norm_g, cd_w_in, cd_w_out, fox_b_f, s5_a_re, s5_a_im, s5_b_re, s5_b_im, s5_c_re, s5_c_im, s5_d, s5_log_dt, s5_glu_w, s5_glu_b, router_w, router_b, moe_w1, moe_b1, moe_w2, moe_b2, final_norm_g):
    xs = [x_prompt, x_sample]
    cs = [c_prompt, c_sample]
    new = [{}, {}]
    n_prompt = x_prompt.shape[0] * x_prompt.shape[1]
    for l in range(DEPTH):
        j = l // 2
        mods = [jnp.split(_mm(jax.nn.silu(c), ada_w[l], ada_b[l]), 6, axis=-1) for c in cs]
        for grp in range(2):
            x = xs[grp]
            sh1, sc1, g1 = mods[grp][:3]
            hn = _modulated_norm(x, norm_g[l, 0], sh1, sc1)
            if l % 2 == 0:
                past = None
                if grp == 1:
                    past = (_gather_pages(cache_ab_nsa_kv[j], page_table), cache_ab_win_kv[j], state_ab_gla[j])
                y, st = _mixer_ab(hn, ab_w_in[j], ab_w_out[j], nsa_cmp_pe[j], nsa_cmp_w1[j], nsa_cmp_w2[j],
                                  gla_w_alpha[j], gla_b_alpha[j], gla_norm_g[j], past)
                names = ('nsa_kv', 'win_kv', 'gla')
            else:
                past = None
                if grp == 1:
                    past = (_gather_pages(cache_cd_fox_kv[j], page_table),
                            _gather_pages(cache_cd_fox_logf[j], page_table), state_cd_s5[j])
                y, st = _mixer_cd(hn, cd_w_in[j], cd_w_out[j], fox_b_f[j], s5_a_re[j], s5_a_im[j], s5_b_re[j],
                                  s5_b_im[j], s5_c_re[j], s5_c_im[j], s5_d[j], s5_log_dt[j], s5_glu_w[j],
                                  s5_glu_b[j], past)
                names = ('fox_kv', 'fox_logf', 's5')
            for name, s in zip(names, st):
                new[grp].setdefault(name, []).append(s)
            xs[grp] = x + g1[:, None, :] * y
        hn2 = [_modulated_norm(xs[grp], norm_g[l, 1], mods[grp][3], mods[grp][4]).reshape(-1, D_MODEL)
               for grp in range(2)]
        ym = _moe(jnp.concatenate(hn2, axis=0), router_w[l], router_b[l], moe_w1[l], moe_b1[l], moe_w2[l], moe_b2[l])
        yms = [ym[:n_prompt], ym[n_prompt:]]
        for grp in range(2):
            xs[grp] = xs[grp] + mods[grp][5][:, None, :] * yms[grp].reshape(xs[grp].shape)
    ys = [_rmsnorm(x, final_norm_g) for x in xs]
    names = ('nsa_kv', 'win_kv', 'gla', 'fox_kv', 'fox_logf', 's5')
    outs = [jnp.stack(new[grp][name]) for grp in range(2) for name in names]
    return (ys[0], ys[1]) + tuple(outs)
```

```python
import functools
import math

import numpy as np
import jax
import jax.numpy as jnp
from jax import lax
from jax.experimental import pallas as pl
from jax.experimental.pallas import tpu as pltpu

D_MODEL = 1024
DEPTH = 2
PAGE_SIZE = 128
HEAD_DIM = 64
NSA_HEADS = 8
NSA_KV_HEADS = 2
CMP_BLK = 32
CMP_STRIDE = 16
SEL_BLK = 64
SEL_TOPN = 16
WINDOW = 512
GLA_HEADS = 4
GLA_DK = 64
GLA_DV = 128
GLA_LOWRANK = 16
GLA_TAU = 16.0
GLA_CHUNK = 64
FOX_HEADS = 8
S5_GROUPS = 32
S5_CH = 16
S5_STATE = 64
N_EXPERTS = 32
TOP_K = 4
D_FF = D_MODEL
SWIGLU_LIMIT = 7.0
SWIGLU_ALPHA = 1.702
EPS = 1e-6
NEG_INF = -1e30
BIG = 1e9
NSA_WIDTH = NSA_HEADS * HEAD_DIM
NSA_KVW = NSA_KV_HEADS * HEAD_DIM
GLA_QK = GLA_HEADS * GLA_DK
GLA_WIDTH = GLA_HEADS * GLA_DV
FOX_WIDTH = FOX_HEADS * HEAD_DIM
S5_WIDTH = S5_GROUPS * S5_CH
AB_SPLITS = (NSA_WIDTH, 6 * NSA_KVW, 3 * NSA_HEADS, GLA_QK, GLA_QK, GLA_WIDTH, GLA_WIDTH, GLA_LOWRANK)
CD_SPLITS = (FOX_WIDTH, FOX_WIDTH, FOX_WIDTH, FOX_HEADS, S5_WIDTH)

LANE = 128
SUBLANE = 8
VMEM_LIMIT = 56 * 1024 * 1024
MOE_TM = 256

F32 = jnp.float32
BF16 = jnp.bfloat16


def _round_up(n, m):
    return -(-n // m) * m


def _pick_tile(n, candidates):
    for c in candidates:
        if n % c == 0:
            return c
    return n


def _mm_kernel(x_ref, w_ref, b_ref, o_ref, wb_ref, *, act, precise):
    if precise:
        acc = jnp.dot(x_ref[...], w_ref[...], preferred_element_type=F32, precision=lax.Precision.HIGHEST)
    else:
        @pl.when(pl.program_id(1) == 0)
        def _():
            wb_ref[...] = w_ref[...].astype(BF16)

        acc = jnp.dot(x_ref[...].astype(BF16), wb_ref[...], preferred_element_type=F32)
    acc = acc + b_ref[...]
    if act == 'gelu':
        acc = jax.nn.gelu(acc)
    o_ref[...] = acc.astype(o_ref.dtype)


def _mm(x, w, b=None, act=None, out_dtype=F32, precise=False):
    M, K = x.shape
    N = w.shape[1]
    Np = _round_up(N, LANE)
    Mp = _round_up(M, 512) if M > 256 else _round_up(M, SUBLANE)
    tm = min(Mp, 512)
    tn = _pick_tile(Np, (512, 384, 256, 128))
    if Np != N:
        w = jnp.pad(w, ((0, 0), (0, Np - N)))
    if b is None:
        b = jnp.zeros((N,), F32)
    b = jnp.pad(b.astype(F32), (0, Np - N)).reshape(1, Np)
    if Mp != M:
        x = jnp.pad(x, ((0, Mp - M), (0, 0)))
    out = pl.pallas_call(
        functools.partial(_mm_kernel, act=act, precise=precise),
        grid=(Np // tn, Mp // tm),
        in_specs=[pl.BlockSpec((tm, K), lambda j, i: (i, 0)),
                  pl.BlockSpec((K, tn), lambda j, i: (0, j)),
                  pl.BlockSpec((1, tn), lambda j, i: (0, j))],
        out_specs=pl.BlockSpec((tm, tn), lambda j, i: (i, j)),
        out_shape=jax.ShapeDtypeStruct((Mp, Np), out_dtype),
        scratch_shapes=[pltpu.VMEM((K, tn), BF16)],
        compiler_params=pltpu.CompilerParams(
            dimension_semantics=("arbitrary", "arbitrary"), vmem_limit_bytes=VMEM_LIMIT),
        name="mm",
    )(x, w, b)
    return out[:M, :N]


def _flash_kernel(*refs, tq, tk, R, window, use_bias, use_sel):
    q_ref, k_ref, v_ref = refs[:3]
    n = 3
    if use_bias:
        qb_ref, kb_ref = refs[n:n + 2]
        n += 2
    if use_sel:
        sel_ref = refs[n]
        n += 1
    o_ref = refs[n]
    i = pl.program_id(2)
    rows = R * tq
    q = q_ref[0, 0].reshape(rows, HEAD_DIM)
    q_pos = i * tq + jnp.bitwise_and(lax.broadcasted_iota(jnp.int32, (rows, 1), 0), tq - 1)
    hi = ((i + 1) * tq + tk - 1) // tk
    lo = jnp.maximum(i * tq - window + 1, 0) // tk if window else 0
    if use_bias:
        qb = qb_ref[0, 0]
        qb = jnp.concatenate([qb] * (tk // LANE), axis=1)
    if use_sel:
        sel = sel_ref[0, 0].astype(BF16)
        n_sel = sel.shape[1]

    def body(j, carry):
        m, l, acc = carry
        start = pl.multiple_of(j * tk, tk)
        k = k_ref[0, 0, pl.ds(start, tk), :]
        v = v_ref[0, 0, pl.ds(start, tk), :]
        s = lax.dot_general(q, k, (((1,), (1,)), ((), ())), preferred_element_type=F32)
        k_pos = j * tk + lax.broadcasted_iota(jnp.int32, (1, tk), 1)
        if use_bias:
            s = s + qb - kb_ref[0, 0, j]
        s = jnp.where(k_pos <= q_pos, s, NEG_INF)
        if window:
            s = jnp.where(q_pos - k_pos < window, s, NEG_INF)
        if use_sel:
            blk_of_key = (j * tk + lax.broadcasted_iota(jnp.int32, (n_sel, tk), 1)) // SEL_BLK
            expand = jnp.where(blk_of_key == lax.broadcasted_iota(jnp.int32, (n_sel, tk), 0), 1.0, 0.0).astype(BF16)
            chosen = jnp.dot(sel, expand, preferred_element_type=F32)
            chosen = jnp.concatenate([chosen] * R, axis=0)
            s = jnp.where(chosen > 0.5, s, NEG_INF)
        m_new = jnp.maximum(m, jnp.max(s, axis=1, keepdims=True))
        p = jnp.where(s > 0.5 * NEG_INF, jnp.exp(s - m_new), 0.0)
        alpha = jnp.exp(m - m_new)
        l = alpha * l + jnp.sum(p, axis=1, keepdims=True)
        acc = alpha * acc + jnp.dot(p.astype(BF16), v, preferred_element_type=F32)
        return m_new, l, acc

    m0 = jnp.full((rows, 1), NEG_INF, F32)
    l0 = jnp.zeros((rows, 1), F32)
    a0 = jnp.zeros((rows, HEAD_DIM), F32)
    m, l, acc = lax.fori_loop(lo, hi, body, (m0, l0, a0))
    o_ref[0, 0] = (acc / l).reshape(R, tq, HEAD_DIM)


def _flash(q, k, v, *, tq, tk, window=0, qbias=None, kbias=None, sel=None):
    B, G, R, T, _ = q.shape
    grid = (B, G, T // tq)
    in_specs = [pl.BlockSpec((1, 1, R, tq, HEAD_DIM), lambda b, g, i: (b, g, 0, i, 0)),
                pl.BlockSpec((1, 1, T, HEAD_DIM), lambda b, g, i: (b, g, 0, 0)),
                pl.BlockSpec((1, 1, T, HEAD_DIM), lambda b, g, i: (b, g, 0, 0))]
    args = [q, k, v]
    if qbias is not None:
        in_specs += [pl.BlockSpec((1, 1, tq, LANE), lambda b, g, i: (b, g, i, 0)),
                     pl.BlockSpec((1, 1, T // tk, 1, tk), lambda b, g, i: (b, g, 0, 0, 0))]
        args += [qbias, kbias.reshape(B, G, T // tk, 1, tk)]
    if sel is not None:
        in_specs += [pl.BlockSpec((1, 1, tq, sel.shape[-1]), lambda b, g, i: (b, g, i, 0))]
        args += [sel]
    return pl.pallas_call(
        functools.partial(_flash_kernel, tq=tq, tk=tk, R=R, window=window,
                          use_bias=qbias is not None, use_sel=sel is not None),
        grid=grid,
        in_specs=in_specs,
        out_specs=pl.BlockSpec((1, 1, R, tq, HEAD_DIM), lambda b, g, i: (b, g, 0, i, 0)),
        out_shape=jax.ShapeDtypeStruct((B, G, R, T, HEAD_DIM), F32),
        compiler_params=pltpu.CompilerParams(
            dimension_semantics=("arbitrary", "arbitrary", "arbitrary"), vmem_limit_bytes=VMEM_LIMIT),
        name="flash",
    )(*args)


def _moe_kernel(be_ref, nb_ref, x_ref, w1_ref, b1_ref, w2_ref, b2_ref, o_ref, w1b_ref, w2b_ref):
    i = pl.program_id(0)
    prev = be_ref[jnp.maximum(i - 1, 0)]

    @pl.when(jnp.logical_or(i == 0, be_ref[i] != prev))
    def _():
        w1b_ref[...] = w1_ref[0].astype(BF16)
        w2b_ref[...] = w2_ref[0].astype(BF16)

    @pl.when(i < nb_ref[0])
    def _():
        gu = jnp.dot(x_ref[...], w1b_ref[...], preferred_element_type=F32) + b1_ref[0]
        g = jnp.minimum(gu[:, :D_FF], SWIGLU_LIMIT)
        up = jnp.clip(gu[:, D_FF:], -SWIGLU_LIMIT, SWIGLU_LIMIT)
        h = (up + 1) * g * jax.nn.sigmoid(SWIGLU_ALPHA * g)
        o_ref[...] = jnp.dot(h.astype(BF16), w2b_ref[...], preferred_element_type=F32) + b2_ref[0]

    @pl.when(i >= nb_ref[0])
    def _():
        o_ref[...] = jnp.zeros_like(o_ref)


def _moe_experts(xs, blk_e, n_used, w1, b1, w2, b2):
    n_blocks = xs.shape[0] // MOE_TM
    grid_spec = pltpu.PrefetchScalarGridSpec(
        num_scalar_prefetch=2,
        grid=(n_blocks,),
        in_specs=[pl.BlockSpec((MOE_TM, D_MODEL), lambda i, be, nb: (i, 0)),
                  pl.BlockSpec((1, D_MODEL, 2 * D_FF), lambda i, be, nb: (be[i], 0, 0)),
                  pl.BlockSpec((1, 1, 2 * D_FF), lambda i, be, nb: (be[i], 0, 0)),
                  pl.BlockSpec((1, D_FF, D_MODEL), lambda i, be, nb: (be[i], 0, 0)),
                  pl.BlockSpec((1, 1, D_MODEL), lambda i, be, nb: (be[i], 0, 0))],
        out_specs=pl.BlockSpec((MOE_TM, D_MODEL), lambda i, be, nb: (i, 0)),
        scratch_shapes=[pltpu.VMEM((D_MODEL, 2 * D_FF), BF16), pltpu.VMEM((D_FF, D_MODEL), BF16)],
    )
    return pl.pallas_call(
        _moe_kernel,
        grid_spec=grid_spec,
        out_shape=jax.ShapeDtypeStruct((n_blocks * MOE_TM, D_MODEL), F32),
        compiler_params=pltpu.CompilerParams(
            dimension_semantics=("arbitrary",), vmem_limit_bytes=VMEM_LIMIT),
        name="moe_experts",
    )(blk_e, n_used, xs, w1, b1.reshape(N_EXPERTS, 1, 2 * D_FF), w2, b2.reshape(N_EXPERTS, 1, D_MODEL))


def _moe(xf, w_r, b_r, w1, b1, w2, b2):
    N = xf.shape[0]
    n_rows = N * TOP_K
    logits = _mm(xf, w_r, b_r, precise=True)
    top_v, top_i = lax.top_k(logits, TOP_K)
    gate = jax.nn.softmax(top_v, axis=-1)
    flat_e = top_i.reshape(-1)
    order = jnp.argsort(flat_e)
    sorted_e = flat_e[order]
    counts = jnp.bincount(flat_e, length=N_EXPERTS)
    padded = (counts + MOE_TM - 1) // MOE_TM * MOE_TM
    pad_end = jnp.cumsum(padded)
    pad_start = pad_end - padded
    start = jnp.cumsum(counts) - counts
    dest = (pad_start[sorted_e] + jnp.arange(n_rows) - start[sorted_e]).astype(jnp.int32)
    n_blocks = -(-(n_rows + N_EXPERTS * (MOE_TM - 1)) // MOE_TM)
    row_tok = jnp.zeros((n_blocks * MOE_TM,), jnp.int32).at[dest].set((order // TOP_K).astype(jnp.int32))
    xs = xf.astype(BF16)[row_tok]
    n_used = (pad_end[-1] // MOE_TM).astype(jnp.int32)
    blk = jnp.minimum(jnp.arange(n_blocks), n_used - 1) * MOE_TM
    blk_e = jnp.minimum(jnp.searchsorted(pad_end, blk, side='right'), N_EXPERTS - 1).astype(jnp.int32)
    ys = _moe_experts(xs, blk_e, n_used.reshape(1), w1, b1, w2, b2)
    pos = jnp.zeros((n_rows,), jnp.int32).at[order].set(dest)
    y = ys[pos].reshape(N, TOP_K, D_MODEL) * gate[:, :, None]
    return jnp.sum(y, axis=1)


def _rmsnorm(x, g):
    y = x * lax.rsqrt(jnp.mean(x * x, axis=-1, keepdims=True) + EPS)
    return y * g


def _modulated_norm(x, g, shift, scale):
    return _rmsnorm(x, g) * (1 + scale[:, None, :]) + shift[:, None, :]


def _split(x, sizes):
    return jnp.split(x, np.cumsum(sizes)[:-1].tolist(), axis=-1)


def _gather_pages(pool, page_table):
    g = pool[page_table]
    return g.reshape((g.shape[0], g.shape[1] * g.shape[2]) + g.shape[3:])


def _attend(q, k, v, mask, bias=None):
    B, Q, H, D = q.shape
    G = k.shape[2]
    s = jnp.einsum('bqgnd,bkgd->bgnqk', q.reshape(B, Q, G, H // G, D), k).astype(F32) * D ** -0.5
    if bias is not None:
        s = s + bias
    p = jax.nn.softmax(jnp.where(mask, s, NEG_INF), axis=-1).astype(v.dtype)
    return jnp.einsum('bgnqk,bkgd->bqgnd', p, v).reshape(B, Q, H, D)


def _window_mask(q_pos, k_pos):
    d = q_pos[:, None] - k_pos[None, :]
    return (d >= 0) & (d < WINDOW) & (k_pos[None, :] >= 0)


def _to_heads(x, G):
    B, T, _ = x.shape
    return x.astype(BF16).reshape(B, T, G, HEAD_DIM).transpose(0, 2, 1, 3)


def _q_to_heads(q, G):
    B, T, W = q.shape
    R = W // HEAD_DIM // G
    return (q * HEAD_DIM ** -0.5).astype(BF16).reshape(B, T, G, R, HEAD_DIM).transpose(0, 2, 3, 1, 4)


def _from_heads(o):
    B, G, R, T, D = o.shape
    return o.transpose(0, 3, 1, 2, 4).reshape(B, T, G * R * D)


def _nsa_compress(k, pe, w1, w2):
    B, T, G, D = k.shape
    r = CMP_BLK // CMP_STRIDE
    nch = T // CMP_STRIDE
    n = nch - r + 1
    ch = k[:, :nch * CMP_STRIDE].reshape(B, nch, CMP_STRIDE, G, D)
    blocks = jnp.concatenate([ch[:, i:i + n] for i in range(r)], axis=2) + pe[:, None, :]
    flat = blocks.transpose(0, 1, 3, 2, 4).reshape(B * n * G, CMP_BLK * D)
    hid = _mm(flat, w1, act='gelu', out_dtype=BF16)
    return _mm(hid, w2).reshape(B, n, G, D)


def _cmp_attention(q, kc, vc, q_pos):
    B, Q, H, D = q.shape
    N, G = kc.shape[1], kc.shape[2]
    c_end = jnp.arange(N) * CMP_STRIDE + CMP_BLK
    valid = c_end[None, :] <= q_pos[:, None] + 1
    s = jnp.einsum('bqgnd,bcgd->bgnqc', q.reshape(B, Q, G, H // G, D), kc).astype(F32) * D ** -0.5
    p = jax.nn.softmax(jnp.where(valid, s, NEG_INF), axis=-1) * valid
    o = jnp.einsum('bgnqc,bcgd->bqgnd', p.astype(vc.dtype), vc).reshape(B, Q, H, D)
    return o, p


def _cmp_to_sel(n_cmp, n_sel):
    c0 = jnp.arange(n_cmp)[:, None] * CMP_STRIDE
    s0 = jnp.arange(n_sel)[None, :] * SEL_BLK
    ov = jnp.minimum(c0 + CMP_BLK, s0 + SEL_BLK) - jnp.maximum(c0, s0)
    return jnp.maximum(ov, 0).astype(F32) / CMP_BLK


def _selection_scores(p_cmp, q_pos, n_sel):
    imp = jnp.einsum('bgnqc,cs->bgqs', p_cmp, _cmp_to_sel(p_cmp.shape[-1], n_sel))
    blk = jnp.arange(n_sel)[None, :]
    cur = (q_pos // SEL_BLK)[:, None]
    valid = blk * SEL_BLK <= q_pos[:, None]
    forced = (blk == 0) | (blk == cur) | (blk == cur - 1)
    return jnp.where(valid, jnp.where(forced, BIG, imp), -BIG)


def _selection_mask(score):
    n_sel = score.shape[-1]
    a = score[..., :, None]
    b = score[..., None, :]
    lower = jnp.arange(n_sel)[None, :] < jnp.arange(n_sel)[:, None]
    beats = (b > a) | ((b == a) & lower)
    rank = jnp.sum(beats, axis=-1)
    return (rank < min(SEL_TOPN, n_sel)).astype(F32)


def _selected_attention_gather(q, k, v, p_cmp, q_pos):
    B, Q, H, D = q.shape
    T_all, G = k.shape[1], k.shape[2]
    n_sel = -(-T_all // SEL_BLK)
    score = _selection_scores(p_cmp, q_pos, n_sel)
    _, idx = lax.top_k(score, min(SEL_TOPN, n_sel))
    tok = (idx[..., None] * SEL_BLK + jnp.arange(SEL_BLK)).reshape(B, G, Q, -1)
    pad = ((0, 0), (0, n_sel * SEL_BLK - T_all), (0, 0), (0, 0))
    kt = jnp.pad(k, pad).transpose(0, 2, 1, 3)
    vt = jnp.pad(v, pad).transpose(0, 2, 1, 3)
    b_ix = jnp.arange(B)[:, None, None]
    g_ix = jnp.arange(G)[None, :, None]
    L = tok.shape[3]
    flat = tok.reshape(B, G, Q * L)
    ks = kt[b_ix, g_ix, flat].reshape(B, G, Q, L, D)
    vs = vt[b_ix, g_ix, flat].reshape(B, G, Q, L, D)
    s = jnp.einsum('bqgnd,bgqld->bgnql', q.reshape(B, Q, G, H // G, D), ks).astype(F32) * D ** -0.5
    mask = (tok <= q_pos[None, None, :, None])[:, :, None]
    p = jax.nn.softmax(jnp.where(mask, s, NEG_INF), axis=-1)
    return jnp.einsum('bgnql,bgqld->bqgnd', p, vs).reshape(B, Q, H, D)


def _gla_kernel(q_ref, k_ref, v_ref, za_ref, gr_ref, g_ref, s0_ref, o_ref, sT_ref, st_ref, *, C, n_chunks):
    @pl.when(pl.program_id(1) == 0)
    def _():
        st_ref[...] = s0_ref[0]

    tri = (lax.broadcasted_iota(jnp.int32, (C, C), 0) >= lax.broadcasted_iota(jnp.int32, (C, C), 1))
    tri_f = jnp.where(tri, 1.0, 0.0)
    for c in range(n_chunks):
        rows = pl.ds(c * C, C)
        log_a = jax.nn.log_sigmoid(za_ref[0, rows, :]) / GLA_TAU
        b = jnp.dot(tri_f, log_a, preferred_element_type=F32, precision=lax.Precision.HIGHEST)
        b_last = b[C - 1:C, :]
        q = q_ref[0, rows, :] * GLA_DK ** -0.5
        k = k_ref[0, rows, :]
        q_dec = (q * jnp.exp(b)).astype(BF16)
        k_inv = (k * jnp.exp(-b)).astype(BF16)
        k_end = (k * jnp.exp(b_last - b)).astype(BF16)
        decay = jnp.exp(b_last)
        v = v_ref[0, rows, :].astype(BF16)
        gr = gr_ref[0, rows, :]
        outs = []
        for h in range(GLA_HEADS):
            kh = slice(h * GLA_DK, (h + 1) * GLA_DK)
            vh = slice(h * GLA_DV, (h + 1) * GLA_DV)
            attn = lax.dot_general(q_dec[:, kh], k_inv[:, kh], (((1,), (1,)), ((), ())), preferred_element_type=F32)
            attn = jnp.where(tri, attn, 0.0).astype(BF16)
            sT = st_ref[h]
            o = jnp.dot(attn, v[:, vh], preferred_element_type=F32)
            o = o + lax.dot_general(q_dec[:, kh], sT.astype(BF16), (((1,), (1,)), ((), ())), preferred_element_type=F32)
            st_ref[h] = decay[:, kh] * sT + lax.dot_general(v[:, vh], k_end[:, kh], (((0,), (0,)), ((), ())),
                                                            preferred_element_type=F32)
            o = o * lax.rsqrt(jnp.mean(o * o, axis=-1, keepdims=True) + EPS) * g_ref[...]
            outs.append(o * jax.nn.silu(gr[:, vh]))
        o_ref[0, rows, :] = jnp.concatenate(outs, axis=1)

    @pl.when(pl.program_id(1) == pl.num_programs(1) - 1)
    def _():
        sT_ref[0] = st_ref[...]


def _gla(gq, gk, gv, za, gr, gla_g, s0, C, n_chunks):
    B, T, _ = gq.shape
    tt = C * n_chunks
    s0T = s0.transpose(0, 1, 3, 2)
    qk_spec = pl.BlockSpec((1, tt, GLA_QK), lambda b, i: (b, i, 0))
    v_spec = pl.BlockSpec((1, tt, GLA_WIDTH), lambda b, i: (b, i, 0))
    st_spec = pl.BlockSpec((1, GLA_HEADS, GLA_DV, GLA_DK), lambda b, i: (b, 0, 0, 0))
    o, sT = pl.pallas_call(
        functools.partial(_gla_kernel, C=C, n_chunks=n_chunks),
        grid=(B, T // tt),
        in_specs=[qk_spec, qk_spec, v_spec, qk_spec, v_spec,
                  pl.BlockSpec((1, GLA_DV), lambda b, i: (0, 0)), st_spec],
        out_specs=[v_spec, st_spec],
        out_shape=[jax.ShapeDtypeStruct((B, T, GLA_WIDTH), F32),
                   jax.ShapeDtypeStruct((B, GLA_HEADS, GLA_DV, GLA_DK), F32)],
        scratch_shapes=[pltpu.VMEM((GLA_HEADS, GLA_DV, GLA_DK), F32)],
        compiler_params=pltpu.CompilerParams(
            dimension_semantics=("arbitrary", "arbitrary"), vmem_limit_bytes=VMEM_LIMIT),
        name="gla",
    )(gq, gk, gv, za, gr, gla_g.reshape(1, GLA_DV), s0T)
    return o, sT.transpose(0, 1, 3, 2)


S5_NK = 4
S5_GPK = S5_GROUPS // S5_NK
S5_HALF = S5_GPK * S5_STATE


def _s5_kernel(u_ref, wb_ref, wc_ref, coef_ref, d_ref, gw_ref, gb_ref, h0_ref, y_ref, hfin_ref, hst_ref, xb_ref,
               *, BT, Tc):
    @pl.when(pl.program_id(0) == 0)
    def _():
        hst_ref[...] = h0_ref[...]

    u = u_ref[...]
    ys = []
    for k in range(S5_NK):
        bu = jnp.dot(u[:, k * LANE:(k + 1) * LANE].astype(BF16), wb_ref[k], preferred_element_type=F32)
        bre, bim = bu[:, :S5_HALF], bu[:, S5_HALF:]
        ar, ai = coef_ref[k, 0:1, :], coef_ref[k, 1:2, :]
        cr, ci = coef_ref[k, 2:3, :], coef_ref[k, 3:4, :]
        xb_ref[:, :S5_HALF] = cr * bre - ci * bim
        xb_ref[:, S5_HALF:] = cr * bim + ci * bre

        def step(t, carry):
            hr, hi = carry
            rows = pl.ds(pl.multiple_of(t * BT, BT), BT)
            hr2 = ar * hr - ai * hi + xb_ref[rows, :S5_HALF]
            hi2 = ar * hi + ai * hr + xb_ref[rows, S5_HALF:]
            xb_ref[rows, :S5_HALF] = hr2
            xb_ref[rows, S5_HALF:] = hi2
            return hr2, hi2

        hr, hi = lax.fori_loop(0, Tc, step, (hst_ref[k, :, :S5_HALF], hst_ref[k, :, S5_HALF:]),
                               unroll=min(Tc, 8))
        hst_ref[k, :, :S5_HALF] = hr
        hst_ref[k, :, S5_HALF:] = hi
        ys.append(jnp.dot(xb_ref[...].astype(BF16), wc_ref[k], preferred_element_type=F32))
    y = jax.nn.gelu(jnp.concatenate(ys, axis=1) + d_ref[...] * u)
    z = jnp.dot(y.astype(BF16), gw_ref[...].astype(BF16), preferred_element_type=F32) + gb_ref[...]
    y_ref[...] = y * jax.nn.sigmoid(z)

    @pl.when(pl.program_id(0) == pl.num_programs(0) - 1)
    def _():
        hfin_ref[...] = hst_ref[...]


def _s5(u, a_re, a_im, b_re, b_im, c_re, c_im, d, log_dt, glu_w, glu_b, h0, Tc):
    B, T, _ = u.shape
    dt = jnp.exp(log_dt)[:, None]
    mag = jnp.exp(a_re * dt)
    ab_re, ab_im = mag * jnp.cos(a_im * dt), mag * jnp.sin(a_im * dt)
    den = a_re * a_re + a_im * a_im
    coef_re = ((ab_re - 1) * a_re + ab_im * a_im) / den
    coef_im = (ab_im * a_re - (ab_re - 1) * a_im) / den
    coefs = jnp.stack([ab_re, ab_im, coef_re, coef_im], axis=0).reshape(4, S5_NK, S5_HALF).transpose(1, 0, 2)
    eye = jnp.eye(S5_GPK, dtype=F32)

    def in_weights(bm):
        bk = bm.reshape(S5_NK, S5_GPK, S5_STATE, S5_CH)
        return jnp.einsum('kgpc,gh->kgchp', bk, eye).reshape(S5_NK, S5_GPK * S5_CH, S5_HALF)

    def out_weights(cm):
        ck = cm.reshape(S5_NK, S5_GPK, S5_CH, S5_STATE)
        return jnp.einsum('kgcp,gh->kgphc', ck, eye).reshape(S5_NK, S5_HALF, S5_GPK * S5_CH)

    wb = jnp.concatenate([in_weights(b_re), in_weights(b_im)], axis=2).astype(BF16)
    wc = jnp.concatenate([out_weights(c_re), -out_weights(c_im)], axis=1).astype(BF16)
    if h0 is None:
        hs0 = jnp.zeros((S5_NK, B, 2 * S5_HALF), F32)
    else:
        hs0 = h0.reshape(B, S5_NK, S5_HALF, 2).transpose(1, 0, 3, 2).reshape(S5_NK, B, 2 * S5_HALF)
    ut = u.transpose(1, 0, 2).reshape(T * B, S5_WIDTH)
    rows = Tc * B
    const2 = lambda i: (0, 0)
    const3 = lambda i: (0, 0, 0)
    y, hfin = pl.pallas_call(
        functools.partial(_s5_kernel, BT=B, Tc=Tc),
        grid=(T // Tc,),
        in_specs=[pl.BlockSpec((rows, S5_WIDTH), lambda i: (i, 0)),
                  pl.BlockSpec(wb.shape, const3), pl.BlockSpec(wc.shape, const3), pl.BlockSpec(coefs.shape, const3),
                  pl.BlockSpec((1, S5_WIDTH), const2), pl.BlockSpec((S5_WIDTH, S5_WIDTH), const2),
                  pl.BlockSpec((1, S5_WIDTH), const2), pl.BlockSpec(hs0.shape, const3)],
        out_specs=[pl.BlockSpec((rows, S5_WIDTH), lambda i: (i, 0)), pl.BlockSpec(hs0.shape, const3)],
        out_shape=[jax.ShapeDtypeStruct((T * B, S5_WIDTH), F32), jax.ShapeDtypeStruct(hs0.shape, F32)],
        scratch_shapes=[pltpu.VMEM(hs0.shape, F32), pltpu.VMEM((rows, 2 * S5_HALF), F32)],
        compiler_params=pltpu.CompilerParams(dimension_semantics=("arbitrary",), vmem_limit_bytes=VMEM_LIMIT),
        name="s5",
    )(ut, wb, wc, coefs, d.reshape(1, S5_WIDTH), glu_w, glu_b.reshape(1, S5_WIDTH), hs0)
    y = y.reshape(T, B, S5_WIDTH).transpose(1, 0, 2)
    hfin = hfin.reshape(S5_NK, B, 2, S5_HALF).transpose(1, 0, 3, 2).reshape(B, S5_GROUPS, S5_STATE, 2)
    return y, hfin


def _mixer_ab(h, w_in, w_out, cmp_pe, cmp_w1, cmp_w2, gla_wa, gla_ba, gla_g, past):
    B, T, _ = h.shape
    proj = _mm(h.reshape(B * T, D_MODEL), w_in).reshape(B, T, -1)
    q, kv6, gate_logit, gq, gk, gv, gr, ga = _split(proj, AB_SPLITS)
    kv6 = kv6.reshape(B, T, 6, NSA_KV_HEADS, HEAD_DIM)
    new_nsa, new_win = kv6[:, :, :4], kv6[:, :, 4:]
    s0 = jnp.zeros((B, GLA_HEADS, GLA_DK, GLA_DV), F32)
    if past is None:
        t0 = 0
        kv_full, kv_win = new_nsa, new_win
    else:
        past_nsa, win_buf, s0 = past
        t0 = past_nsa.shape[1]
        kv_full = jnp.concatenate([past_nsa, new_nsa], axis=1)
        kv_win = jnp.concatenate([win_buf, new_win], axis=1)
    q_pos = t0 + jnp.arange(T)
    q4 = q.reshape(B, T, NSA_HEADS, HEAD_DIM)
    kc = _nsa_compress(kv_full[:, :, 0], cmp_pe[0], cmp_w1[0], cmp_w2[0])
    vc = _nsa_compress(kv_full[:, :, 1], cmp_pe[1], cmp_w1[1], cmp_w2[1])
    o_cmp, p_cmp = _cmp_attention(q4, kc, vc, q_pos)
    if past is None:
        n_sel = -(-T // SEL_BLK)
        sel = _selection_mask(_selection_scores(p_cmp, q_pos, n_sel))
        qh = _q_to_heads(q, NSA_KV_HEADS)
        slc_k = _to_heads(kv_full[:, :, 2].reshape(B, T, NSA_KVW), NSA_KV_HEADS)
        slc_v = _to_heads(kv_full[:, :, 3].reshape(B, T, NSA_KVW), NSA_KV_HEADS)
        win_k = _to_heads(kv_win[:, :, 0].reshape(B, T, NSA_KVW), NSA_KV_HEADS)
        win_v = _to_heads(kv_win[:, :, 1].reshape(B, T, NSA_KVW), NSA_KV_HEADS)
        o_slc = _from_heads(_flash(qh, slc_k, slc_v, tq=128, tk=512, sel=sel)).reshape(B, T, NSA_HEADS, HEAD_DIM)
        o_win = _from_heads(_flash(qh, win_k, win_v, tq=128, tk=512, window=WINDOW)).reshape(B, T, NSA_HEADS, HEAD_DIM)
        win_state = kv_win[:, -min(WINDOW, T):]
    else:
        o_slc = _selected_attention_gather(q4, kv_full[:, :, 2], kv_full[:, :, 3], p_cmp, q_pos)
        wb = past[1].shape[1]
        k_pos = t0 - wb + jnp.arange(kv_win.shape[1])
        o_win = _attend(q4, kv_win[:, :, 0], kv_win[:, :, 1], _window_mask(q_pos, k_pos))
        win_state = kv_win[:, -wb:]
    g = jax.nn.sigmoid(gate_logit).reshape(B, T, NSA_HEADS, 3, 1)
    o_nsa = (g[:, :, :, 0] * o_cmp + g[:, :, :, 1] * o_slc + g[:, :, :, 2] * o_win).reshape(B, T, NSA_WIDTH)
    za = _mm(ga.reshape(B * T, GLA_LOWRANK), gla_wa, gla_ba).reshape(B, T, GLA_QK)
    if T % GLA_CHUNK == 0:
        o_gla, s_new = _gla(gq, gk, gv, za, gr, gla_g, s0, GLA_CHUNK, 4)
    else:
        Tp = _round_up(T, 2 * SUBLANE)
        pad = lambda a, val=0.0: jnp.pad(a, ((0, 0), (0, Tp - T), (0, 0)), constant_values=val)
        o_gla, s_new = _gla(pad(gq), pad(gk), pad(gv), pad(za, 1e4), pad(gr), gla_g, s0, Tp, 1)
        o_gla = o_gla[:, :T]
    y = _mm(jnp.concatenate([o_nsa, o_gla], axis=-1).reshape(B * T, -1), w_out).reshape(B, T, D_MODEL)
    return y, (new_nsa, win_state, s_new)


def _mixer_cd(h, w_in, w_out, b_f, a_re, a_im, b_re, b_im, c_re, c_im, d, log_dt, glu_w, glu_b, past):
    B, T, _ = h.shape
    proj = _mm(h.reshape(B * T, D_MODEL), w_in).reshape(B, T, -1)
    q, k, v, f, u = _split(proj, CD_SPLITS)
    log_f = jax.nn.log_sigmoid(f + b_f)
    new_kv = jnp.stack([k.reshape(B, T, FOX_HEADS, HEAD_DIM), v.reshape(B, T, FOX_HEADS, HEAD_DIM)], axis=2)
    h0 = None
    if past is None:
        F = jnp.cumsum(log_f, axis=1)
        Ft = F.transpose(0, 2, 1)
        qb = jnp.broadcast_to(Ft[..., None], (B, FOX_HEADS, T, LANE))
        o = _flash(_q_to_heads(q, FOX_HEADS), _to_heads(k, FOX_HEADS), _to_heads(v, FOX_HEADS),
                   tq=256, tk=512, qbias=qb, kbias=Ft)
        o_fox = _from_heads(o)
    else:
        past_kv, past_lf, h0 = past
        t0 = past_kv.shape[1]
        k_all = jnp.concatenate([past_kv[:, :, 0], k.reshape(B, T, FOX_HEADS, HEAD_DIM)], axis=1)
        v_all = jnp.concatenate([past_kv[:, :, 1], v.reshape(B, T, FOX_HEADS, HEAD_DIM)], axis=1)
        lf_all = jnp.concatenate([past_lf, log_f], axis=1)
        F = jnp.cumsum(lf_all, axis=1)
        Fq = F[:, t0:]
        q_pos = t0 + jnp.arange(T)
        k_pos = jnp.arange(k_all.shape[1])
        bias = (Fq[:, :, None, :] - F[:, None, :, :]).transpose(0, 3, 1, 2)[:, :, None]
        o_fox = _attend(q.reshape(B, T, FOX_HEADS, HEAD_DIM), k_all, v_all,
                        k_pos[None, :] <= q_pos[:, None], bias).reshape(B, T, FOX_WIDTH)
    y_s5, s5_state = _s5(u, a_re, a_im, b_re, b_im, c_re, c_im, d, log_dt, glu_w, glu_b, h0, min(T, 128))
    y = _mm(jnp.concatenate([o_fox, y_s5], axis=-1).reshape(B * T, -1), w_out).reshape(B, T, D_MODEL)
    return y, (new_kv, log_f, s5_state)


def kernel(x_prompt, x_sample, cache_ab_nsa_kv, cache_ab_win_kv, state_ab_gla, cache_cd_fox_kv, cache_cd_fox_logf, state_cd_s5, page_table, c_prompt, c_sample, ada_w, ada_b, norm_g, ab_w_in, ab_w_out, nsa_cmp_pe, nsa_cmp_w1, nsa_cmp_w2, gla_w_alpha, gla_b_alpha, gla_norm_g, cd_w_in, cd_w_out, fox_b_f, s5_a_re, s5_a_im, s5_b_re, s5_b_im, s5_c_re, s5_c_im, s5_d, s5_log_dt, s5_glu_w, s5_glu_b, router_w, router_b, moe_w1, moe_b1, moe_w2, moe_b2, final_norm_g):
    xs = [x_prompt, x_sample]
    cs = [c_prompt, c_sample]
    new = [{}, {}]
    n_prompt = x_prompt.shape[0] * x_prompt.shape[1]
    for l in range(DEPTH):
        j = l // 2
        mods = [jnp.split(_mm(jax.nn.silu(c), ada_w[l], ada_b[l]), 6, axis=-1) for c in cs]
        for grp in range(2):
            x = xs[grp]
            sh1, sc1, g1 = mods[grp][:3]
            hn = _modulated_norm(x, norm_g[l, 0], sh1, sc1)
            if l % 2 == 0:
                past = None
                if grp == 1:
                    past = (_gather_pages(cache_ab_nsa_kv[j], page_table), cache_ab_win_kv[j], state_ab_gla[j])
                y, st = _mixer_ab(hn, ab_w_in[j], ab_w_out[j], nsa_cmp_pe[j], nsa_cmp_w1[j], nsa_cmp_w2[j],
                                  gla_w_alpha[j], gla_b_alpha[j], gla_norm_g[j], past)
                names = ('nsa_kv', 'win_kv', 'gla')
            else:
                past = None
                if grp == 1:
                    past = (_gather_pages(cache_cd_fox_kv[j], page_table),
                            _gather_pages(cache_cd_fox_logf[j], page_table), state_cd_s5[j])
                y, st = _mixer_cd(hn, cd_w_in[j], cd_w_out[j], fox_b_f[j], s5_a_re[j], s5_a_im[j], s5_b_re[j],
                                  s5_b_im[j], s5_c_re[j], s5_c_im[j], s5_d[j], s5_log_dt[j], s5_glu_w[j],
                                  s5_glu_b[j], past)
                names = ('fox_kv', 'fox_logf', 's5')
            for name, s in zip(names, st):
                new[grp].setdefault(name, []).append(s)
            xs[grp] = x + g1[:, None, :] * y
        hn2 = [_modulated_norm(xs[grp], norm_g[l, 1], mods[grp][3], mods[grp][4]).reshape(-1, D_MODEL)
               for grp in range(2)]
        ym = _moe(jnp.concatenate(hn2, axis=0), router_w[l], router_b[l], moe_w1[l], moe_b1[l], moe_w2[l], moe_b2[l])
        yms = [ym[:n_prompt], ym[n_prompt:]]
        for grp in range(2):
            xs[grp] = xs[grp] + mods[grp][5][:, None, :] * yms[grp].reshape(xs[grp].shape)
    ys = [_rmsnorm(x, final_norm_g) for x in xs]
    names = ('nsa_kv', 'win_kv', 'gla', 'fox_kv', 'fox_logf', 's5')
    outs = [jnp.stack(new[grp][name]) for grp in range(2) for name in names]
    return (ys[0], ys[1]) + tuple(outs)
```

```python
import functools
import math

import numpy as np
import jax
import jax.numpy as jnp
from jax import lax
from jax.experimental import pallas as pl
from jax.experimental.pallas import tpu as pltpu

D_MODEL = 1024
DEPTH = 2
PAGE_SIZE = 128
HEAD_DIM = 64
NSA_HEADS = 8
NSA_KV_HEADS = 2
CMP_BLK = 32
CMP_STRIDE = 16
SEL_BLK = 64
SEL_TOPN = 16
WINDOW = 512
GLA_HEADS = 4
GLA_DK = 64
GLA_DV = 128
GLA_LOWRANK = 16
GLA_TAU = 16.0
GLA_CHUNK = 64
FOX_HEADS = 8
S5_GROUPS = 32
S5_CH = 16
S5_STATE = 64
N_EXPERTS = 32
TOP_K = 4
D_FF = D_MODEL
SWIGLU_LIMIT = 7.0
SWIGLU_ALPHA = 1.702
EPS = 1e-6
NEG_INF = -1e30
BIG = 1e9
NSA_WIDTH = NSA_HEADS * HEAD_DIM
NSA_KVW = NSA_KV_HEADS * HEAD_DIM
GLA_QK = GLA_HEADS * GLA_DK
GLA_WIDTH = GLA_HEADS * GLA_DV
FOX_WIDTH = FOX_HEADS * HEAD_DIM
S5_WIDTH = S5_GROUPS * S5_CH
AB_SPLITS = (NSA_WIDTH, 6 * NSA_KVW, 3 * NSA_HEADS, GLA_QK, GLA_QK, GLA_WIDTH, GLA_WIDTH, GLA_LOWRANK)
CD_SPLITS = (FOX_WIDTH, FOX_WIDTH, FOX_WIDTH, FOX_HEADS, S5_WIDTH)

LANE = 128
SUBLANE = 8
VMEM_LIMIT = 56 * 1024 * 1024
MOE_TM = 256

F32 = jnp.float32
BF16 = jnp.bfloat16


def _round_up(n, m):
    return -(-n // m) * m


def _pick_tile(n, candidates):
    for c in candidates:
        if n % c == 0:
            return c
    return n


def _mm_kernel(x_ref, w_ref, b_ref, o_ref, wb_ref, *, act, precise):
    if precise:
        acc = jnp.dot(x_ref[...], w_ref[...], preferred_element_type=F32, precision=lax.Precision.HIGHEST)
    else:
        @pl.when(pl.program_id(1) == 0)
        def _():
            wb_ref[...] = w_ref[...].astype(BF16)

        acc = jnp.dot(x_ref[...].astype(BF16), wb_ref[...], preferred_element_type=F32)
    acc = acc + b_ref[...]
    if act == 'gelu':
        acc = jax.nn.gelu(acc)
    o_ref[...] = acc.astype(o_ref.dtype)


def _mm(x, w, b=None, act=None, out_dtype=F32, precise=False, keep_cols=False):
    M, K = x.shape
    N = w.shape[1]
    Np = _round_up(N, 2 * LANE) if N > 2 * LANE else _round_up(N, LANE)
    Mp = _round_up(M, 512) if M > 256 else _round_up(M, SUBLANE)
    tm = min(Mp, 512)
    tn = _pick_tile(Np, (512, 256, 128))
    if Np != N:
        w = jnp.pad(w, ((0, 0), (0, Np - N)))
    if b is None:
        b = jnp.zeros((N,), F32)
    b = jnp.pad(b.astype(F32), (0, Np - N)).reshape(1, Np)
    if Mp != M:
        x = jnp.pad(x, ((0, Mp - M), (0, 0)))
    out = pl.pallas_call(
        functools.partial(_mm_kernel, act=act, precise=precise),
        grid=(Np // tn, Mp // tm),
        in_specs=[pl.BlockSpec((tm, K), lambda j, i: (i, 0)),
                  pl.BlockSpec((K, tn), lambda j, i: (0, j)),
                  pl.BlockSpec((1, tn), lambda j, i: (0, j))],
        out_specs=pl.BlockSpec((tm, tn), lambda j, i: (i, j)),
        out_shape=jax.ShapeDtypeStruct((Mp, Np), out_dtype),
        scratch_shapes=[pltpu.VMEM((K, tn), BF16)],
        compiler_params=pltpu.CompilerParams(
            dimension_semantics=("arbitrary", "arbitrary"), vmem_limit_bytes=VMEM_LIMIT),
        name="mm",
    )(x, w, b)
    return out[:M] if keep_cols else out[:M, :N]


def _flash_kernel(*refs, tq, tk, R, window, use_bias, use_sel):
    q_ref, k_ref, v_ref = refs[:3]
    n = 3
    if use_bias:
        qb_ref, kb_ref = refs[n:n + 2]
        n += 2
    if use_sel:
        sel_ref = refs[n]
        n += 1
    o_ref = refs[n]
    i = pl.program_id(2)
    rows = R * tq
    q = q_ref[0, 0].reshape(rows, HEAD_DIM)
    q_pos = i * tq + jnp.bitwise_and(lax.broadcasted_iota(jnp.int32, (rows, 1), 0), tq - 1)
    hi = ((i + 1) * tq + tk - 1) // tk
    lo = jnp.maximum(i * tq - window + 1, 0) // tk if window else 0
    if use_bias:
        qb = qb_ref[0, 0]
        qb = jnp.concatenate([qb] * (tk // LANE), axis=1)
    if use_sel:
        sel = sel_ref[0, 0].astype(BF16)
        n_sel = sel.shape[1]

    def body(j, carry):
        m, l, acc = carry
        start = pl.multiple_of(j * tk, tk)
        k = k_ref[0, 0, pl.ds(start, tk), :]
        v = v_ref[0, 0, pl.ds(start, tk), :]
        s = lax.dot_general(q, k, (((1,), (1,)), ((), ())), preferred_element_type=F32)
        k_pos = j * tk + lax.broadcasted_iota(jnp.int32, (1, tk), 1)
        if use_bias:
            s = s + qb - kb_ref[0, 0, j]
        s = jnp.where(k_pos <= q_pos, s, NEG_INF)
        if window:
            s = jnp.where(q_pos - k_pos < window, s, NEG_INF)
        if use_sel:
            blk_of_key = (j * tk + lax.broadcasted_iota(jnp.int32, (n_sel, tk), 1)) // SEL_BLK
            expand = jnp.where(blk_of_key == lax.broadcasted_iota(jnp.int32, (n_sel, tk), 0), 1.0, 0.0).astype(BF16)
            chosen = jnp.dot(sel, expand, preferred_element_type=F32)
            chosen = jnp.concatenate([chosen] * R, axis=0)
            s = jnp.where(chosen > 0.5, s, NEG_INF)
        m_new = jnp.maximum(m, jnp.max(s, axis=1, keepdims=True))
        p = jnp.where(s > 0.5 * NEG_INF, jnp.exp(s - m_new), 0.0)
        alpha = jnp.exp(m - m_new)
        l = alpha * l + jnp.sum(p, axis=1, keepdims=True)
        acc = alpha * acc + jnp.dot(p.astype(BF16), v, preferred_element_type=F32)
        return m_new, l, acc

    m0 = jnp.full((rows, 1), NEG_INF, F32)
    l0 = jnp.zeros((rows, 1), F32)
    a0 = jnp.zeros((rows, HEAD_DIM), F32)
    m, l, acc = lax.fori_loop(lo, hi, body, (m0, l0, a0))
    o_ref[0, 0] = (acc / l).reshape(R, tq, HEAD_DIM)


def _flash(q, k, v, *, tq, tk, window=0, qbias=None, kbias=None, sel=None):
    B, G, R, T, _ = q.shape
    grid = (B, G, T // tq)
    in_specs = [pl.BlockSpec((1, 1, R, tq, HEAD_DIM), lambda b, g, i: (b, g, 0, i, 0)),
                pl.BlockSpec((1, 1, T, HEAD_DIM), lambda b, g, i: (b, g, 0, 0)),
                pl.BlockSpec((1, 1, T, HEAD_DIM), lambda b, g, i: (b, g, 0, 0))]
    args = [q, k, v]
    if qbias is not None:
        in_specs += [pl.BlockSpec((1, 1, tq, LANE), lambda b, g, i: (b, g, i, 0)),
                     pl.BlockSpec((1, 1, T // tk, 1, tk), lambda b, g, i: (b, g, 0, 0, 0))]
        args += [qbias, kbias.reshape(B, G, T // tk, 1, tk)]
    if sel is not None:
        in_specs += [pl.BlockSpec((1, 1, tq, sel.shape[-1]), lambda b, g, i: (b, g, i, 0))]
        args += [sel]
    return pl.pallas_call(
        functools.partial(_flash_kernel, tq=tq, tk=tk, R=R, window=window,
                          use_bias=qbias is not None, use_sel=sel is not None),
        grid=grid,
        in_specs=in_specs,
        out_specs=pl.BlockSpec((1, 1, R, tq, HEAD_DIM), lambda b, g, i: (b, g, 0, i, 0)),
        out_shape=jax.ShapeDtypeStruct((B, G, R, T, HEAD_DIM), F32),
        compiler_params=pltpu.CompilerParams(
            dimension_semantics=("arbitrary", "arbitrary", "arbitrary"), vmem_limit_bytes=VMEM_LIMIT),
        name="flash",
    )(*args)


def _attn_kernel(*refs, tq, tk, G, R, window, use_bias, use_sel):
    q_ref, k_ref, v_ref = refs[:3]
    n = 3
    if use_bias:
        fq_ref, fk_ref = refs[n:n + 2]
        n += 2
    if use_sel:
        sel_ref = refs[n]
        n += 1
    o_ref, kb_ref, vb_ref = refs[n:n + 3]
    i = pl.program_id(1)

    @pl.when(i == 0)
    def _():
        kb_ref[...] = k_ref[0].astype(BF16)
        vb_ref[...] = v_ref[0].astype(BF16)

    rows = R * tq
    q_pos = i * tq + jnp.bitwise_and(lax.broadcasted_iota(jnp.int32, (rows, 1), 0), tq - 1)
    hi = ((i + 1) * tq + tk - 1) // tk
    lo = jnp.maximum(i * tq - window + 1, 0) // tk if window else 0
    outs = []
    for g in range(G):
        cols = slice(g * HEAD_DIM, (g + 1) * HEAD_DIM)
        qg = q_ref[0, :, g * R * HEAD_DIM:(g + 1) * R * HEAD_DIM] * HEAD_DIM ** -0.5
        q = jnp.concatenate([qg[:, r * HEAD_DIM:(r + 1) * HEAD_DIM] for r in range(R)], axis=0).astype(BF16)
        if use_bias:
            fq = fq_ref[0, :, g:g + 1]
        if use_sel:
            sel = sel_ref[0, g].astype(BF16)
            n_sel = sel.shape[1]

        def body(j, carry):
            m, l, acc = carry
            start = pl.multiple_of(j * tk, tk)
            k = kb_ref[pl.ds(start, tk), cols]
            v = vb_ref[pl.ds(start, tk), cols]
            s = lax.dot_general(q, k, (((1,), (1,)), ((), ())), preferred_element_type=F32)
            k_pos = j * tk + lax.broadcasted_iota(jnp.int32, (1, tk), 1)
            if use_bias:
                s = s + fq - fk_ref[0, g, j]
            s = jnp.where(k_pos <= q_pos, s, NEG_INF)
            if window:
                s = jnp.where(q_pos - k_pos < window, s, NEG_INF)
            if use_sel:
                blk_of_key = (j * tk + lax.broadcasted_iota(jnp.int32, (n_sel, tk), 1)) // SEL_BLK
                expand = jnp.where(blk_of_key == lax.broadcasted_iota(jnp.int32, (n_sel, tk), 0), 1.0, 0.0)
                chosen = jnp.dot(sel, expand.astype(BF16), preferred_element_type=F32)
                chosen = jnp.concatenate([chosen] * R, axis=0)
                s = jnp.where(chosen > 0.5, s, NEG_INF)
            m_new = jnp.maximum(m, jnp.max(s, axis=1, keepdims=True))
            p = jnp.where(s > 0.5 * NEG_INF, jnp.exp(s - m_new), 0.0)
            alpha = jnp.exp(m - m_new)
            l = alpha * l + jnp.sum(p, axis=1, keepdims=True)
            acc = alpha * acc + jnp.dot(p.astype(BF16), v, preferred_element_type=F32)
            return m_new, l, acc

        m0 = jnp.full((rows, 1), NEG_INF, F32)
        l0 = jnp.zeros((rows, 1), F32)
        a0 = jnp.zeros((rows, HEAD_DIM), F32)
        m, l, acc = lax.fori_loop(lo, hi, body, (m0, l0, a0))
        o = acc / l
        outs += [o[r * tq:(r + 1) * tq] for r in range(R)]
    o_ref[0] = jnp.concatenate(outs, axis=1)


def _attn(proj, q_col, k_col, v_col, *, G, R, tq, tk, window=0, fq=None, fk=None, sel=None):
    B, T, _ = proj.shape
    qw, kw = G * R * HEAD_DIM, G * HEAD_DIM
    assert q_col % qw == 0 and k_col % kw == 0 and v_col % kw == 0
    in_specs = [pl.BlockSpec((1, tq, qw), lambda b, i: (b, i, q_col // qw)),
                pl.BlockSpec((1, T, kw), lambda b, i: (b, 0, k_col // kw)),
                pl.BlockSpec((1, T, kw), lambda b, i: (b, 0, v_col // kw))]
    args = [proj, proj, proj]
    if fq is not None:
        in_specs += [pl.BlockSpec((1, tq, G), lambda b, i: (b, i, 0)),
                     pl.BlockSpec((1, G, T // tk, 1, tk), lambda b, i: (b, 0, 0, 0, 0))]
        args += [fq, fk.reshape(B, G, T // tk, 1, tk)]
    if sel is not None:
        in_specs += [pl.BlockSpec((1, G, tq, sel.shape[-1]), lambda b, i: (b, 0, i, 0))]
        args += [sel]
    return pl.pallas_call(
        functools.partial(_attn_kernel, tq=tq, tk=tk, G=G, R=R, window=window,
                          use_bias=fq is not None, use_sel=sel is not None),
        grid=(B, T // tq),
        in_specs=in_specs,
        out_specs=pl.BlockSpec((1, tq, qw), lambda b, i: (b, i, 0)),
        out_shape=jax.ShapeDtypeStruct((B, T, qw), F32),
        scratch_shapes=[pltpu.VMEM((T, kw), BF16), pltpu.VMEM((T, kw), BF16)],
        compiler_params=pltpu.CompilerParams(
            dimension_semantics=("arbitrary", "arbitrary"), vmem_limit_bytes=VMEM_LIMIT),
        name="attn",
    )(*args)


def _moe_kernel(be_ref, nb_ref, x_ref, w1_ref, b1_ref, w2_ref, b2_ref, o_ref, w1b_ref, w2b_ref):
    i = pl.program_id(0)
    prev = be_ref[jnp.maximum(i - 1, 0)]

    @pl.when(jnp.logical_or(i == 0, be_ref[i] != prev))
    def _():
        w1b_ref[...] = w1_ref[0].astype(BF16)
        w2b_ref[...] = w2_ref[0].astype(BF16)

    @pl.when(i < nb_ref[0])
    def _():
        gu = jnp.dot(x_ref[...].astype(BF16), w1b_ref[...], preferred_element_type=F32) + b1_ref[0]
        g = jnp.minimum(gu[:, :D_FF], SWIGLU_LIMIT)
        up = jnp.clip(gu[:, D_FF:], -SWIGLU_LIMIT, SWIGLU_LIMIT)
        h = (up + 1) * g * jax.nn.sigmoid(SWIGLU_ALPHA * g)
        o_ref[...] = jnp.dot(h.astype(BF16), w2b_ref[...], preferred_element_type=F32) + b2_ref[0]

    @pl.when(i >= nb_ref[0])
    def _():
        o_ref[...] = jnp.zeros_like(o_ref)


def _moe_experts(xs, blk_e, n_used, w1, b1, w2, b2):
    n_blocks = xs.shape[0] // MOE_TM
    grid_spec = pltpu.PrefetchScalarGridSpec(
        num_scalar_prefetch=2,
        grid=(n_blocks,),
        in_specs=[pl.BlockSpec((MOE_TM, D_MODEL), lambda i, be, nb: (i, 0)),
                  pl.BlockSpec((1, D_MODEL, 2 * D_FF), lambda i, be, nb: (be[i], 0, 0)),
                  pl.BlockSpec((1, 1, 2 * D_FF), lambda i, be, nb: (be[i], 0, 0)),
                  pl.BlockSpec((1, D_FF, D_MODEL), lambda i, be, nb: (be[i], 0, 0)),
                  pl.BlockSpec((1, 1, D_MODEL), lambda i, be, nb: (be[i], 0, 0))],
        out_specs=pl.BlockSpec((MOE_TM, D_MODEL), lambda i, be, nb: (i, 0)),
        scratch_shapes=[pltpu.VMEM((D_MODEL, 2 * D_FF), BF16), pltpu.VMEM((D_FF, D_MODEL), BF16)],
    )
    return pl.pallas_call(
        _moe_kernel,
        grid_spec=grid_spec,
        out_shape=jax.ShapeDtypeStruct((n_blocks * MOE_TM, D_MODEL), F32),
        compiler_params=pltpu.CompilerParams(
            dimension_semantics=("arbitrary",), vmem_limit_bytes=VMEM_LIMIT),
        name="moe_experts",
    )(blk_e, n_used, xs, w1, b1.reshape(N_EXPERTS, 1, 2 * D_FF), w2, b2.reshape(N_EXPERTS, 1, D_MODEL))


def _moe_combine_kernel(y_ref, g_ref, o_ref):
    g = g_ref[...]
    acc = g[:, 0:1] * y_ref[:, :D_MODEL]
    for k in range(1, TOP_K):
        acc = acc + g[:, k:k + 1] * y_ref[:, k * D_MODEL:(k + 1) * D_MODEL]
    o_ref[...] = acc


def _moe_combine(yg, gate):
    N = yg.shape[0]
    tn = _pick_tile(N, (512, 256, 128, 64, 32, 16, 8))
    return pl.pallas_call(
        _moe_combine_kernel,
        grid=(N // tn,),
        in_specs=[pl.BlockSpec((tn, TOP_K * D_MODEL), lambda i: (i, 0)),
                  pl.BlockSpec((tn, TOP_K), lambda i: (i, 0))],
        out_specs=pl.BlockSpec((tn, D_MODEL), lambda i: (i, 0)),
        out_shape=jax.ShapeDtypeStruct((N, D_MODEL), F32),
        compiler_params=pltpu.CompilerParams(dimension_semantics=("arbitrary",), vmem_limit_bytes=VMEM_LIMIT),
        name="moe_combine",
    )(yg, gate)


def _moe(xf, w_r, b_r, w1, b1, w2, b2):
    N = xf.shape[0]
    n_rows = N * TOP_K
    logits = _mm(xf, w_r, b_r, precise=True)
    top_v, top_i = lax.top_k(logits, TOP_K)
    gate = jax.nn.softmax(top_v, axis=-1)
    flat_e = top_i.reshape(-1)
    order = jnp.argsort(flat_e)
    sorted_e = flat_e[order]
    counts = jnp.bincount(flat_e, length=N_EXPERTS)
    padded = (counts + MOE_TM - 1) // MOE_TM * MOE_TM
    pad_end = jnp.cumsum(padded)
    pad_start = pad_end - padded
    start = jnp.cumsum(counts) - counts
    dest = (pad_start[sorted_e] + jnp.arange(n_rows) - start[sorted_e]).astype(jnp.int32)
    n_blocks = -(-(n_rows + N_EXPERTS * (MOE_TM - 1)) // MOE_TM)
    n_used = (pad_end[-1] // MOE_TM).astype(jnp.int32)
    blk = jnp.minimum(jnp.arange(n_blocks), n_used - 1) * MOE_TM
    blk_e = jnp.minimum(jnp.searchsorted(pad_end, blk, side='right'), N_EXPERTS - 1).astype(jnp.int32)
    row_e = jnp.repeat(blk_e, MOE_TM)
    off = jnp.arange(n_blocks * MOE_TM) - pad_start[row_e]
    src = jnp.clip(start[row_e] + off, 0, n_rows - 1)
    row_tok = jnp.where(off < counts[row_e], (order // TOP_K)[src], 0).astype(jnp.int32)
    xs = xf[row_tok]
    ys = _moe_experts(xs, blk_e, n_used.reshape(1), w1, b1, w2, b2)
    pos = dest[jnp.argsort(order)]
    return _moe_combine(ys[pos].reshape(N, TOP_K * D_MODEL), gate)


def _rmsnorm(x, g):
    y = x * lax.rsqrt(jnp.mean(x * x, axis=-1, keepdims=True) + EPS)
    return y * g


def _modulated_norm(x, g, shift, scale):
    return _rmsnorm(x, g) * (1 + scale[:, None, :]) + shift[:, None, :]


def _split(x, sizes):
    return jnp.split(x, np.cumsum(sizes)[:-1].tolist(), axis=-1)


def _gather_pages(pool, page_table):
    g = pool[page_table]
    return g.reshape((g.shape[0], g.shape[1] * g.shape[2]) + g.shape[3:])


def _attend(q, k, v, mask, bias=None):
    B, Q, H, D = q.shape
    G = k.shape[2]
    s = jnp.einsum('bqgnd,bkgd->bgnqk', q.reshape(B, Q, G, H // G, D), k).astype(F32) * D ** -0.5
    if bias is not None:
        s = s + bias
    p = jax.nn.softmax(jnp.where(mask, s, NEG_INF), axis=-1).astype(v.dtype)
    return jnp.einsum('bgnqk,bkgd->bqgnd', p, v).reshape(B, Q, H, D)


def _window_mask(q_pos, k_pos):
    d = q_pos[:, None] - k_pos[None, :]
    return (d >= 0) & (d < WINDOW) & (k_pos[None, :] >= 0)


def _to_heads(x, G):
    B, T, _ = x.shape
    return x.astype(BF16).reshape(B, T, G, HEAD_DIM).transpose(0, 2, 1, 3)


def _q_to_heads(q, G):
    B, T, W = q.shape
    R = W // HEAD_DIM // G
    return (q * HEAD_DIM ** -0.5).astype(BF16).reshape(B, T, G, R, HEAD_DIM).transpose(0, 2, 3, 1, 4)


def _from_heads(o):
    B, G, R, T, D = o.shape
    return o.transpose(0, 3, 1, 2, 4).reshape(B, T, G * R * D)


CMP_CHUNK_W = CMP_STRIDE * HEAD_DIM
CMP_HID = 256


def _cmp_chunk_proj(tokens, w1):
    assert CMP_BLK == 2 * CMP_STRIDE
    S, T, G, D = tokens.shape
    n = T // CMP_STRIDE
    ch = tokens.reshape(S, n, CMP_STRIDE, G, D).transpose(0, 1, 3, 2, 4).reshape(S * n * G, CMP_CHUNK_W)
    wcat = jnp.concatenate([w1[:CMP_CHUNK_W], w1[CMP_CHUNK_W:]], axis=1)
    return _mm(ch, wcat).reshape(S, n, G, 2 * CMP_HID)


def _cmp_out_kernel(a_ref, b_ref, bias_ref, w_ref, o_ref):
    hid = jax.nn.gelu(a_ref[...] + b_ref[...] + bias_ref[...])
    o_ref[...] = jnp.dot(hid.astype(BF16), w_ref[...].astype(BF16), preferred_element_type=F32)


def _cmp_finish(proj, pe, w1, w2):
    B, n, G, _ = proj.shape
    first = proj[:, :-1, :, :CMP_HID].reshape(-1, CMP_HID)
    second = proj[:, 1:, :, CMP_HID:].reshape(-1, CMP_HID)
    bias = _mm(pe.reshape(1, CMP_BLK * HEAD_DIM), w1)
    rows = first.shape[0]
    rows_p = _round_up(rows, 512)
    pad = lambda a: jnp.pad(a, ((0, rows_p - rows), (0, 0)))
    w2p = jnp.pad(w2, ((0, 0), (0, LANE - HEAD_DIM)))
    out = pl.pallas_call(
        _cmp_out_kernel,
        grid=(rows_p // 512,),
        in_specs=[pl.BlockSpec((512, CMP_HID), lambda i: (i, 0)), pl.BlockSpec((512, CMP_HID), lambda i: (i, 0)),
                  pl.BlockSpec((1, CMP_HID), lambda i: (0, 0)), pl.BlockSpec((CMP_HID, LANE), lambda i: (0, 0))],
        out_specs=pl.BlockSpec((512, LANE), lambda i: (i, 0)),
        out_shape=jax.ShapeDtypeStruct((rows_p, LANE), F32),
        compiler_params=pltpu.CompilerParams(dimension_semantics=("arbitrary",), vmem_limit_bytes=VMEM_LIMIT),
        name="cmp_out",
    )(pad(first), pad(second), bias, w2p)
    return out[:rows, :HEAD_DIM].reshape(B, n - 1, G, HEAD_DIM)


def _cmp_attention(q, kc, vc, q_pos):
    B, Q, H, D = q.shape
    N, G = kc.shape[1], kc.shape[2]
    c_end = jnp.arange(N) * CMP_STRIDE + CMP_BLK
    valid = c_end[None, :] <= q_pos[:, None] + 1
    s = jnp.einsum('bqgnd,bcgd->bgnqc', q.reshape(B, Q, G, H // G, D), kc).astype(F32) * D ** -0.5
    p = jax.nn.softmax(jnp.where(valid, s, NEG_INF), axis=-1) * valid
    o = jnp.einsum('bgnqc,bcgd->bqgnd', p.astype(vc.dtype), vc).reshape(B, Q, H, D)
    return o, p


def _cmp_to_sel(n_cmp, n_sel):
    c0 = jnp.arange(n_cmp)[:, None] * CMP_STRIDE
    s0 = jnp.arange(n_sel)[None, :] * SEL_BLK
    ov = jnp.minimum(c0 + CMP_BLK, s0 + SEL_BLK) - jnp.maximum(c0, s0)
    return jnp.maximum(ov, 0).astype(F32) / CMP_BLK


def _selection_scores(p_cmp, q_pos, n_sel):
    imp = jnp.einsum('bgnqc,cs->bgqs', p_cmp, _cmp_to_sel(p_cmp.shape[-1], n_sel))
    blk = jnp.arange(n_sel)[None, :]
    cur = (q_pos // SEL_BLK)[:, None]
    valid = blk * SEL_BLK <= q_pos[:, None]
    forced = (blk == 0) | (blk == cur) | (blk == cur - 1)
    return jnp.where(valid, jnp.where(forced, BIG, imp), -BIG)


def _selection_mask(score):
    n_sel = score.shape[-1]
    a = score[..., :, None]
    b = score[..., None, :]
    lower = jnp.arange(n_sel)[None, :] < jnp.arange(n_sel)[:, None]
    beats = (b > a) | ((b == a) & lower)
    rank = jnp.sum(beats, axis=-1)
    return (rank < min(SEL_TOPN, n_sel)).astype(F32)


def _selected_attention_gather(q, k, v, p_cmp, q_pos):
    B, Q, H, D = q.shape
    T_all, G = k.shape[1], k.shape[2]
    n_sel = -(-T_all // SEL_BLK)
    score = _selection_scores(p_cmp, q_pos, n_sel)
    _, idx = lax.top_k(score, min(SEL_TOPN, n_sel))
    tok = (idx[..., None] * SEL_BLK + jnp.arange(SEL_BLK)).reshape(B, G, Q, -1)
    pad = ((0, 0), (0, n_sel * SEL_BLK - T_all), (0, 0), (0, 0))
    kt = jnp.pad(k, pad).transpose(0, 2, 1, 3)
    vt = jnp.pad(v, pad).transpose(0, 2, 1, 3)
    b_ix = jnp.arange(B)[:, None, None]
    g_ix = jnp.arange(G)[None, :, None]
    L = tok.shape[3]
    flat = tok.reshape(B, G, Q * L)
    ks = kt[b_ix, g_ix, flat].reshape(B, G, Q, L, D)
    vs = vt[b_ix, g_ix, flat].reshape(B, G, Q, L, D)
    s = jnp.einsum('bqgnd,bgqld->bgnql', q.reshape(B, Q, G, H // G, D), ks).astype(F32) * D ** -0.5
    mask = (tok <= q_pos[None, None, :, None])[:, :, None]
    p = jax.nn.softmax(jnp.where(mask, s, NEG_INF), axis=-1)
    return jnp.einsum('bgnql,bgqld->bqgnd', p, vs).reshape(B, Q, H, D)


def _gla_kernel(q_ref, k_ref, v_ref, za_ref, gr_ref, g_ref, s0_ref, o_ref, sT_ref, st_ref, *, C, n_chunks):
    @pl.when(pl.program_id(1) == 0)
    def _():
        st_ref[...] = s0_ref[0]

    tri = (lax.broadcasted_iota(jnp.int32, (C, C), 0) >= lax.broadcasted_iota(jnp.int32, (C, C), 1))
    tri_f = jnp.where(tri, 1.0, 0.0)
    for c in range(n_chunks):
        rows = pl.ds(c * C, C)
        log_a = jax.nn.log_sigmoid(za_ref[0, rows, :]) / GLA_TAU
        b = jnp.dot(tri_f, log_a, preferred_element_type=F32, precision=lax.Precision.HIGHEST)
        b_last = b[C - 1:C, :]
        q = q_ref[0, rows, :] * GLA_DK ** -0.5
        k = k_ref[0, rows, :]
        q_dec = (q * jnp.exp(b)).astype(BF16)
        k_inv = (k * jnp.exp(-b)).astype(BF16)
        k_end = (k * jnp.exp(b_last - b)).astype(BF16)
        decay = jnp.exp(b_last)
        v = v_ref[0, rows, :].astype(BF16)
        gr = gr_ref[0, rows, :]
        outs = []
        for h in range(GLA_HEADS):
            kh = slice(h * GLA_DK, (h + 1) * GLA_DK)
            vh = slice(h * GLA_DV, (h + 1) * GLA_DV)
            attn = lax.dot_general(q_dec[:, kh], k_inv[:, kh], (((1,), (1,)), ((), ())), preferred_element_type=F32)
            attn = jnp.where(tri, attn, 0.0).astype(BF16)
            sT = st_ref[h]
            o = jnp.dot(attn, v[:, vh], preferred_element_type=F32)
            o = o + lax.dot_general(q_dec[:, kh], sT.astype(BF16), (((1,), (1,)), ((), ())), preferred_element_type=F32)
            st_ref[h] = decay[:, kh] * sT + lax.dot_general(v[:, vh], k_end[:, kh], (((0,), (0,)), ((), ())),
                                                            preferred_element_type=F32)
            o = o * lax.rsqrt(jnp.mean(o * o, axis=-1, keepdims=True) + EPS) * g_ref[...]
            outs.append(o * jax.nn.silu(gr[:, vh]))
        o_ref[0, rows, :] = jnp.concatenate(outs, axis=1)

    @pl.when(pl.program_id(1) == pl.num_programs(1) - 1)
    def _():
        sT_ref[0] = st_ref[...]


def _gla(gq, gk, gv, za, gr, gla_g, s0, C, n_chunks):
    B, T, _ = gq.shape
    tt = C * n_chunks
    s0T = s0.transpose(0, 1, 3, 2)
    qk_spec = pl.BlockSpec((1, tt, GLA_QK), lambda b, i: (b, i, 0))
    v_spec = pl.BlockSpec((1, tt, GLA_WIDTH), lambda b, i: (b, i, 0))
    st_spec = pl.BlockSpec((1, GLA_HEADS, GLA_DV, GLA_DK), lambda b, i: (b, 0, 0, 0))
    o, sT = pl.pallas_call(
        functools.partial(_gla_kernel, C=C, n_chunks=n_chunks),
        grid=(B, T // tt),
        in_specs=[qk_spec, qk_spec, v_spec, qk_spec, v_spec,
                  pl.BlockSpec((1, GLA_DV), lambda b, i: (0, 0)), st_spec],
        out_specs=[v_spec, st_spec],
        out_shape=[jax.ShapeDtypeStruct((B, T, GLA_WIDTH), F32),
                   jax.ShapeDtypeStruct((B, GLA_HEADS, GLA_DV, GLA_DK), F32)],
        scratch_shapes=[pltpu.VMEM((GLA_HEADS, GLA_DV, GLA_DK), F32)],
        compiler_params=pltpu.CompilerParams(
            dimension_semantics=("arbitrary", "arbitrary"), vmem_limit_bytes=VMEM_LIMIT),
        name="gla",
    )(gq, gk, gv, za, gr, gla_g.reshape(1, GLA_DV), s0T)
    return o, sT.transpose(0, 1, 3, 2)


S5_NK = 4
S5_GPK = S5_GROUPS // S5_NK
S5_HALF = S5_GPK * S5_STATE


def _s5_kernel(u_ref, wb_ref, wc_ref, coef_ref, d_ref, gw_ref, gb_ref, h0_ref, y_ref, hfin_ref, hst_ref, xb_ref,
               *, BT, Tc):
    @pl.when(pl.program_id(0) == 0)
    def _():
        hst_ref[...] = h0_ref[...]

    u = u_ref[...]
    ys = []
    for k in range(S5_NK):
        bu = jnp.dot(u[:, k * LANE:(k + 1) * LANE].astype(BF16), wb_ref[k], preferred_element_type=F32)
        bre, bim = bu[:, :S5_HALF], bu[:, S5_HALF:]
        ar, ai = coef_ref[k, 0:1, :], coef_ref[k, 1:2, :]
        cr, ci = coef_ref[k, 2:3, :], coef_ref[k, 3:4, :]
        xb_ref[:, :S5_HALF] = cr * bre - ci * bim
        xb_ref[:, S5_HALF:] = cr * bim + ci * bre

        def step(t, carry):
            hr, hi = carry
            rows = pl.ds(pl.multiple_of(t * BT, BT), BT)
            hr2 = ar * hr - ai * hi + xb_ref[rows, :S5_HALF]
            hi2 = ar * hi + ai * hr + xb_ref[rows, S5_HALF:]
            xb_ref[rows, :S5_HALF] = hr2
            xb_ref[rows, S5_HALF:] = hi2
            return hr2, hi2

        hr, hi = lax.fori_loop(0, Tc, step, (hst_ref[k, :, :S5_HALF], hst_ref[k, :, S5_HALF:]),
                               unroll=min(Tc, 8))
        hst_ref[k, :, :S5_HALF] = hr
        hst_ref[k, :, S5_HALF:] = hi
        ys.append(jnp.dot(xb_ref[...].astype(BF16), wc_ref[k], preferred_element_type=F32))
    y = jax.nn.gelu(jnp.concatenate(ys, axis=1) + d_ref[...] * u)
    z = jnp.dot(y.astype(BF16), gw_ref[...].astype(BF16), preferred_element_type=F32) + gb_ref[...]
    y_ref[...] = y * jax.nn.sigmoid(z)

    @pl.when(pl.program_id(0) == pl.num_programs(0) - 1)
    def _():
        hfin_ref[...] = hst_ref[...]


def _s5(u, a_re, a_im, b_re, b_im, c_re, c_im, d, log_dt, glu_w, glu_b, h0, Tc):
    B, T, _ = u.shape
    dt = jnp.exp(log_dt)[:, None]
    mag = jnp.exp(a_re * dt)
    ab_re, ab_im = mag * jnp.cos(a_im * dt), mag * jnp.sin(a_im * dt)
    den = a_re * a_re + a_im * a_im
    coef_re = ((ab_re - 1) * a_re + ab_im * a_im) / den
    coef_im = (ab_im * a_re - (ab_re - 1) * a_im) / den
    coefs = jnp.stack([ab_re, ab_im, coef_re, coef_im], axis=0).reshape(4, S5_NK, S5_HALF).transpose(1, 0, 2)
    eye = jnp.eye(S5_GPK, dtype=F32)

    def in_weights(bm):
        bk = bm.reshape(S5_NK, S5_GPK, S5_STATE, S5_CH)
        return jnp.einsum('kgpc,gh->kgchp', bk, eye).reshape(S5_NK, S5_GPK * S5_CH, S5_HALF)

    def out_weights(cm):
        ck = cm.reshape(S5_NK, S5_GPK, S5_CH, S5_STATE)
        return jnp.einsum('kgcp,gh->kgphc', ck, eye).reshape(S5_NK, S5_HALF, S5_GPK * S5_CH)

    wb = jnp.concatenate([in_weights(b_re), in_weights(b_im)], axis=2).astype(BF16)
    wc = jnp.concatenate([out_weights(c_re), -out_weights(c_im)], axis=1).astype(BF16)
    if h0 is None:
        hs0 = jnp.zeros((S5_NK, B, 2 * S5_HALF), F32)
    else:
        hs0 = h0.reshape(B, S5_NK, S5_HALF, 2).transpose(1, 0, 3, 2).reshape(S5_NK, B, 2 * S5_HALF)
    ut = u.transpose(1, 0, 2).reshape(T * B, S5_WIDTH)
    rows = Tc * B
    const2 = lambda i: (0, 0)
    const3 = lambda i: (0, 0, 0)
    y, hfin = pl.pallas_call(
        functools.partial(_s5_kernel, BT=B, Tc=Tc),
        grid=(T // Tc,),
        in_specs=[pl.BlockSpec((rows, S5_WIDTH), lambda i: (i, 0)),
                  pl.BlockSpec(wb.shape, const3), pl.BlockSpec(wc.shape, const3), pl.BlockSpec(coefs.shape, const3),
                  pl.BlockSpec((1, S5_WIDTH), const2), pl.BlockSpec((S5_WIDTH, S5_WIDTH), const2),
                  pl.BlockSpec((1, S5_WIDTH), const2), pl.BlockSpec(hs0.shape, const3)],
        out_specs=[pl.BlockSpec((rows, S5_WIDTH), lambda i: (i, 0)), pl.BlockSpec(hs0.shape, const3)],
        out_shape=[jax.ShapeDtypeStruct((T * B, S5_WIDTH), F32), jax.ShapeDtypeStruct(hs0.shape, F32)],
        scratch_shapes=[pltpu.VMEM(hs0.shape, F32), pltpu.VMEM((rows, 2 * S5_HALF), F32)],
        compiler_params=pltpu.CompilerParams(dimension_semantics=("arbitrary",), vmem_limit_bytes=VMEM_LIMIT),
        name="s5",
    )(ut, wb, wc, coefs, d.reshape(1, S5_WIDTH), glu_w, glu_b.reshape(1, S5_WIDTH), hs0)
    y = y.reshape(T, B, S5_WIDTH).transpose(1, 0, 2)
    hfin = hfin.reshape(S5_NK, B, 2, S5_HALF).transpose(1, 0, 3, 2).reshape(B, S5_GROUPS, S5_STATE, 2)
    return y, hfin


def _mixer_ab(h, w_in, w_out, cmp_pe, cmp_w1, cmp_w2, gla_wa, gla_ba, gla_g, past):
    B, T, _ = h.shape
    projp = _mm(h.reshape(B * T, D_MODEL), w_in, keep_cols=True).reshape(B, T, -1)
    q, kv6, gate_logit, gq, gk, gv, gr, ga = _split(projp[..., :sum(AB_SPLITS)], AB_SPLITS)
    kv6 = kv6.reshape(B, T, 6, NSA_KV_HEADS, HEAD_DIM)
    new_nsa, new_win = kv6[:, :, :4], kv6[:, :, 4:]
    s0 = jnp.zeros((B, GLA_HEADS, GLA_DK, GLA_DV), F32)
    if past is None:
        t0 = 0
        kv_full, kv_win = new_nsa, new_win
        cmp_proj = [_cmp_chunk_proj(new_nsa[:, :T // CMP_STRIDE * CMP_STRIDE, c], cmp_w1[c]) for c in range(2)]
    else:
        pool, page_table, win_buf, s0 = past
        n_pages = page_table.shape[1]
        t0 = n_pages * PAGE_SIZE
        assert PAGE_SIZE % CMP_STRIDE == 0 and T < CMP_STRIDE
        cmp_proj = [_cmp_chunk_proj(pool[:, :, c], cmp_w1[c])[page_table].reshape(
            B, n_pages * (PAGE_SIZE // CMP_STRIDE), NSA_KV_HEADS, 2 * CMP_HID) for c in range(2)]
        past_slc = _gather_pages(pool[:, :, 2:4], page_table)
        slc_all = jnp.concatenate([past_slc, new_nsa[:, :, 2:4]], axis=1)
        kv_win = jnp.concatenate([win_buf, new_win], axis=1)
    q_pos = t0 + jnp.arange(T)
    q4 = q.reshape(B, T, NSA_HEADS, HEAD_DIM)
    kc = _cmp_finish(cmp_proj[0], cmp_pe[0], cmp_w1[0], cmp_w2[0])
    vc = _cmp_finish(cmp_proj[1], cmp_pe[1], cmp_w1[1], cmp_w2[1])
    o_cmp, p_cmp = _cmp_attention(q4, kc, vc, q_pos)
    if past is None:
        n_sel = -(-T // SEL_BLK)
        sel = _selection_mask(_selection_scores(p_cmp, q_pos, n_sel))
        kv_col = lambda kind: NSA_WIDTH + kind * NSA_KVW
        hpg = NSA_HEADS // NSA_KV_HEADS
        o_slc = _attn(projp, 0, kv_col(2), kv_col(3), G=NSA_KV_HEADS, R=hpg, tq=128, tk=512, sel=sel)
        o_win = _attn(projp, 0, kv_col(4), kv_col(5), G=NSA_KV_HEADS, R=hpg, tq=128, tk=512, window=WINDOW)
        o_slc = o_slc.reshape(B, T, NSA_HEADS, HEAD_DIM)
        o_win = o_win.reshape(B, T, NSA_HEADS, HEAD_DIM)
        win_state = kv_win[:, -min(WINDOW, T):]
    else:
        o_slc = _selected_attention_gather(q4, slc_all[:, :, 0], slc_all[:, :, 1], p_cmp, q_pos)
        wb = win_buf.shape[1]
        k_pos = t0 - wb + jnp.arange(kv_win.shape[1])
        o_win = _attend(q4, kv_win[:, :, 0], kv_win[:, :, 1], _window_mask(q_pos, k_pos))
        win_state = kv_win[:, -wb:]
    g = jax.nn.sigmoid(gate_logit).reshape(B, T, NSA_HEADS, 3, 1)
    o_nsa = (g[:, :, :, 0] * o_cmp + g[:, :, :, 1] * o_slc + g[:, :, :, 2] * o_win).reshape(B, T, NSA_WIDTH)
    za = _mm(ga.reshape(B * T, GLA_LOWRANK), gla_wa, gla_ba).reshape(B, T, GLA_QK)
    if T % GLA_CHUNK == 0:
        o_gla, s_new = _gla(gq, gk, gv, za, gr, gla_g, s0, GLA_CHUNK, 4)
    else:
        Tp = _round_up(T, 2 * SUBLANE)
        pad = lambda a, val=0.0: jnp.pad(a, ((0, 0), (0, Tp - T), (0, 0)), constant_values=val)
        o_gla, s_new = _gla(pad(gq), pad(gk), pad(gv), pad(za, 1e4), pad(gr), gla_g, s0, Tp, 1)
        o_gla = o_gla[:, :T]
    y = _mm(jnp.concatenate([o_nsa, o_gla], axis=-1).reshape(B * T, -1), w_out).reshape(B, T, D_MODEL)
    return y, (new_nsa, win_state, s_new)


def _mixer_ab_decode(h, w_in, w_out, cmp_pe, cmp_w1, cmp_w2, gla_wa, gla_ba, gla_g, pool, page_table, win_buf, s0):
    return _mixer_ab(h, w_in, w_out, cmp_pe, cmp_w1, cmp_w2, gla_wa, gla_ba, gla_g, (pool, page_table, win_buf, s0))


def _mixer_cd(h, w_in, w_out, b_f, a_re, a_im, b_re, b_im, c_re, c_im, d, log_dt, glu_w, glu_b, past):
    B, T, _ = h.shape
    projp = _mm(h.reshape(B * T, D_MODEL), w_in, keep_cols=True).reshape(B, T, -1)
    q, k, v, f, u = _split(projp[..., :sum(CD_SPLITS)], CD_SPLITS)
    log_f = jax.nn.log_sigmoid(f + b_f)
    new_kv = jnp.stack([k.reshape(B, T, FOX_HEADS, HEAD_DIM), v.reshape(B, T, FOX_HEADS, HEAD_DIM)], axis=2)
    h0 = None
    if past is None:
        F = jnp.cumsum(log_f, axis=1)
        o_fox = _attn(projp, 0, FOX_WIDTH, 2 * FOX_WIDTH, G=FOX_HEADS, R=1, tq=256, tk=512,
                      fq=F, fk=F.transpose(0, 2, 1))
    else:
        past_kv, past_lf, h0 = past
        t0 = past_kv.shape[1]
        k_all = jnp.concatenate([past_kv[:, :, 0], k.reshape(B, T, FOX_HEADS, HEAD_DIM)], axis=1)
        v_all = jnp.concatenate([past_kv[:, :, 1], v.reshape(B, T, FOX_HEADS, HEAD_DIM)], axis=1)
        lf_all = jnp.concatenate([past_lf, log_f], axis=1)
        F = jnp.cumsum(lf_all, axis=1)
        Fq = F[:, t0:]
        q_pos = t0 + jnp.arange(T)
        k_pos = jnp.arange(k_all.shape[1])
        bias = (Fq[:, :, None, :] - F[:, None, :, :]).transpose(0, 3, 1, 2)[:, :, None]
        o_fox = _attend(q.reshape(B, T, FOX_HEADS, HEAD_DIM), k_all, v_all,
                        k_pos[None, :] <= q_pos[:, None], bias).reshape(B, T, FOX_WIDTH)
    y_s5, s5_state = _s5(u, a_re, a_im, b_re, b_im, c_re, c_im, d, log_dt, glu_w, glu_b, h0, min(T, 128))
    y = _mm(jnp.concatenate([o_fox, y_s5], axis=-1).reshape(B * T, -1), w_out).reshape(B, T, D_MODEL)
    return y, (new_kv, log_f, s5_state)


def kernel(x_prompt, x_sample, cache_ab_nsa_kv, cache_ab_win_kv, state_ab_gla, cache_cd_fox_kv, cache_cd_fox_logf, state_cd_s5, page_table, c_prompt, c_sample, ada_w, ada_b, norm_g, ab_w_in, ab_w_out, nsa_cmp_pe, nsa_cmp_w1, nsa_cmp_w2, gla_w_alpha, gla_b_alpha, gla_norm_g, cd_w_in, cd_w_out, fox_b_f, s5_a_re, s5_a_im, s5_b_re, s5_b_im, s5_c_re, s5_c_im, s5_d, s5_log_dt, s5_glu_w, s5_glu_b, router_w, router_b, moe_w1, moe_b1, moe_w2, moe_b2, final_norm_g):
    xs = [x_prompt, x_sample]
    cs = [c_prompt, c_sample]
    new = [{}, {}]
    n_prompt = x_prompt.shape[0] * x_prompt.shape[1]
    for l in range(DEPTH):
        j = l // 2
        mods = [jnp.split(_mm(jax.nn.silu(c), ada_w[l], ada_b[l]), 6, axis=-1) for c in cs]
        for grp in range(2):
            x = xs[grp]
            sh1, sc1, g1 = mods[grp][:3]
            hn = _modulated_norm(x, norm_g[l, 0], sh1, sc1)
            if l % 2 == 0:
                past = None
                if grp == 1:
                    past = (cache_ab_nsa_kv[j], page_table, cache_ab_win_kv[j], state_ab_gla[j])
                y, st = _mixer_ab(hn, ab_w_in[j], ab_w_out[j], nsa_cmp_pe[j], nsa_cmp_w1[j], nsa_cmp_w2[j],
                                  gla_w_alpha[j], gla_b_alpha[j], gla_norm_g[j], past)
                names = ('nsa_kv', 'win_kv', 'gla')
            else:
                past = None
                if grp == 1:
                    past = (_gather_pages(cache_cd_fox_kv[j], page_table),
                            _gather_pages(cache_cd_fox_logf[j], page_table), state_cd_s5[j])
                y, st = _mixer_cd(hn, cd_w_in[j], cd_w_out[j], fox_b_f[j], s5_a_re[j], s5_a_im[j], s5_b_re[j],
                                  s5_b_im[j], s5_c_re[j], s5_c_im[j], s5_d[j], s5_log_dt[j], s5_glu_w[j],
                                  s5_glu_b[j], past)
                names = ('fox_kv', 'fox_logf', 's5')
            for name, s in zip(names, st):
                new[grp].setdefault(name, []).append(s)
            xs[grp] = x + g1[:, None, :] * y
        hn2 = [_modulated_norm(xs[grp], norm_g[l, 1], mods[grp][3], mods[grp][4]).reshape(-1, D_MODEL)
               for grp in range(2)]
        ym = _moe(jnp.concatenate(hn2, axis=0), router_w[l], router_b[l], moe_w1[l], moe_b1[l], moe_w2[l], moe_b2[l])
        yms = [ym[:n_prompt], ym[n_prompt:]]
        for grp in range(2):
            xs[grp] = xs[grp] + mods[grp][5][:, None, :] * yms[grp].reshape(xs[grp].shape)
    ys = [_rmsnorm(x, final_norm_g) for x in xs]
    names = ('nsa_kv', 'win_kv', 'gla', 'fox_kv', 'fox_logf', 's5')
    outs = [jnp.stack(new[grp][name]) for grp in range(2) for name in names]
    return (ys[0], ys[1]) + tuple(outs)
```

```python
import functools
import math

import numpy as np
import jax
import jax.numpy as jnp
from jax import lax
from jax.experimental import pallas as pl
from jax.experimental.pallas import tpu as pltpu

D_MODEL = 1024
DEPTH = 2
PAGE_SIZE = 128
HEAD_DIM = 64
NSA_HEADS = 8
NSA_KV_HEADS = 2
CMP_BLK = 32
CMP_STRIDE = 16
SEL_BLK = 64
SEL_TOPN = 16
WINDOW = 512
GLA_HEADS = 4
GLA_DK = 64
GLA_DV = 128
GLA_LOWRANK = 16
GLA_TAU = 16.0
GLA_CHUNK = 64
FOX_HEADS = 8
S5_GROUPS = 32
S5_CH = 16
S5_STATE = 64
N_EXPERTS = 32
TOP_K = 4
D_FF = D_MODEL
SWIGLU_LIMIT = 7.0
SWIGLU_ALPHA = 1.702
EPS = 1e-6
NEG_INF = -1e30
BIG = 1e9
NSA_WIDTH = NSA_HEADS * HEAD_DIM
NSA_KVW = NSA_KV_HEADS * HEAD_DIM
GLA_QK = GLA_HEADS * GLA_DK
GLA_WIDTH = GLA_HEADS * GLA_DV
FOX_WIDTH = FOX_HEADS * HEAD_DIM
S5_WIDTH = S5_GROUPS * S5_CH
AB_SPLITS = (NSA_WIDTH, 6 * NSA_KVW, 3 * NSA_HEADS, GLA_QK, GLA_QK, GLA_WIDTH, GLA_WIDTH, GLA_LOWRANK)
CD_SPLITS = (FOX_WIDTH, FOX_WIDTH, FOX_WIDTH, FOX_HEADS, S5_WIDTH)

LANE = 128
SUBLANE = 8
VMEM_LIMIT = 56 * 1024 * 1024
MOE_TM = 256

F32 = jnp.float32
BF16 = jnp.bfloat16


def _round_up(n, m):
    return -(-n // m) * m


def _pick_tile(n, candidates):
    for c in candidates:
        if n % c == 0:
            return c
    return n


def _mm_kernel(x_ref, w_ref, b_ref, o_ref, wb_ref, *, act, precise):
    if precise:
        acc = jnp.dot(x_ref[...], w_ref[...], preferred_element_type=F32, precision=lax.Precision.HIGHEST)
    else:
        @pl.when(pl.program_id(1) == 0)
        def _():
            wb_ref[...] = w_ref[...].astype(BF16)

        acc = jnp.dot(x_ref[...].astype(BF16), wb_ref[...], preferred_element_type=F32)
    acc = acc + b_ref[...]
    if act == 'gelu':
        acc = jax.nn.gelu(acc)
    o_ref[...] = acc.astype(o_ref.dtype)


def _mm(x, w, b=None, act=None, out_dtype=F32, precise=False, keep_cols=False):
    M, K = x.shape
    N = w.shape[1]
    Np = _round_up(N, 2 * LANE) if N > 2 * LANE else _round_up(N, LANE)
    Mp = _round_up(M, 512) if M > 256 else _round_up(M, SUBLANE)
    tm = min(Mp, 512)
    tn = _pick_tile(Np, (512, 256, 128))
    if Np != N:
        w = jnp.pad(w, ((0, 0), (0, Np - N)))
    if b is None:
        b = jnp.zeros((N,), F32)
    b = jnp.pad(b.astype(F32), (0, Np - N)).reshape(1, Np)
    if Mp != M:
        x = jnp.pad(x, ((0, Mp - M), (0, 0)))
    out = pl.pallas_call(
        functools.partial(_mm_kernel, act=act, precise=precise),
        grid=(Np // tn, Mp // tm),
        in_specs=[pl.BlockSpec((tm, K), lambda j, i: (i, 0)),
                  pl.BlockSpec((K, tn), lambda j, i: (0, j)),
                  pl.BlockSpec((1, tn), lambda j, i: (0, j))],
        out_specs=pl.BlockSpec((tm, tn), lambda j, i: (i, j)),
        out_shape=jax.ShapeDtypeStruct((Mp, Np), out_dtype),
        scratch_shapes=[pltpu.VMEM((K, tn), BF16)],
        compiler_params=pltpu.CompilerParams(
            dimension_semantics=("arbitrary", "arbitrary"), vmem_limit_bytes=VMEM_LIMIT),
        name="mm",
    )(x, w, b)
    return out[:M] if keep_cols else out[:M, :N]


PROJ_TM = 512


def _norm_proj_kernel(x_ref, g_ref, sh_ref, sc_ref, w_ref, o_ref):
    x = x_ref[0]
    y = x * lax.rsqrt(jnp.mean(x * x, axis=-1, keepdims=True) + EPS) * g_ref[...]
    hn = y * (1 + sc_ref[0]) + sh_ref[0]
    o_ref[0] = jnp.dot(hn.astype(BF16), w_ref[...], preferred_element_type=F32)


def _rows_and_mods(x, mods):
    B, T, D = x.shape
    if T % PROJ_TM == 0:
        return x, [m.reshape(B, 1, D) for m in mods]
    assert (B * T) % PROJ_TM == 0 or B * T <= PROJ_TM
    return x.reshape(1, B * T, D), [jnp.repeat(m, T, axis=0).reshape(1, B * T, D) for m in mods]


def _mod_spec(m, tm):
    if m.shape[1] == 1:
        return pl.BlockSpec((1, 1, m.shape[2]), lambda b, i: (b, 0, 0))
    return pl.BlockSpec((1, tm, m.shape[2]), lambda b, i: (b, i, 0))


def _norm_proj(x, g, shift, scale, w):
    B, T, D = x.shape
    N = w.shape[1]
    Np = _round_up(N, 2 * LANE)
    wb = jnp.pad(w, ((0, 0), (0, Np - N))).astype(BF16)
    xr, (sh, sc) = _rows_and_mods(x, [shift, scale])
    Bm, Tm, _ = xr.shape
    tm = min(PROJ_TM, Tm)
    out = pl.pallas_call(
        _norm_proj_kernel,
        grid=(Bm, Tm // tm),
        in_specs=[pl.BlockSpec((1, tm, D), lambda b, i: (b, i, 0)),
                  pl.BlockSpec((1, D), lambda b, i: (0, 0)),
                  _mod_spec(sh, tm), _mod_spec(sc, tm),
                  pl.BlockSpec((D, Np), lambda b, i: (0, 0))],
        out_specs=pl.BlockSpec((1, tm, Np), lambda b, i: (b, i, 0)),
        out_shape=jax.ShapeDtypeStruct((Bm, Tm, Np), F32),
        compiler_params=pltpu.CompilerParams(
            dimension_semantics=("arbitrary", "arbitrary"), vmem_limit_bytes=VMEM_LIMIT),
        name="norm_proj",
    )(xr, g.reshape(1, D), sh, sc, wb)
    return out.reshape(B, T, Np)


def _out_proj_kernel(a_ref, b_ref, x_ref, g_ref, w_ref, o_ref):
    wa = a_ref.shape[2]
    y = jnp.dot(a_ref[0].astype(BF16), w_ref[:wa, :], preferred_element_type=F32)
    y = y + jnp.dot(b_ref[0].astype(BF16), w_ref[wa:, :], preferred_element_type=F32)
    o_ref[0] = x_ref[0] + g_ref[0] * y


def _out_proj(a, b, x, gate, w):
    B, T, D = x.shape
    wa, wb_ = a.shape[2], b.shape[2]
    xr, (g,) = _rows_and_mods(x, [gate])
    Bm, Tm, _ = xr.shape
    tm = min(PROJ_TM, Tm)
    row = lambda b_, i: (b_, i, 0)
    out = pl.pallas_call(
        _out_proj_kernel,
        grid=(Bm, Tm // tm),
        in_specs=[pl.BlockSpec((1, tm, wa), row), pl.BlockSpec((1, tm, wb_), row), pl.BlockSpec((1, tm, D), row),
                  _mod_spec(g, tm), pl.BlockSpec((wa + wb_, D), lambda b_, i: (0, 0))],
        out_specs=pl.BlockSpec((1, tm, D), row),
        out_shape=jax.ShapeDtypeStruct((Bm, Tm, D), F32),
        compiler_params=pltpu.CompilerParams(
            dimension_semantics=("arbitrary", "arbitrary"), vmem_limit_bytes=VMEM_LIMIT),
        name="out_proj",
    )(a.reshape(Bm, Tm, wa), b.reshape(Bm, Tm, wb_), xr, g, w.astype(BF16))
    return out.reshape(B, T, D)


def _flash_kernel(*refs, tq, tk, R, window, use_bias, use_sel):
    q_ref, k_ref, v_ref = refs[:3]
    n = 3
    if use_bias:
        qb_ref, kb_ref = refs[n:n + 2]
        n += 2
    if use_sel:
        sel_ref = refs[n]
        n += 1
    o_ref = refs[n]
    i = pl.program_id(2)
    rows = R * tq
    q = q_ref[0, 0].reshape(rows, HEAD_DIM)
    q_pos = i * tq + jnp.bitwise_and(lax.broadcasted_iota(jnp.int32, (rows, 1), 0), tq - 1)
    hi = ((i + 1) * tq + tk - 1) // tk
    lo = jnp.maximum(i * tq - window + 1, 0) // tk if window else 0
    if use_bias:
        qb = qb_ref[0, 0]
        qb = jnp.concatenate([qb] * (tk // LANE), axis=1)
    if use_sel:
        sel = sel_ref[0, 0].astype(BF16)
        n_sel = sel.shape[1]

    def body(j, carry):
        m, l, acc = carry
        start = pl.multiple_of(j * tk, tk)
        k = k_ref[0, 0, pl.ds(start, tk), :]
        v = v_ref[0, 0, pl.ds(start, tk), :]
        s = lax.dot_general(q, k, (((1,), (1,)), ((), ())), preferred_element_type=F32)
        k_pos = j * tk + lax.broadcasted_iota(jnp.int32, (1, tk), 1)
        if use_bias:
            s = s + qb - kb_ref[0, 0, j]
        s = jnp.where(k_pos <= q_pos, s, NEG_INF)
        if window:
            s = jnp.where(q_pos - k_pos < window, s, NEG_INF)
        if use_sel:
            blk_of_key = (j * tk + lax.broadcasted_iota(jnp.int32, (n_sel, tk), 1)) // SEL_BLK
            expand = jnp.where(blk_of_key == lax.broadcasted_iota(jnp.int32, (n_sel, tk), 0), 1.0, 0.0).astype(BF16)
            chosen = jnp.dot(sel, expand, preferred_element_type=F32)
            chosen = jnp.concatenate([chosen] * R, axis=0)
            s = jnp.where(chosen > 0.5, s, NEG_INF)
        m_new = jnp.maximum(m, jnp.max(s, axis=1, keepdims=True))
        p = jnp.where(s > 0.5 * NEG_INF, jnp.exp(s - m_new), 0.0)
        alpha = jnp.exp(m - m_new)
        l = alpha * l + jnp.sum(p, axis=1, keepdims=True)
        acc = alpha * acc + jnp.dot(p.astype(BF16), v, preferred_element_type=F32)
        return m_new, l, acc

    m0 = jnp.full((rows, 1), NEG_INF, F32)
    l0 = jnp.zeros((rows, 1), F32)
    a0 = jnp.zeros((rows, HEAD_DIM), F32)
    m, l, acc = lax.fori_loop(lo, hi, body, (m0, l0, a0))
    o_ref[0, 0] = (acc / l).reshape(R, tq, HEAD_DIM)


def _flash(q, k, v, *, tq, tk, window=0, qbias=None, kbias=None, sel=None):
    B, G, R, T, _ = q.shape
    grid = (B, G, T // tq)
    in_specs = [pl.BlockSpec((1, 1, R, tq, HEAD_DIM), lambda b, g, i: (b, g, 0, i, 0)),
                pl.BlockSpec((1, 1, T, HEAD_DIM), lambda b, g, i: (b, g, 0, 0)),
                pl.BlockSpec((1, 1, T, HEAD_DIM), lambda b, g, i: (b, g, 0, 0))]
    args = [q, k, v]
    if qbias is not None:
        in_specs += [pl.BlockSpec((1, 1, tq, LANE), lambda b, g, i: (b, g, i, 0)),
                     pl.BlockSpec((1, 1, T // tk, 1, tk), lambda b, g, i: (b, g, 0, 0, 0))]
        args += [qbias, kbias.reshape(B, G, T // tk, 1, tk)]
    if sel is not None:
        in_specs += [pl.BlockSpec((1, 1, tq, sel.shape[-1]), lambda b, g, i: (b, g, i, 0))]
        args += [sel]
    return pl.pallas_call(
        functools.partial(_flash_kernel, tq=tq, tk=tk, R=R, window=window,
                          use_bias=qbias is not None, use_sel=sel is not None),
        grid=grid,
        in_specs=in_specs,
        out_specs=pl.BlockSpec((1, 1, R, tq, HEAD_DIM), lambda b, g, i: (b, g, 0, i, 0)),
        out_shape=jax.ShapeDtypeStruct((B, G, R, T, HEAD_DIM), F32),
        compiler_params=pltpu.CompilerParams(
            dimension_semantics=("arbitrary", "arbitrary", "arbitrary"), vmem_limit_bytes=VMEM_LIMIT),
        name="flash",
    )(*args)


def _attn_kernel(*refs, tq, tk, G, R, window, use_bias, use_sel):
    q_ref, k_ref, v_ref = refs[:3]
    n = 3
    if use_bias:
        fq_ref, fk_ref = refs[n:n + 2]
        n += 2
    if use_sel:
        sel_ref = refs[n]
        n += 1
    o_ref, kb_ref, vb_ref = refs[n:n + 3]
    i = pl.program_id(1)

    @pl.when(i == 0)
    def _():
        kb_ref[...] = k_ref[0].astype(BF16)
        vb_ref[...] = v_ref[0].astype(BF16)

    rows = R * tq
    q_pos = i * tq + jnp.bitwise_and(lax.broadcasted_iota(jnp.int32, (rows, 1), 0), tq - 1)
    hi = ((i + 1) * tq + tk - 1) // tk
    lo = jnp.maximum(i * tq - window + 1, 0) // tk if window else 0
    outs = []
    for g in range(G):
        cols = slice(g * HEAD_DIM, (g + 1) * HEAD_DIM)
        qg = q_ref[0, :, g * R * HEAD_DIM:(g + 1) * R * HEAD_DIM] * HEAD_DIM ** -0.5
        q = jnp.concatenate([qg[:, r * HEAD_DIM:(r + 1) * HEAD_DIM] for r in range(R)], axis=0).astype(BF16)
        if use_bias:
            fq = fq_ref[0, :, g:g + 1]
        if use_sel:
            sel = sel_ref[0, g].astype(BF16)
            n_sel = sel.shape[1]

        def body(j, carry):
            m, l, acc = carry
            start = pl.multiple_of(j * tk, tk)
            k = kb_ref[pl.ds(start, tk), cols]
            v = vb_ref[pl.ds(start, tk), cols]
            s = lax.dot_general(q, k, (((1,), (1,)), ((), ())), preferred_element_type=F32)
            k_pos = j * tk + lax.broadcasted_iota(jnp.int32, (1, tk), 1)
            if use_bias:
                s = s + fq - fk_ref[0, g, j]
            s = jnp.where(k_pos <= q_pos, s, NEG_INF)
            if window:
                s = jnp.where(q_pos - k_pos < window, s, NEG_INF)
            if use_sel:
                blk_of_key = (j * tk + lax.broadcasted_iota(jnp.int32, (n_sel, tk), 1)) // SEL_BLK
                expand = jnp.where(blk_of_key == lax.broadcasted_iota(jnp.int32, (n_sel, tk), 0), 1.0, 0.0)
                chosen = jnp.dot(sel, expand.astype(BF16), preferred_element_type=F32)
                chosen = jnp.concatenate([chosen] * R, axis=0)
                s = jnp.where(chosen > 0.5, s, NEG_INF)
            m_new = jnp.maximum(m, jnp.max(s, axis=1, keepdims=True))
            p = jnp.where(s > 0.5 * NEG_INF, jnp.exp(s - m_new), 0.0)
            alpha = jnp.exp(m - m_new)
            l = alpha * l + jnp.sum(p, axis=1, keepdims=True)
            acc = alpha * acc + jnp.dot(p.astype(BF16), v, preferred_element_type=F32)
            return m_new, l, acc

        m0 = jnp.full((rows, 1), NEG_INF, F32)
        l0 = jnp.zeros((rows, 1), F32)
        a0 = jnp.zeros((rows, HEAD_DIM), F32)
        m, l, acc = lax.fori_loop(lo, hi, body, (m0, l0, a0))
        o = acc / l
        outs += [o[r * tq:(r + 1) * tq] for r in range(R)]
    o_ref[0] = jnp.concatenate(outs, axis=1)


def _attn(proj, q_col, k_col, v_col, *, G, R, tq, tk, window=0, fq=None, fk=None, sel=None):
    B, T, _ = proj.shape
    qw, kw = G * R * HEAD_DIM, G * HEAD_DIM
    assert q_col % qw == 0 and k_col % kw == 0 and v_col % kw == 0
    in_specs = [pl.BlockSpec((1, tq, qw), lambda b, i: (b, i, q_col // qw)),
                pl.BlockSpec((1, T, kw), lambda b, i: (b, 0, k_col // kw)),
                pl.BlockSpec((1, T, kw), lambda b, i: (b, 0, v_col // kw))]
    args = [proj, proj, proj]
    if fq is not None:
        in_specs += [pl.BlockSpec((1, tq, G), lambda b, i: (b, i, 0)),
                     pl.BlockSpec((1, G, T // tk, 1, tk), lambda b, i: (b, 0, 0, 0, 0))]
        args += [fq, fk.reshape(B, G, T // tk, 1, tk)]
    if sel is not None:
        in_specs += [pl.BlockSpec((1, G, tq, sel.shape[-1]), lambda b, i: (b, 0, i, 0))]
        args += [sel]
    return pl.pallas_call(
        functools.partial(_attn_kernel, tq=tq, tk=tk, G=G, R=R, window=window,
                          use_bias=fq is not None, use_sel=sel is not None),
        grid=(B, T // tq),
        in_specs=in_specs,
        out_specs=pl.BlockSpec((1, tq, qw), lambda b, i: (b, i, 0)),
        out_shape=jax.ShapeDtypeStruct((B, T, qw), F32),
        scratch_shapes=[pltpu.VMEM((T, kw), BF16), pltpu.VMEM((T, kw), BF16)],
        compiler_params=pltpu.CompilerParams(
            dimension_semantics=("arbitrary", "arbitrary"), vmem_limit_bytes=VMEM_LIMIT),
        name="attn",
    )(*args)


def _decode_attn_kernel(*refs, n_pages):
    pt_ref, q_ref = refs[0], refs[1]
    del pt_ref
    k_refs = refs[2:2 + n_pages]
    v_refs = refs[2 + n_pages:2 + 2 * n_pages]
    n = 2 + 2 * n_pages
    if n_pages:
        bp_ref = refs[n]
        n += 1
    ke_ref, ve_ref, be_ref, o_ref = refs[n:n + 4]
    nt = (((1,), (1,)), ((), ()))
    q = q_ref[0].astype(BF16)
    se = lax.dot_general(q, ke_ref[0].astype(BF16), nt, preferred_element_type=F32) + be_ref[0]
    m = jnp.max(se, axis=1, keepdims=True)
    if n_pages:
        s = jnp.concatenate([lax.dot_general(q, k_refs[p][0].astype(BF16), nt, preferred_element_type=F32)
                             for p in range(n_pages)], axis=1) + bp_ref[0]
        m = jnp.maximum(m, jnp.max(s, axis=1, keepdims=True))
    pe = jnp.exp(se - m)
    l = jnp.sum(pe, axis=1, keepdims=True)
    o = jnp.dot(pe.astype(BF16), ve_ref[0].astype(BF16), preferred_element_type=F32)
    if n_pages:
        p = jnp.exp(s - m)
        l = l + jnp.sum(p, axis=1, keepdims=True)
        pb = p.astype(BF16)
        for pg in range(n_pages):
            o = o + jnp.dot(pb[:, pg * PAGE_SIZE:(pg + 1) * PAGE_SIZE], v_refs[pg][0].astype(BF16),
                            preferred_element_type=F32)
    o_ref[0] = o / l


def _decode_attn(qblk, k_extra, v_extra, bias_extra, pool=None, page_table=None, k_col=0, v_col=0, bias_pages=None):
    B, R, KW = qblk.shape
    NE = k_extra.shape[1]
    n_pages = 0 if pool is None else page_table.shape[1]
    row3 = lambda b, pt: (b, 0, 0)
    in_specs = [pl.BlockSpec((1, R, KW), row3)]
    args = [qblk]
    if n_pages:
        assert k_col % KW == 0 and v_col % KW == 0
        for col in (k_col, v_col):
            for p in range(n_pages):
                in_specs.append(pl.BlockSpec((1, PAGE_SIZE, KW),
                                             functools.partial(lambda b, pt, p, c: (pt[b, p], 0, c), p=p, c=col // KW)))
                args.append(pool)
        in_specs.append(pl.BlockSpec((1, R, n_pages * PAGE_SIZE), row3))
        args.append(bias_pages)
    else:
        page_table = jnp.zeros((1, 1), jnp.int32)
    be_map = row3 if bias_extra.shape[0] == B else (lambda b, pt: (0, 0, 0))
    in_specs += [pl.BlockSpec((1, NE, KW), row3), pl.BlockSpec((1, NE, KW), row3), pl.BlockSpec((1, R, NE), be_map)]
    args += [k_extra, v_extra, bias_extra]
    grid_spec = pltpu.PrefetchScalarGridSpec(
        num_scalar_prefetch=1, grid=(B,), in_specs=in_specs,
        out_specs=pl.BlockSpec((1, R, KW), row3))
    return pl.pallas_call(
        functools.partial(_decode_attn_kernel, n_pages=n_pages),
        grid_spec=grid_spec,
        out_shape=jax.ShapeDtypeStruct((B, R, KW), F32),
        compiler_params=pltpu.CompilerParams(dimension_semantics=("arbitrary",), vmem_limit_bytes=VMEM_LIMIT),
        name="decode_attn",
    )(page_table, *args)


def _block_diag_queries(q, G):
    B, T, W = q.shape
    H = W // HEAD_DIM
    own = (jnp.arange(H)[:, None] // (H // G) == jnp.arange(G)[None, :]).astype(F32)
    qs = (q * HEAD_DIM ** -0.5).reshape(B, T, H, 1, HEAD_DIM) * own[None, None, :, :, None]
    return qs.reshape(B, T * H, G * HEAD_DIM)


def _own_head_columns(o, T, G):
    B, R, _ = o.shape
    H = R // T
    own = (jnp.arange(H)[:, None] // (H // G) == jnp.arange(G)[None, :]).astype(F32)
    o5 = o.reshape(B, T, H, G, HEAD_DIM) * own[None, None, :, :, None]
    return jnp.sum(o5, axis=3).reshape(B, T, H * HEAD_DIM)


def _moe_kernel(be_ref, nb_ref, x_ref, w1_ref, b1_ref, w2_ref, b2_ref, o_ref, w1b_ref, w2b_ref):
    i = pl.program_id(0)
    prev = be_ref[jnp.maximum(i - 1, 0)]

    @pl.when(jnp.logical_or(i == 0, be_ref[i] != prev))
    def _():
        w1b_ref[...] = w1_ref[0].astype(BF16)
        w2b_ref[...] = w2_ref[0].astype(BF16)

    @pl.when(i < nb_ref[0])
    def _():
        gu = jnp.dot(x_ref[...].astype(BF16), w1b_ref[...], preferred_element_type=F32) + b1_ref[0]
        g = jnp.minimum(gu[:, :D_FF], SWIGLU_LIMIT)
        up = jnp.clip(gu[:, D_FF:], -SWIGLU_LIMIT, SWIGLU_LIMIT)
        h = (up + 1) * g * jax.nn.sigmoid(SWIGLU_ALPHA * g)
        o_ref[...] = jnp.dot(h.astype(BF16), w2b_ref[...], preferred_element_type=F32) + b2_ref[0]

    @pl.when(i >= nb_ref[0])
    def _():
        o_ref[...] = jnp.zeros_like(o_ref)


def _moe_experts(xs, blk_e, n_used, w1, b1, w2, b2):
    n_blocks = xs.shape[0] // MOE_TM
    grid_spec = pltpu.PrefetchScalarGridSpec(
        num_scalar_prefetch=2,
        grid=(n_blocks,),
        in_specs=[pl.BlockSpec((MOE_TM, D_MODEL), lambda i, be, nb: (i, 0)),
                  pl.BlockSpec((1, D_MODEL, 2 * D_FF), lambda i, be, nb: (be[i], 0, 0)),
                  pl.BlockSpec((1, 1, 2 * D_FF), lambda i, be, nb: (be[i], 0, 0)),
                  pl.BlockSpec((1, D_FF, D_MODEL), lambda i, be, nb: (be[i], 0, 0)),
                  pl.BlockSpec((1, 1, D_MODEL), lambda i, be, nb: (be[i], 0, 0))],
        out_specs=pl.BlockSpec((MOE_TM, D_MODEL), lambda i, be, nb: (i, 0)),
        scratch_shapes=[pltpu.VMEM((D_MODEL, 2 * D_FF), BF16), pltpu.VMEM((D_FF, D_MODEL), BF16)],
    )
    return pl.pallas_call(
        _moe_kernel,
        grid_spec=grid_spec,
        out_shape=jax.ShapeDtypeStruct((n_blocks * MOE_TM, D_MODEL), F32),
        compiler_params=pltpu.CompilerParams(
            dimension_semantics=("arbitrary",), vmem_limit_bytes=VMEM_LIMIT),
        name="moe_experts",
    )(blk_e, n_used, xs, w1, b1.reshape(N_EXPERTS, 1, 2 * D_FF), w2, b2.reshape(N_EXPERTS, 1, D_MODEL))


def _moe_combine_kernel(y_ref, g_ref, o_ref):
    g = g_ref[...]
    acc = g[:, 0:1] * y_ref[:, :D_MODEL]
    for k in range(1, TOP_K):
        acc = acc + g[:, k:k + 1] * y_ref[:, k * D_MODEL:(k + 1) * D_MODEL]
    o_ref[...] = acc


def _moe_combine(yg, gate):
    N = yg.shape[0]
    tn = _pick_tile(N, (512, 256, 128, 64, 32, 16, 8))
    return pl.pallas_call(
        _moe_combine_kernel,
        grid=(N // tn,),
        in_specs=[pl.BlockSpec((tn, TOP_K * D_MODEL), lambda i: (i, 0)),
                  pl.BlockSpec((tn, TOP_K), lambda i: (i, 0))],
        out_specs=pl.BlockSpec((tn, D_MODEL), lambda i: (i, 0)),
        out_shape=jax.ShapeDtypeStruct((N, D_MODEL), F32),
        compiler_params=pltpu.CompilerParams(dimension_semantics=("arbitrary",), vmem_limit_bytes=VMEM_LIMIT),
        name="moe_combine",
    )(yg, gate)


def _moe(xf, w_r, b_r, w1, b1, w2, b2):
    N = xf.shape[0]
    n_rows = N * TOP_K
    logits = _mm(xf, w_r, b_r, precise=True)
    top_v, top_i = lax.top_k(logits, TOP_K)
    gate = jax.nn.softmax(top_v, axis=-1)
    flat_e = top_i.reshape(-1)
    order = jnp.argsort(flat_e)
    sorted_e = flat_e[order]
    counts = jnp.bincount(flat_e, length=N_EXPERTS)
    padded = (counts + MOE_TM - 1) // MOE_TM * MOE_TM
    pad_end = jnp.cumsum(padded)
    pad_start = pad_end - padded
    start = jnp.cumsum(counts) - counts
    dest = (pad_start[sorted_e] + jnp.arange(n_rows) - start[sorted_e]).astype(jnp.int32)
    n_blocks = -(-(n_rows + N_EXPERTS * (MOE_TM - 1)) // MOE_TM)
    n_used = (pad_end[-1] // MOE_TM).astype(jnp.int32)
    blk = jnp.minimum(jnp.arange(n_blocks), n_used - 1) * MOE_TM
    blk_e = jnp.minimum(jnp.sum(pad_end[None, :] <= blk[:, None], axis=1), N_EXPERTS - 1).astype(jnp.int32)
    row_e = jnp.repeat(blk_e, MOE_TM)
    off = jnp.arange(n_blocks * MOE_TM) - pad_start[row_e]
    src = jnp.clip(start[row_e] + off, 0, n_rows - 1)
    row_tok = jnp.where(off < counts[row_e], (order // TOP_K)[src], 0).astype(jnp.int32)
    xs = xf[row_tok]
    ys = _moe_experts(xs, blk_e, n_used.reshape(1), w1, b1, w2, b2)
    pos = dest[jnp.argsort(order)]
    return _moe_combine(ys[pos].reshape(N, TOP_K * D_MODEL), gate)


def _rmsnorm(x, g):
    y = x * lax.rsqrt(jnp.mean(x * x, axis=-1, keepdims=True) + EPS)
    return y * g


def _modulated_norm(x, g, shift, scale):
    return _rmsnorm(x, g) * (1 + scale[:, None, :]) + shift[:, None, :]


def _split(x, sizes):
    return jnp.split(x, np.cumsum(sizes)[:-1].tolist(), axis=-1)


def _gather_pages(pool, page_table):
    g = pool[page_table]
    return g.reshape((g.shape[0], g.shape[1] * g.shape[2]) + g.shape[3:])


def _attend(q, k, v, mask, bias=None):
    B, Q, H, D = q.shape
    G = k.shape[2]
    s = jnp.einsum('bqgnd,bkgd->bgnqk', q.reshape(B, Q, G, H // G, D), k).astype(F32) * D ** -0.5
    if bias is not None:
        s = s + bias
    p = jax.nn.softmax(jnp.where(mask, s, NEG_INF), axis=-1).astype(v.dtype)
    return jnp.einsum('bgnqk,bkgd->bqgnd', p, v).reshape(B, Q, H, D)


def _window_mask(q_pos, k_pos):
    d = q_pos[:, None] - k_pos[None, :]
    return (d >= 0) & (d < WINDOW) & (k_pos[None, :] >= 0)


def _to_heads(x, G):
    B, T, _ = x.shape
    return x.astype(BF16).reshape(B, T, G, HEAD_DIM).transpose(0, 2, 1, 3)


def _q_to_heads(q, G):
    B, T, W = q.shape
    R = W // HEAD_DIM // G
    return (q * HEAD_DIM ** -0.5).astype(BF16).reshape(B, T, G, R, HEAD_DIM).transpose(0, 2, 3, 1, 4)


def _from_heads(o):
    B, G, R, T, D = o.shape
    return o.transpose(0, 3, 1, 2, 4).reshape(B, T, G * R * D)


CMP_CHUNK_W = CMP_STRIDE * HEAD_DIM
CMP_HID = 256


def _cmp_chunk_proj(tokens, w1):
    assert CMP_BLK == 2 * CMP_STRIDE
    S, T, G, D = tokens.shape
    n = T // CMP_STRIDE
    ch = tokens.reshape(S, n, CMP_STRIDE, G, D).transpose(0, 1, 3, 2, 4).reshape(S * n * G, CMP_CHUNK_W)
    wcat = jnp.concatenate([w1[:CMP_CHUNK_W], w1[CMP_CHUNK_W:]], axis=1)
    return _mm(ch, wcat).reshape(S, n, G, 2 * CMP_HID)


def _cmp_out_kernel(a_ref, b_ref, bias_ref, w_ref, o_ref):
    hid = jax.nn.gelu(a_ref[...] + b_ref[...] + bias_ref[...])
    o_ref[...] = jnp.dot(hid.astype(BF16), w_ref[...].astype(BF16), preferred_element_type=F32)


def _cmp_finish(proj, pe, w1, w2):
    B, n, G, _ = proj.shape
    first = proj[:, :-1, :, :CMP_HID].reshape(-1, CMP_HID)
    second = proj[:, 1:, :, CMP_HID:].reshape(-1, CMP_HID)
    bias = _mm(pe.reshape(1, CMP_BLK * HEAD_DIM), w1)
    rows = first.shape[0]
    rows_p = _round_up(rows, 512)
    pad = lambda a: jnp.pad(a, ((0, rows_p - rows), (0, 0)))
    w2p = jnp.pad(w2, ((0, 0), (0, LANE - HEAD_DIM)))
    out = pl.pallas_call(
        _cmp_out_kernel,
        grid=(rows_p // 512,),
        in_specs=[pl.BlockSpec((512, CMP_HID), lambda i: (i, 0)), pl.BlockSpec((512, CMP_HID), lambda i: (i, 0)),
                  pl.BlockSpec((1, CMP_HID), lambda i: (0, 0)), pl.BlockSpec((CMP_HID, LANE), lambda i: (0, 0))],
        out_specs=pl.BlockSpec((512, LANE), lambda i: (i, 0)),
        out_shape=jax.ShapeDtypeStruct((rows_p, LANE), F32),
        compiler_params=pltpu.CompilerParams(dimension_semantics=("arbitrary",), vmem_limit_bytes=VMEM_LIMIT),
        name="cmp_out",
    )(pad(first), pad(second), bias, w2p)
    return out[:rows, :HEAD_DIM].reshape(B, n - 1, G, HEAD_DIM)


def _cmp_attention(q, kc, vc, q_pos):
    B, Q, H, D = q.shape
    N, G = kc.shape[1], kc.shape[2]
    c_end = jnp.arange(N) * CMP_STRIDE + CMP_BLK
    valid = c_end[None, :] <= q_pos[:, None] + 1
    s = jnp.einsum('bqgnd,bcgd->bgnqc', q.reshape(B, Q, G, H // G, D), kc).astype(F32) * D ** -0.5
    p = jax.nn.softmax(jnp.where(valid, s, NEG_INF), axis=-1) * valid
    o = jnp.einsum('bgnqc,bcgd->bqgnd', p.astype(vc.dtype), vc).reshape(B, Q, H, D)
    return o, p


def _cmp_to_sel(n_cmp, n_sel):
    c0 = jnp.arange(n_cmp)[:, None] * CMP_STRIDE
    s0 = jnp.arange(n_sel)[None, :] * SEL_BLK
    ov = jnp.minimum(c0 + CMP_BLK, s0 + SEL_BLK) - jnp.maximum(c0, s0)
    return jnp.maximum(ov, 0).astype(F32) / CMP_BLK


def _selection_scores(p_cmp, q_pos, n_sel):
    imp = jnp.einsum('bgnqc,cs->bgqs', p_cmp, _cmp_to_sel(p_cmp.shape[-1], n_sel))
    blk = jnp.arange(n_sel)[None, :]
    cur = (q_pos // SEL_BLK)[:, None]
    valid = blk * SEL_BLK <= q_pos[:, None]
    forced = (blk == 0) | (blk == cur) | (blk == cur - 1)
    return jnp.where(valid, jnp.where(forced, BIG, imp), -BIG)


def _selection_mask(score):
    n_sel = score.shape[-1]
    a = score[..., :, None]
    b = score[..., None, :]
    lower = jnp.arange(n_sel)[None, :] < jnp.arange(n_sel)[:, None]
    beats = (b > a) | ((b == a) & lower)
    rank = jnp.sum(beats, axis=-1)
    return (rank < min(SEL_TOPN, n_sel)).astype(F32)


def _selected_attention_gather(q, k, v, p_cmp, q_pos):
    B, Q, H, D = q.shape
    T_all, G = k.shape[1], k.shape[2]
    n_sel = -(-T_all // SEL_BLK)
    score = _selection_scores(p_cmp, q_pos, n_sel)
    _, idx = lax.top_k(score, min(SEL_TOPN, n_sel))
    tok = (idx[..., None] * SEL_BLK + jnp.arange(SEL_BLK)).reshape(B, G, Q, -1)
    pad = ((0, 0), (0, n_sel * SEL_BLK - T_all), (0, 0), (0, 0))
    kt = jnp.pad(k, pad).transpose(0, 2, 1, 3)
    vt = jnp.pad(v, pad).transpose(0, 2, 1, 3)
    b_ix = jnp.arange(B)[:, None, None]
    g_ix = jnp.arange(G)[None, :, None]
    L = tok.shape[3]
    flat = tok.reshape(B, G, Q * L)
    ks = kt[b_ix, g_ix, flat].reshape(B, G, Q, L, D)
    vs = vt[b_ix, g_ix, flat].reshape(B, G, Q, L, D)
    s = jnp.einsum('bqgnd,bgqld->bgnql', q.reshape(B, Q, G, H // G, D), ks).astype(F32) * D ** -0.5
    mask = (tok <= q_pos[None, None, :, None])[:, :, None]
    p = jax.nn.softmax(jnp.where(mask, s, NEG_INF), axis=-1)
    return jnp.einsum('bgnql,bgqld->bqgnd', p, vs).reshape(B, Q, H, D)


def _gla_kernel(q_ref, k_ref, v_ref, za_ref, gr_ref, g_ref, s0_ref, o_ref, sT_ref, st_ref, *, C, n_chunks):
    @pl.when(pl.program_id(1) == 0)
    def _():
        st_ref[...] = s0_ref[0]

    tri = (lax.broadcasted_iota(jnp.int32, (C, C), 0) >= lax.broadcasted_iota(jnp.int32, (C, C), 1))
    tri_f = jnp.where(tri, 1.0, 0.0)
    for c in range(n_chunks):
        rows = pl.ds(c * C, C)
        log_a = jax.nn.log_sigmoid(za_ref[0, rows, :]) / GLA_TAU
        b = jnp.dot(tri_f, log_a, preferred_element_type=F32, precision=lax.Precision.HIGHEST)
        b_last = b[C - 1:C, :]
        q = q_ref[0, rows, :] * GLA_DK ** -0.5
        k = k_ref[0, rows, :]
        q_dec = (q * jnp.exp(b)).astype(BF16)
        k_inv = (k * jnp.exp(-b)).astype(BF16)
        k_end = (k * jnp.exp(b_last - b)).astype(BF16)
        decay = jnp.exp(b_last)
        v = v_ref[0, rows, :].astype(BF16)
        gr = gr_ref[0, rows, :]
        outs = []
        for h in range(GLA_HEADS):
            kh = slice(h * GLA_DK, (h + 1) * GLA_DK)
            vh = slice(h * GLA_DV, (h + 1) * GLA_DV)
            attn = lax.dot_general(q_dec[:, kh], k_inv[:, kh], (((1,), (1,)), ((), ())), preferred_element_type=F32)
            attn = jnp.where(tri, attn, 0.0).astype(BF16)
            sT = st_ref[h]
            o = jnp.dot(attn, v[:, vh], preferred_element_type=F32)
            o = o + lax.dot_general(q_dec[:, kh], sT.astype(BF16), (((1,), (1,)), ((), ())), preferred_element_type=F32)
            st_ref[h] = decay[:, kh] * sT + lax.dot_general(v[:, vh], k_end[:, kh], (((0,), (0,)), ((), ())),
                                                            preferred_element_type=F32)
            o = o * lax.rsqrt(jnp.mean(o * o, axis=-1, keepdims=True) + EPS) * g_ref[...]
            outs.append(o * jax.nn.silu(gr[:, vh]))
        o_ref[0, rows, :] = jnp.concatenate(outs, axis=1)

    @pl.when(pl.program_id(1) == pl.num_programs(1) - 1)
    def _():
        sT_ref[0] = st_ref[...]


def _gla(gq, gk, gv, za, gr, gla_g, s0, C, n_chunks):
    B, T, _ = gq.shape
    tt = C * n_chunks
    s0T = s0.transpose(0, 1, 3, 2)
    qk_spec = pl.BlockSpec((1, tt, GLA_QK), lambda b, i: (b, i, 0))
    v_spec = pl.BlockSpec((1, tt, GLA_WIDTH), lambda b, i: (b, i, 0))
    st_spec = pl.BlockSpec((1, GLA_HEADS, GLA_DV, GLA_DK), lambda b, i: (b, 0, 0, 0))
    o, sT = pl.pallas_call(
        functools.partial(_gla_kernel, C=C, n_chunks=n_chunks),
        grid=(B, T // tt),
        in_specs=[qk_spec, qk_spec, v_spec, qk_spec, v_spec,
                  pl.BlockSpec((1, GLA_DV), lambda b, i: (0, 0)), st_spec],
        out_specs=[v_spec, st_spec],
        out_shape=[jax.ShapeDtypeStruct((B, T, GLA_WIDTH), F32),
                   jax.ShapeDtypeStruct((B, GLA_HEADS, GLA_DV, GLA_DK), F32)],
        scratch_shapes=[pltpu.VMEM((GLA_HEADS, GLA_DV, GLA_DK), F32)],
        compiler_params=pltpu.CompilerParams(
            dimension_semantics=("arbitrary", "arbitrary"), vmem_limit_bytes=VMEM_LIMIT),
        name="gla",
    )(gq, gk, gv, za, gr, gla_g.reshape(1, GLA_DV), s0T)
    return o, sT.transpose(0, 1, 3, 2)


S5_NK = 4
S5_GPK = S5_GROUPS // S5_NK
S5_HALF = S5_GPK * S5_STATE


def _s5_kernel(u_ref, wb_ref, wc_ref, coef_ref, d_ref, gw_ref, gb_ref, h0_ref, y_ref, hfin_ref, hst_ref, xb_ref,
               *, BT, Tc):
    @pl.when(pl.program_id(0) == 0)
    def _():
        hst_ref[...] = h0_ref[...]

    u = u_ref[...]
    ys = []
    for k in range(S5_NK):
        bu = jnp.dot(u[:, k * LANE:(k + 1) * LANE].astype(BF16), wb_ref[k], preferred_element_type=F32)
        bre, bim = bu[:, :S5_HALF], bu[:, S5_HALF:]
        ar, ai = coef_ref[k, 0:1, :], coef_ref[k, 1:2, :]
        cr, ci = coef_ref[k, 2:3, :], coef_ref[k, 3:4, :]
        xb_ref[:, :S5_HALF] = cr * bre - ci * bim
        xb_ref[:, S5_HALF:] = cr * bim + ci * bre

        def step(t, carry):
            hr, hi = carry
            rows = pl.ds(pl.multiple_of(t * BT, BT), BT)
            hr2 = ar * hr - ai * hi + xb_ref[rows, :S5_HALF]
            hi2 = ar * hi + ai * hr + xb_ref[rows, S5_HALF:]
            xb_ref[rows, :S5_HALF] = hr2
            xb_ref[rows, S5_HALF:] = hi2
            return hr2, hi2

        hr, hi = lax.fori_loop(0, Tc, step, (hst_ref[k, :, :S5_HALF], hst_ref[k, :, S5_HALF:]),
                               unroll=min(Tc, 8))
        hst_ref[k, :, :S5_HALF] = hr
        hst_ref[k, :, S5_HALF:] = hi
        ys.append(jnp.dot(xb_ref[...].astype(BF16), wc_ref[k], preferred_element_type=F32))
    y = jax.nn.gelu(jnp.concatenate(ys, axis=1) + d_ref[...] * u)
    z = jnp.dot(y.astype(BF16), gw_ref[...].astype(BF16), preferred_element_type=F32) + gb_ref[...]
    y_ref[...] = y * jax.nn.sigmoid(z)

    @pl.when(pl.program_id(0) == pl.num_programs(0) - 1)
    def _():
        hfin_ref[...] = hst_ref[...]


def _s5(u, a_re, a_im, b_re, b_im, c_re, c_im, d, log_dt, glu_w, glu_b, h0, Tc):
    B, T, _ = u.shape
    dt = jnp.exp(log_dt)[:, None]
    mag = jnp.exp(a_re * dt)
    ab_re, ab_im = mag * jnp.cos(a_im * dt), mag * jnp.sin(a_im * dt)
    den = a_re * a_re + a_im * a_im
    coef_re = ((ab_re - 1) * a_re + ab_im * a_im) / den
    coef_im = (ab_im * a_re - (ab_re - 1) * a_im) / den
    coefs = jnp.stack([ab_re, ab_im, coef_re, coef_im], axis=0).reshape(4, S5_NK, S5_HALF).transpose(1, 0, 2)
    eye = jnp.eye(S5_GPK, dtype=F32)

    def in_weights(bm):
        bk = bm.reshape(S5_NK, S5_GPK, S5_STATE, S5_CH)
        return jnp.einsum('kgpc,gh->kgchp', bk, eye).reshape(S5_NK, S5_GPK * S5_CH, S5_HALF)

    def out_weights(cm):
        ck = cm.reshape(S5_NK, S5_GPK, S5_CH, S5_STATE)
        return jnp.einsum('kgcp,gh->kgphc', ck, eye).reshape(S5_NK, S5_HALF, S5_GPK * S5_CH)

    wb = jnp.concatenate([in_weights(b_re), in_weights(b_im)], axis=2).astype(BF16)
    wc = jnp.concatenate([out_weights(c_re), -out_weights(c_im)], axis=1).astype(BF16)
    if h0 is None:
        hs0 = jnp.zeros((S5_NK, B, 2 * S5_HALF), F32)
    else:
        hs0 = h0.reshape(B, S5_NK, S5_HALF, 2).transpose(1, 0, 3, 2).reshape(S5_NK, B, 2 * S5_HALF)
    ut = u.transpose(1, 0, 2).reshape(T * B, S5_WIDTH)
    rows = Tc * B
    const2 = lambda i: (0, 0)
    const3 = lambda i: (0, 0, 0)
    y, hfin = pl.pallas_call(
        functools.partial(_s5_kernel, BT=B, Tc=Tc),
        grid=(T // Tc,),
        in_specs=[pl.BlockSpec((rows, S5_WIDTH), lambda i: (i, 0)),
                  pl.BlockSpec(wb.shape, const3), pl.BlockSpec(wc.shape, const3), pl.BlockSpec(coefs.shape, const3),
                  pl.BlockSpec((1, S5_WIDTH), const2), pl.BlockSpec((S5_WIDTH, S5_WIDTH), const2),
                  pl.BlockSpec((1, S5_WIDTH), const2), pl.BlockSpec(hs0.shape, const3)],
        out_specs=[pl.BlockSpec((rows, S5_WIDTH), lambda i: (i, 0)), pl.BlockSpec(hs0.shape, const3)],
        out_shape=[jax.ShapeDtypeStruct((T * B, S5_WIDTH), F32), jax.ShapeDtypeStruct(hs0.shape, F32)],
        scratch_shapes=[pltpu.VMEM(hs0.shape, F32), pltpu.VMEM((rows, 2 * S5_HALF), F32)],
        compiler_params=pltpu.CompilerParams(dimension_semantics=("arbitrary",), vmem_limit_bytes=VMEM_LIMIT),
        name="s5",
    )(ut, wb, wc, coefs, d.reshape(1, S5_WIDTH), glu_w, glu_b.reshape(1, S5_WIDTH), hs0)
    y = y.reshape(T, B, S5_WIDTH).transpose(1, 0, 2)
    hfin = hfin.reshape(S5_NK, B, 2, S5_HALF).transpose(1, 0, 3, 2).reshape(B, S5_GROUPS, S5_STATE, 2)
    return y, hfin


def _mixer_ab(x, norm, w_in, w_out, cmp_pe, cmp_w1, cmp_w2, gla_wa, gla_ba, gla_g, past):
    B, T, _ = x.shape
    projp = _norm_proj(x, norm[0], norm[1], norm[2], w_in)
    q, kv6, gate_logit, gq, gk, gv, gr, ga = _split(projp[..., :sum(AB_SPLITS)], AB_SPLITS)
    kv6 = kv6.reshape(B, T, 6, NSA_KV_HEADS, HEAD_DIM)
    new_nsa, new_win = kv6[:, :, :4], kv6[:, :, 4:]
    s0 = jnp.zeros((B, GLA_HEADS, GLA_DK, GLA_DV), F32)
    if past is None:
        t0 = 0
        kv_full, kv_win = new_nsa, new_win
        cmp_proj = [_cmp_chunk_proj(new_nsa[:, :T // CMP_STRIDE * CMP_STRIDE, c], cmp_w1[c]) for c in range(2)]
    else:
        pool, page_table, win_buf, s0 = past
        pool = lax.optimization_barrier(pool.astype(BF16).reshape(pool.shape[0], PAGE_SIZE, 4 * NSA_KVW))
        pool_kind = lambda c: pool[:, :, c * NSA_KVW:(c + 1) * NSA_KVW].reshape(-1, PAGE_SIZE, NSA_KV_HEADS, HEAD_DIM)
        n_pages = page_table.shape[1]
        t0 = n_pages * PAGE_SIZE
        assert PAGE_SIZE % CMP_STRIDE == 0 and T < CMP_STRIDE
        cmp_proj = [_cmp_chunk_proj(pool_kind(c), cmp_w1[c])[page_table].reshape(
            B, n_pages * (PAGE_SIZE // CMP_STRIDE), NSA_KV_HEADS, 2 * CMP_HID) for c in range(2)]
        kv_win = jnp.concatenate([win_buf, new_win], axis=1)
    q_pos = t0 + jnp.arange(T)
    q4 = q.reshape(B, T, NSA_HEADS, HEAD_DIM)
    kc = _cmp_finish(cmp_proj[0], cmp_pe[0], cmp_w1[0], cmp_w2[0])
    vc = _cmp_finish(cmp_proj[1], cmp_pe[1], cmp_w1[1], cmp_w2[1])
    o_cmp, p_cmp = _cmp_attention(q4, kc, vc, q_pos)
    if past is None:
        n_sel = -(-T // SEL_BLK)
        sel = _selection_mask(_selection_scores(p_cmp, q_pos, n_sel))
        kv_col = lambda kind: NSA_WIDTH + kind * NSA_KVW
        hpg = NSA_HEADS // NSA_KV_HEADS
        o_slc = _attn(projp, 0, kv_col(2), kv_col(3), G=NSA_KV_HEADS, R=hpg, tq=128, tk=512, sel=sel)
        o_win = _attn(projp, 0, kv_col(4), kv_col(5), G=NSA_KV_HEADS, R=hpg, tq=128, tk=512, window=WINDOW)
        o_slc = o_slc.reshape(B, T, NSA_HEADS, HEAD_DIM)
        o_win = o_win.reshape(B, T, NSA_HEADS, HEAD_DIM)
        win_state = kv_win[:, -min(WINDOW, T):]
    else:
        assert t0 % SEL_BLK == 0 and T <= SEL_BLK
        n_sel = t0 // SEL_BLK + 1
        hpg = NSA_HEADS // NSA_KV_HEADS
        sel = _selection_mask(_selection_scores(p_cmp, q_pos, n_sel))
        sel_rows = jnp.repeat(sel.transpose(0, 2, 1, 3), hpg, axis=2).reshape(B, T * NSA_HEADS, n_sel)
        row_t = jnp.repeat(jnp.arange(T), NSA_HEADS)
        new_j = jnp.arange(LANE)
        bias_pages = jnp.where(jnp.repeat(sel_rows[:, :, :n_sel - 1], SEL_BLK, axis=2) > 0.5, 0.0, NEG_INF)
        causal_new = (new_j[None, :] <= row_t[:, None]) & (new_j[None, :] < T)
        bias_new = jnp.where((sel_rows[:, :, n_sel - 1:] > 0.5) & causal_new[None], 0.0, NEG_INF)
        pad_rows = lambda a, n: jnp.pad(a.reshape(B, a.shape[1], NSA_KVW), ((0, 0), (0, n - a.shape[1]), (0, 0)))
        qblk = _block_diag_queries(q, NSA_KV_HEADS)
        o_slc = _decode_attn(qblk, pad_rows(new_nsa[:, :, 2], LANE), pad_rows(new_nsa[:, :, 3], LANE), bias_new,
                             pool=pool, page_table=page_table,
                             k_col=2 * NSA_KVW, v_col=3 * NSA_KVW, bias_pages=bias_pages)
        o_slc = _own_head_columns(o_slc, T, NSA_KV_HEADS).reshape(B, T, NSA_HEADS, HEAD_DIM)
        wb = win_buf.shape[1]
        n_win = _round_up(wb + T, LANE)
        k_pos = t0 - wb + jnp.arange(n_win)
        in_win = _window_mask(q_pos, k_pos) & (jnp.arange(n_win) < wb + T)[None, :]
        bias_win = jnp.where(in_win, 0.0, NEG_INF)[row_t][None]
        o_win = _decode_attn(qblk, pad_rows(kv_win[:, :, 0], n_win), pad_rows(kv_win[:, :, 1], n_win), bias_win)
        o_win = _own_head_columns(o_win, T, NSA_KV_HEADS).reshape(B, T, NSA_HEADS, HEAD_DIM)
        win_state = kv_win[:, -wb:]
    g = jax.nn.sigmoid(gate_logit).reshape(B, T, NSA_HEADS, 3, 1)
    o_nsa = (g[:, :, :, 0] * o_cmp + g[:, :, :, 1] * o_slc + g[:, :, :, 2] * o_win).reshape(B, T, NSA_WIDTH)
    za = _mm(ga.reshape(B * T, GLA_LOWRANK), gla_wa, gla_ba).reshape(B, T, GLA_QK)
    if T % GLA_CHUNK == 0:
        o_gla, s_new = _gla(gq, gk, gv, za, gr, gla_g, s0, GLA_CHUNK, 4)
    else:
        Tp = _round_up(T, 2 * SUBLANE)
        pad = lambda a, val=0.0: jnp.pad(a, ((0, 0), (0, Tp - T), (0, 0)), constant_values=val)
        o_gla, s_new = _gla(pad(gq), pad(gk), pad(gv), pad(za, 1e4), pad(gr), gla_g, s0, Tp, 1)
        o_gla = o_gla[:, :T]
    return _out_proj(o_nsa, o_gla, x, norm[3], w_out), (new_nsa, win_state, s_new)


def _mixer_cd(x, norm, w_in, w_out, b_f, a_re, a_im, b_re, b_im, c_re, c_im, d, log_dt, glu_w, glu_b, past):
    B, T, _ = x.shape
    projp = _norm_proj(x, norm[0], norm[1], norm[2], w_in)
    q, k, v, f, u = _split(projp[..., :sum(CD_SPLITS)], CD_SPLITS)
    log_f = jax.nn.log_sigmoid(f + b_f)
    new_kv = jnp.stack([k.reshape(B, T, FOX_HEADS, HEAD_DIM), v.reshape(B, T, FOX_HEADS, HEAD_DIM)], axis=2)
    h0 = None
    if past is None:
        F = jnp.cumsum(log_f, axis=1)
        o_fox = _attn(projp, 0, FOX_WIDTH, 2 * FOX_WIDTH, G=FOX_HEADS, R=1, tq=256, tk=512,
                      fq=F, fk=F.transpose(0, 2, 1))
    else:
        pool_kv, pool_lf, page_table, h0 = past
        t0 = page_table.shape[1] * PAGE_SIZE
        lf_all = jnp.concatenate([_gather_pages(pool_lf, page_table), log_f], axis=1)
        F = jnp.cumsum(lf_all, axis=1)
        Fq = F[:, t0:]
        fq_rows = Fq.reshape(B, T * FOX_HEADS, 1)
        fk_rows = jnp.tile(F.transpose(0, 2, 1), (1, T, 1))
        row_t = jnp.repeat(jnp.arange(T), FOX_HEADS)
        new_j = jnp.arange(LANE)
        bias_pages = fq_rows - fk_rows[:, :, :t0]
        fk_new = jnp.pad(fk_rows[:, :, t0:], ((0, 0), (0, 0), (0, LANE - T)))
        causal_new = (new_j[None, :] <= row_t[:, None]) & (new_j[None, :] < T)
        bias_new = jnp.where(causal_new[None], fq_rows - fk_new, NEG_INF)
        pad_rows = lambda a: jnp.pad(a, ((0, 0), (0, LANE - T), (0, 0)))
        o = _decode_attn(_block_diag_queries(q, FOX_HEADS), pad_rows(k), pad_rows(v), bias_new,
                         pool=pool_kv.astype(BF16).reshape(pool_kv.shape[0], PAGE_SIZE, 2 * FOX_WIDTH),
                         page_table=page_table,
                         k_col=0, v_col=FOX_WIDTH, bias_pages=bias_pages)
        o_fox = _own_head_columns(o, T, FOX_HEADS)
    y_s5, s5_state = _s5(u, a_re, a_im, b_re, b_im, c_re, c_im, d, log_dt, glu_w, glu_b, h0, min(T, 128))
    return _out_proj(o_fox, y_s5, x, norm[3], w_out), (new_kv, log_f, s5_state)


def kernel(x_prompt, x_sample, cache_ab_nsa_kv, cache_ab_win_kv, state_ab_gla, cache_cd_fox_kv, cache_cd_fox_logf, state_cd_s5, page_table, c_prompt, c_sample, ada_w, ada_b, norm_g, ab_w_in, ab_w_out, nsa_cmp_pe, nsa_cmp_w1, nsa_cmp_w2, gla_w_alpha, gla_b_alpha, gla_norm_g, cd_w_in, cd_w_out, fox_b_f, s5_a_re, s5_a_im, s5_b_re, s5_b_im, s5_c_re, s5_c_im, s5_d, s5_log_dt, s5_glu_w, s5_glu_b, router_w, router_b, moe_w1, moe_b1, moe_w2, moe_b2, final_norm_g):
    xs = [x_prompt, x_sample]
    cs = [c_prompt, c_sample]
    new = [{}, {}]
    n_prompt = x_prompt.shape[0] * x_prompt.shape[1]
    for l in range(DEPTH):
        j = l // 2
        mods = [jnp.split(_mm(jax.nn.silu(c), ada_w[l], ada_b[l]), 6, axis=-1) for c in cs]
        for grp in range(2):
            x = xs[grp]
            sh1, sc1, g1 = mods[grp][:3]
            norm = (norm_g[l, 0], sh1, sc1, g1)
            if l % 2 == 0:
                past = None
                if grp == 1:
                    past = (cache_ab_nsa_kv[j], page_table, cache_ab_win_kv[j], state_ab_gla[j])
                xs[grp], st = _mixer_ab(x, norm, ab_w_in[j], ab_w_out[j], nsa_cmp_pe[j], nsa_cmp_w1[j],
                                        nsa_cmp_w2[j], gla_w_alpha[j], gla_b_alpha[j], gla_norm_g[j], past)
                names = ('nsa_kv', 'win_kv', 'gla')
            else:
                past = None
                if grp == 1:
                    past = (cache_cd_fox_kv[j], cache_cd_fox_logf[j], page_table, state_cd_s5[j])
                xs[grp], st = _mixer_cd(x, norm, cd_w_in[j], cd_w_out[j], fox_b_f[j], s5_a_re[j], s5_a_im[j],
                                        s5_b_re[j], s5_b_im[j], s5_c_re[j], s5_c_im[j], s5_d[j], s5_log_dt[j],
                                        s5_glu_w[j], s5_glu_b[j], past)
                names = ('fox_kv', 'fox_logf', 's5')
            for name, s in zip(names, st):
                new[grp].setdefault(name, []).append(s)
        hn2 = [_modulated_norm(xs[grp], norm_g[l, 1], mods[grp][3], mods[grp][4]).reshape(-1, D_MODEL)
               for grp in range(2)]
        ym = _moe(jnp.concatenate(hn2, axis=0), router_w[l], router_b[l], moe_w1[l], moe_b1[l], moe_w2[l], moe_b2[l])
        yms = [ym[:n_prompt], ym[n_prompt:]]
        for grp in range(2):
            xs[grp] = xs[grp] + mods[grp][5][:, None, :] * yms[grp].reshape(xs[grp].shape)
    ys = [_rmsnorm(x, final_norm_g) for x in xs]
    names = ('nsa_kv', 'win_kv', 'gla', 'fox_kv', 'fox_logf', 's5')
    outs = [jnp.stack(new[grp][name]) for grp in range(2) for name in names]
    return (ys[0], ys[1]) + tuple(outs)
```

```python
import functools
import math

import numpy as np
import jax
import jax.numpy as jnp
from jax import lax
from jax.experimental import pallas as pl
from jax.experimental.pallas import tpu as pltpu

D_MODEL = 1024
DEPTH = 2
PAGE_SIZE = 128
HEAD_DIM = 64
NSA_HEADS = 8
NSA_KV_HEADS = 2
CMP_BLK = 32
CMP_STRIDE = 16
SEL_BLK = 64
SEL_TOPN = 16
WINDOW = 512
GLA_HEADS = 4
GLA_DK = 64
GLA_DV = 128
GLA_LOWRANK = 16
GLA_TAU = 16.0
GLA_CHUNK = 64
FOX_HEADS = 8
S5_GROUPS = 32
S5_CH = 16
S5_STATE = 64
N_EXPERTS = 32
TOP_K = 4
D_FF = D_MODEL
SWIGLU_LIMIT = 7.0
SWIGLU_ALPHA = 1.702
EPS = 1e-6
NEG_INF = -1e30
BIG = 1e9
NSA_WIDTH = NSA_HEADS * HEAD_DIM
NSA_KVW = NSA_KV_HEADS * HEAD_DIM
GLA_QK = GLA_HEADS * GLA_DK
GLA_WIDTH = GLA_HEADS * GLA_DV
FOX_WIDTH = FOX_HEADS * HEAD_DIM
S5_WIDTH = S5_GROUPS * S5_CH
AB_SPLITS = (NSA_WIDTH, 6 * NSA_KVW, 3 * NSA_HEADS, GLA_QK, GLA_QK, GLA_WIDTH, GLA_WIDTH, GLA_LOWRANK)
CD_SPLITS = (FOX_WIDTH, FOX_WIDTH, FOX_WIDTH, FOX_HEADS, S5_WIDTH)

LANE = 128
SUBLANE = 8
VMEM_LIMIT = 56 * 1024 * 1024
MOE_TM = 256

F32 = jnp.float32
BF16 = jnp.bfloat16


def _round_up(n, m):
    return -(-n // m) * m


def _pick_tile(n, candidates):
    for c in candidates:
        if n % c == 0:
            return c
    return n


def _mm_kernel(x_ref, w_ref, b_ref, o_ref, wb_ref, *, act, precise):
    if precise:
        acc = jnp.dot(x_ref[...], w_ref[...], preferred_element_type=F32, precision=lax.Precision.HIGHEST)
    else:
        @pl.when(pl.program_id(1) == 0)
        def _():
            wb_ref[...] = w_ref[...].astype(BF16)

        acc = jnp.dot(x_ref[...].astype(BF16), wb_ref[...], preferred_element_type=F32)
    acc = acc + b_ref[...]
    if act == 'gelu':
        acc = jax.nn.gelu(acc)
    o_ref[...] = acc.astype(o_ref.dtype)


def _mm(x, w, b=None, act=None, out_dtype=F32, precise=False, keep_cols=False):
    M, K = x.shape
    N = w.shape[1]
    Np = _round_up(N, 2 * LANE) if N > 2 * LANE else _round_up(N, LANE)
    Mp = _round_up(M, 512) if M > 256 else _round_up(M, SUBLANE)
    tm = min(Mp, 512)
    tn = _pick_tile(Np, (512, 256, 128))
    if Np != N:
        w = jnp.pad(w, ((0, 0), (0, Np - N)))
    if b is None:
        b = jnp.zeros((N,), F32)
    b = jnp.pad(b.astype(F32), (0, Np - N)).reshape(1, Np)
    if Mp != M:
        x = jnp.pad(x, ((0, Mp - M), (0, 0)))
    out = pl.pallas_call(
        functools.partial(_mm_kernel, act=act, precise=precise),
        grid=(Np // tn, Mp // tm),
        in_specs=[pl.BlockSpec((tm, K), lambda j, i: (i, 0)),
                  pl.BlockSpec((K, tn), lambda j, i: (0, j)),
                  pl.BlockSpec((1, tn), lambda j, i: (0, j))],
        out_specs=pl.BlockSpec((tm, tn), lambda j, i: (i, j)),
        out_shape=jax.ShapeDtypeStruct((Mp, Np), out_dtype),
        scratch_shapes=[pltpu.VMEM((K, tn), BF16)],
        compiler_params=pltpu.CompilerParams(
            dimension_semantics=("arbitrary", "arbitrary"), vmem_limit_bytes=VMEM_LIMIT),
        name="mm",
    )(x, w, b)
    return out[:M] if keep_cols else out[:M, :N]


PROJ_TM = 512


def _norm_proj_kernel(x_ref, g_ref, sh_ref, sc_ref, w_ref, o_ref):
    x = x_ref[0]
    y = x * lax.rsqrt(jnp.mean(x * x, axis=-1, keepdims=True) + EPS) * g_ref[...]
    hn = y * (1 + sc_ref[0]) + sh_ref[0]
    o_ref[0] = jnp.dot(hn.astype(BF16), w_ref[...], preferred_element_type=F32)


def _rows_and_mods(x, mods):
    B, T, D = x.shape
    if T % PROJ_TM == 0:
        return x, [m.reshape(B, 1, D) for m in mods]
    assert (B * T) % PROJ_TM == 0 or B * T <= PROJ_TM
    return x.reshape(1, B * T, D), [jnp.repeat(m, T, axis=0).reshape(1, B * T, D) for m in mods]


def _mod_spec(m, tm):
    if m.shape[1] == 1:
        return pl.BlockSpec((1, 1, m.shape[2]), lambda b, i: (b, 0, 0))
    return pl.BlockSpec((1, tm, m.shape[2]), lambda b, i: (b, i, 0))


def _norm_proj(x, g, shift, scale, w):
    B, T, D = x.shape
    N = w.shape[1]
    Np = _round_up(N, 2 * LANE)
    wb = jnp.pad(w, ((0, 0), (0, Np - N))).astype(BF16)
    xr, (sh, sc) = _rows_and_mods(x, [shift, scale])
    Bm, Tm, _ = xr.shape
    tm = min(PROJ_TM, Tm)
    out = pl.pallas_call(
        _norm_proj_kernel,
        grid=(Bm, Tm // tm),
        in_specs=[pl.BlockSpec((1, tm, D), lambda b, i: (b, i, 0)),
                  pl.BlockSpec((1, D), lambda b, i: (0, 0)),
                  _mod_spec(sh, tm), _mod_spec(sc, tm),
                  pl.BlockSpec((D, Np), lambda b, i: (0, 0))],
        out_specs=pl.BlockSpec((1, tm, Np), lambda b, i: (b, i, 0)),
        out_shape=jax.ShapeDtypeStruct((Bm, Tm, Np), F32),
        compiler_params=pltpu.CompilerParams(
            dimension_semantics=("arbitrary", "arbitrary"), vmem_limit_bytes=VMEM_LIMIT),
        name="norm_proj",
    )(xr, g.reshape(1, D), sh, sc, wb)
    return out.reshape(B, T, Np)


def _out_proj_kernel(a_ref, b_ref, x_ref, g_ref, w_ref, o_ref):
    wa = a_ref.shape[2]
    y = jnp.dot(a_ref[0].astype(BF16), w_ref[:wa, :], preferred_element_type=F32)
    y = y + jnp.dot(b_ref[0].astype(BF16), w_ref[wa:, :], preferred_element_type=F32)
    o_ref[0] = x_ref[0] + g_ref[0] * y


def _out_proj(a, b, x, gate, w):
    B, T, D = x.shape
    wa, wb_ = a.shape[2], b.shape[2]
    xr, (g,) = _rows_and_mods(x, [gate])
    Bm, Tm, _ = xr.shape
    tm = min(PROJ_TM, Tm)
    row = lambda b_, i: (b_, i, 0)
    out = pl.pallas_call(
        _out_proj_kernel,
        grid=(Bm, Tm // tm),
        in_specs=[pl.BlockSpec((1, tm, wa), row), pl.BlockSpec((1, tm, wb_), row), pl.BlockSpec((1, tm, D), row),
                  _mod_spec(g, tm), pl.BlockSpec((wa + wb_, D), lambda b_, i: (0, 0))],
        out_specs=pl.BlockSpec((1, tm, D), row),
        out_shape=jax.ShapeDtypeStruct((Bm, Tm, D), F32),
        compiler_params=pltpu.CompilerParams(
            dimension_semantics=("arbitrary", "arbitrary"), vmem_limit_bytes=VMEM_LIMIT),
        name="out_proj",
    )(a.reshape(Bm, Tm, wa), b.reshape(Bm, Tm, wb_), xr, g, w.astype(BF16))
    return out.reshape(B, T, D)


def _flash_kernel(*refs, tq, tk, R, window, use_bias, use_sel):
    q_ref, k_ref, v_ref = refs[:3]
    n = 3
    if use_bias:
        qb_ref, kb_ref = refs[n:n + 2]
        n += 2
    if use_sel:
        sel_ref = refs[n]
        n += 1
    o_ref = refs[n]
    i = pl.program_id(2)
    rows = R * tq
    q = q_ref[0, 0].reshape(rows, HEAD_DIM)
    q_pos = i * tq + jnp.bitwise_and(lax.broadcasted_iota(jnp.int32, (rows, 1), 0), tq - 1)
    hi = ((i + 1) * tq + tk - 1) // tk
    lo = jnp.maximum(i * tq - window + 1, 0) // tk if window else 0
    if use_bias:
        qb = qb_ref[0, 0]
        qb = jnp.concatenate([qb] * (tk // LANE), axis=1)
    if use_sel:
        sel = sel_ref[0, 0].astype(BF16)
        n_sel = sel.shape[1]

    def body(j, carry):
        m, l, acc = carry
        start = pl.multiple_of(j * tk, tk)
        k = k_ref[0, 0, pl.ds(start, tk), :]
        v = v_ref[0, 0, pl.ds(start, tk), :]
        s = lax.dot_general(q, k, (((1,), (1,)), ((), ())), preferred_element_type=F32)
        k_pos = j * tk + lax.broadcasted_iota(jnp.int32, (1, tk), 1)
        if use_bias:
            s = s + qb - kb_ref[0, 0, j]
        s = jnp.where(k_pos <= q_pos, s, NEG_INF)
        if window:
            s = jnp.where(q_pos - k_pos < window, s, NEG_INF)
        if use_sel:
            blk_of_key = (j * tk + lax.broadcasted_iota(jnp.int32, (n_sel, tk), 1)) // SEL_BLK
            expand = jnp.where(blk_of_key == lax.broadcasted_iota(jnp.int32, (n_sel, tk), 0), 1.0, 0.0).astype(BF16)
            chosen = jnp.dot(sel, expand, preferred_element_type=F32)
            chosen = jnp.concatenate([chosen] * R, axis=0)
            s = jnp.where(chosen > 0.5, s, NEG_INF)
        m_new = jnp.maximum(m, jnp.max(s, axis=1, keepdims=True))
        p = jnp.where(s > 0.5 * NEG_INF, jnp.exp(s - m_new), 0.0)
        alpha = jnp.exp(m - m_new)
        l = alpha * l + jnp.sum(p, axis=1, keepdims=True)
        acc = alpha * acc + jnp.dot(p.astype(BF16), v, preferred_element_type=F32)
        return m_new, l, acc

    m0 = jnp.full((rows, 1), NEG_INF, F32)
    l0 = jnp.zeros((rows, 1), F32)
    a0 = jnp.zeros((rows, HEAD_DIM), F32)
    m, l, acc = lax.fori_loop(lo, hi, body, (m0, l0, a0))
    o_ref[0, 0] = (acc / l).reshape(R, tq, HEAD_DIM)


def _flash(q, k, v, *, tq, tk, window=0, qbias=None, kbias=None, sel=None):
    B, G, R, T, _ = q.shape
    grid = (B, G, T // tq)
    in_specs = [pl.BlockSpec((1, 1, R, tq, HEAD_DIM), lambda b, g, i: (b, g, 0, i, 0)),
                pl.BlockSpec((1, 1, T, HEAD_DIM), lambda b, g, i: (b, g, 0, 0)),
                pl.BlockSpec((1, 1, T, HEAD_DIM), lambda b, g, i: (b, g, 0, 0))]
    args = [q, k, v]
    if qbias is not None:
        in_specs += [pl.BlockSpec((1, 1, tq, LANE), lambda b, g, i: (b, g, i, 0)),
                     pl.BlockSpec((1, 1, T // tk, 1, tk), lambda b, g, i: (b, g, 0, 0, 0))]
        args += [qbias, kbias.reshape(B, G, T // tk, 1, tk)]
    if sel is not None:
        in_specs += [pl.BlockSpec((1, 1, tq, sel.shape[-1]), lambda b, g, i: (b, g, i, 0))]
        args += [sel]
    return pl.pallas_call(
        functools.partial(_flash_kernel, tq=tq, tk=tk, R=R, window=window,
                          use_bias=qbias is not None, use_sel=sel is not None),
        grid=grid,
        in_specs=in_specs,
        out_specs=pl.BlockSpec((1, 1, R, tq, HEAD_DIM), lambda b, g, i: (b, g, 0, i, 0)),
        out_shape=jax.ShapeDtypeStruct((B, G, R, T, HEAD_DIM), F32),
        compiler_params=pltpu.CompilerParams(
            dimension_semantics=("arbitrary", "arbitrary", "arbitrary"), vmem_limit_bytes=VMEM_LIMIT),
        name="flash",
    )(*args)


def _attn_kernel(*refs, tq, tk, G, R, window, use_bias, use_sel):
    q_ref, k_ref, v_ref = refs[:3]
    n = 3
    if use_bias:
        fq_ref, fk_ref = refs[n:n + 2]
        n += 2
    if use_sel:
        sel_ref = refs[n]
        n += 1
    o_ref, kb_ref, vb_ref = refs[n:n + 3]
    i = pl.program_id(1)

    @pl.when(i == 0)
    def _():
        kb_ref[...] = k_ref[0].astype(BF16)
        vb_ref[...] = v_ref[0].astype(BF16)

    rows = R * tq
    q_pos = i * tq + jnp.bitwise_and(lax.broadcasted_iota(jnp.int32, (rows, 1), 0), tq - 1)
    hi = ((i + 1) * tq + tk - 1) // tk
    lo = jnp.maximum(i * tq - window + 1, 0) // tk if window else 0
    outs = []
    for g in range(G):
        cols = slice(g * HEAD_DIM, (g + 1) * HEAD_DIM)
        qg = q_ref[0, :, g * R * HEAD_DIM:(g + 1) * R * HEAD_DIM] * HEAD_DIM ** -0.5
        q = jnp.concatenate([qg[:, r * HEAD_DIM:(r + 1) * HEAD_DIM] for r in range(R)], axis=0).astype(BF16)
        if use_bias:
            fq = fq_ref[0, :, g:g + 1]
        if use_sel:
            sel = sel_ref[0, g].astype(BF16)
            n_sel = sel.shape[1]

        def body(j, carry):
            m, l, acc = carry
            start = pl.multiple_of(j * tk, tk)
            k = kb_ref[pl.ds(start, tk), cols]
            v = vb_ref[pl.ds(start, tk), cols]
            s = lax.dot_general(q, k, (((1,), (1,)), ((), ())), preferred_element_type=F32)
            k_pos = j * tk + lax.broadcasted_iota(jnp.int32, (1, tk), 1)
            if use_bias:
                s = s + fq - fk_ref[0, g, j]
            s = jnp.where(k_pos <= q_pos, s, NEG_INF)
            if window:
                s = jnp.where(q_pos - k_pos < window, s, NEG_INF)
            if use_sel:
                blk_of_key = (j * tk + lax.broadcasted_iota(jnp.int32, (n_sel, tk), 1)) // SEL_BLK
                expand = jnp.where(blk_of_key == lax.broadcasted_iota(jnp.int32, (n_sel, tk), 0), 1.0, 0.0)
                chosen = jnp.dot(sel, expand.astype(BF16), preferred_element_type=F32)
                chosen = jnp.concatenate([chosen] * R, axis=0)
                s = jnp.where(chosen > 0.5, s, NEG_INF)
            m_new = jnp.maximum(m, jnp.max(s, axis=1, keepdims=True))
            p = jnp.where(s > 0.5 * NEG_INF, jnp.exp(s - m_new), 0.0)
            alpha = jnp.exp(m - m_new)
            l = alpha * l + jnp.sum(p, axis=1, keepdims=True)
            acc = alpha * acc + jnp.dot(p.astype(BF16), v, preferred_element_type=F32)
            return m_new, l, acc

        m0 = jnp.full((rows, 1), NEG_INF, F32)
        l0 = jnp.zeros((rows, 1), F32)
        a0 = jnp.zeros((rows, HEAD_DIM), F32)
        m, l, acc = lax.fori_loop(lo, hi, body, (m0, l0, a0))
        o = acc / l
        outs += [o[r * tq:(r + 1) * tq] for r in range(R)]
    o_ref[0] = jnp.concatenate(outs, axis=1)


def _attn(proj, q_col, k_col, v_col, *, G, R, tq, tk, window=0, fq=None, fk=None, sel=None):
    B, T, _ = proj.shape
    qw, kw = G * R * HEAD_DIM, G * HEAD_DIM
    assert q_col % qw == 0 and k_col % kw == 0 and v_col % kw == 0
    in_specs = [pl.BlockSpec((1, tq, qw), lambda b, i: (b, i, q_col // qw)),
                pl.BlockSpec((1, T, kw), lambda b, i: (b, 0, k_col // kw)),
                pl.BlockSpec((1, T, kw), lambda b, i: (b, 0, v_col // kw))]
    args = [proj, proj, proj]
    if fq is not None:
        in_specs += [pl.BlockSpec((1, tq, G), lambda b, i: (b, i, 0)),
                     pl.BlockSpec((1, G, T // tk, 1, tk), lambda b, i: (b, 0, 0, 0, 0))]
        args += [fq, fk.reshape(B, G, T // tk, 1, tk)]
    if sel is not None:
        in_specs += [pl.BlockSpec((1, G, tq, sel.shape[-1]), lambda b, i: (b, 0, i, 0))]
        args += [sel]
    return pl.pallas_call(
        functools.partial(_attn_kernel, tq=tq, tk=tk, G=G, R=R, window=window,
                          use_bias=fq is not None, use_sel=sel is not None),
        grid=(B, T // tq),
        in_specs=in_specs,
        out_specs=pl.BlockSpec((1, tq, qw), lambda b, i: (b, i, 0)),
        out_shape=jax.ShapeDtypeStruct((B, T, qw), F32),
        scratch_shapes=[pltpu.VMEM((T, kw), BF16), pltpu.VMEM((T, kw), BF16)],
        compiler_params=pltpu.CompilerParams(
            dimension_semantics=("arbitrary", "arbitrary"), vmem_limit_bytes=VMEM_LIMIT),
        name="attn",
    )(*args)


def _decode_attn_kernel(*refs, n_pages):
    pt_ref, q_ref = refs[0], refs[1]
    del pt_ref
    k_refs = refs[2:2 + n_pages]
    v_refs = refs[2 + n_pages:2 + 2 * n_pages]
    n = 2 + 2 * n_pages
    if n_pages:
        bp_ref = refs[n]
        n += 1
    ke_ref, ve_ref, be_ref, o_ref = refs[n:n + 4]
    nt = (((1,), (1,)), ((), ()))
    q = q_ref[0].astype(BF16)
    kw = q.shape[1]
    se = lax.dot_general(q, ke_ref[0].astype(BF16), nt, preferred_element_type=F32) + be_ref[0]
    m = jnp.max(se, axis=1, keepdims=True)
    if n_pages:
        s = jnp.concatenate([jnp.dot(q, k_refs[p][0, 0].reshape(kw, PAGE_SIZE).astype(BF16),
                                     preferred_element_type=F32) for p in range(n_pages)], axis=1) + bp_ref[0]
        m = jnp.maximum(m, jnp.max(s, axis=1, keepdims=True))
    pe = jnp.exp(se - m)
    l = jnp.sum(pe, axis=1, keepdims=True)
    o = jnp.dot(pe.astype(BF16), ve_ref[0].astype(BF16), preferred_element_type=F32)
    if n_pages:
        p = jnp.exp(s - m)
        l = l + jnp.sum(p, axis=1, keepdims=True)
        pb = p.astype(BF16)
        for pg in range(n_pages):
            o = o + lax.dot_general(pb[:, pg * PAGE_SIZE:(pg + 1) * PAGE_SIZE],
                                    v_refs[pg][0, 0].reshape(kw, PAGE_SIZE).astype(BF16), nt,
                                    preferred_element_type=F32)
    o_ref[0] = o / l


def _decode_attn(qblk, k_extra, v_extra, bias_extra, pool_t=None, page_table=None, k_kind=0, v_kind=0,
                 bias_pages=None):
    B, R, KW = qblk.shape
    NE = k_extra.shape[1]
    n_pages = 0 if pool_t is None else page_table.shape[1]
    row3 = lambda b, pt: (b, 0, 0)
    in_specs = [pl.BlockSpec((1, R, KW), row3)]
    args = [qblk]
    if n_pages:
        G = pool_t.shape[2]
        assert G * HEAD_DIM == KW
        for kind in (k_kind, v_kind):
            for p in range(n_pages):
                in_specs.append(pl.BlockSpec((1, 1, G, HEAD_DIM, PAGE_SIZE),
                                             functools.partial(lambda b, pt, p, c: (pt[b, p], c, 0, 0, 0), p=p, c=kind)))
                args.append(pool_t)
        in_specs.append(pl.BlockSpec((1, R, n_pages * PAGE_SIZE), row3))
        args.append(bias_pages)
    else:
        page_table = jnp.zeros((1, 1), jnp.int32)
    be_map = row3 if bias_extra.shape[0] == B else (lambda b, pt: (0, 0, 0))
    in_specs += [pl.BlockSpec((1, NE, KW), row3), pl.BlockSpec((1, NE, KW), row3), pl.BlockSpec((1, R, NE), be_map)]
    args += [k_extra, v_extra, bias_extra]
    grid_spec = pltpu.PrefetchScalarGridSpec(
        num_scalar_prefetch=1, grid=(B,), in_specs=in_specs,
        out_specs=pl.BlockSpec((1, R, KW), row3))
    return pl.pallas_call(
        functools.partial(_decode_attn_kernel, n_pages=n_pages),
        grid_spec=grid_spec,
        out_shape=jax.ShapeDtypeStruct((B, R, KW), F32),
        compiler_params=pltpu.CompilerParams(dimension_semantics=("arbitrary",), vmem_limit_bytes=VMEM_LIMIT),
        name="decode_attn",
    )(page_table, *args)


def _block_diag_queries(q, G):
    B, T, W = q.shape
    H = W // HEAD_DIM
    own = (jnp.arange(H)[:, None] // (H // G) == jnp.arange(G)[None, :]).astype(F32)
    qs = (q * HEAD_DIM ** -0.5).reshape(B, T, H, 1, HEAD_DIM) * own[None, None, :, :, None]
    return qs.reshape(B, T * H, G * HEAD_DIM)


def _own_head_columns(o, T, G):
    B, R, _ = o.shape
    H = R // T
    own = (jnp.arange(H)[:, None] // (H // G) == jnp.arange(G)[None, :]).astype(F32)
    o5 = o.reshape(B, T, H, G, HEAD_DIM) * own[None, None, :, :, None]
    return jnp.sum(o5, axis=3).reshape(B, T, H * HEAD_DIM)


def _moe_kernel(be_ref, nb_ref, x_ref, w1_ref, b1_ref, w2_ref, b2_ref, o_ref, w1b_ref, w2b_ref):
    i = pl.program_id(0)
    prev = be_ref[jnp.maximum(i - 1, 0)]

    @pl.when(jnp.logical_or(i == 0, be_ref[i] != prev))
    def _():
        w1b_ref[...] = w1_ref[0].astype(BF16)
        w2b_ref[...] = w2_ref[0].astype(BF16)

    @pl.when(i < nb_ref[0])
    def _():
        gu = jnp.dot(x_ref[...].astype(BF16), w1b_ref[...], preferred_element_type=F32) + b1_ref[0]
        g = jnp.minimum(gu[:, :D_FF], SWIGLU_LIMIT)
        up = jnp.clip(gu[:, D_FF:], -SWIGLU_LIMIT, SWIGLU_LIMIT)
        h = (up + 1) * g * jax.nn.sigmoid(SWIGLU_ALPHA * g)
        o_ref[...] = jnp.dot(h.astype(BF16), w2b_ref[...], preferred_element_type=F32) + b2_ref[0]

    @pl.when(i >= nb_ref[0])
    def _():
        o_ref[...] = jnp.zeros_like(o_ref)


def _moe_experts(xs, blk_e, n_used, w1, b1, w2, b2):
    n_blocks = xs.shape[0] // MOE_TM
    grid_spec = pltpu.PrefetchScalarGridSpec(
        num_scalar_prefetch=2,
        grid=(n_blocks,),
        in_specs=[pl.BlockSpec((MOE_TM, D_MODEL), lambda i, be, nb: (i, 0)),
                  pl.BlockSpec((1, D_MODEL, 2 * D_FF), lambda i, be, nb: (be[i], 0, 0)),
                  pl.BlockSpec((1, 1, 2 * D_FF), lambda i, be, nb: (be[i], 0, 0)),
                  pl.BlockSpec((1, D_FF, D_MODEL), lambda i, be, nb: (be[i], 0, 0)),
                  pl.BlockSpec((1, 1, D_MODEL), lambda i, be, nb: (be[i], 0, 0))],
        out_specs=pl.BlockSpec((MOE_TM, D_MODEL), lambda i, be, nb: (i, 0)),
        scratch_shapes=[pltpu.VMEM((D_MODEL, 2 * D_FF), BF16), pltpu.VMEM((D_FF, D_MODEL), BF16)],
    )
    return pl.pallas_call(
        _moe_kernel,
        grid_spec=grid_spec,
        out_shape=jax.ShapeDtypeStruct((n_blocks * MOE_TM, D_MODEL), F32),
        compiler_params=pltpu.CompilerParams(
            dimension_semantics=("arbitrary",), vmem_limit_bytes=VMEM_LIMIT),
        name="moe_experts",
    )(blk_e, n_used, xs, w1, b1.reshape(N_EXPERTS, 1, 2 * D_FF), w2, b2.reshape(N_EXPERTS, 1, D_MODEL))


def _moe_combine_kernel(y_ref, g_ref, o_ref):
    g = g_ref[...]
    acc = g[:, 0:1] * y_ref[:, :D_MODEL]
    for k in range(1, TOP_K):
        acc = acc + g[:, k:k + 1] * y_ref[:, k * D_MODEL:(k + 1) * D_MODEL]
    o_ref[...] = acc


def _moe_combine(yg, gate):
    N = yg.shape[0]
    tn = _pick_tile(N, (512, 256, 128, 64, 32, 16, 8))
    return pl.pallas_call(
        _moe_combine_kernel,
        grid=(N // tn,),
        in_specs=[pl.BlockSpec((tn, TOP_K * D_MODEL), lambda i: (i, 0)),
                  pl.BlockSpec((tn, TOP_K), lambda i: (i, 0))],
        out_specs=pl.BlockSpec((tn, D_MODEL), lambda i: (i, 0)),
        out_shape=jax.ShapeDtypeStruct((N, D_MODEL), F32),
        compiler_params=pltpu.CompilerParams(dimension_semantics=("arbitrary",), vmem_limit_bytes=VMEM_LIMIT),
        name="moe_combine",
    )(yg, gate)


def _moe(xf, w_r, b_r, w1, b1, w2, b2):
    N = xf.shape[0]
    n_rows = N * TOP_K
    logits = _mm(xf, w_r, b_r, precise=True)
    top_v, top_i = lax.top_k(logits, TOP_K)
    gate = jax.nn.softmax(top_v, axis=-1)
    flat_e = top_i.reshape(-1)
    order = jnp.argsort(flat_e)
    sorted_e = flat_e[order]
    counts = jnp.bincount(flat_e, length=N_EXPERTS)
    padded = (counts + MOE_TM - 1) // MOE_TM * MOE_TM
    pad_end = jnp.cumsum(padded)
    pad_start = pad_end - padded
    start = jnp.cumsum(counts) - counts
    dest = (pad_start[sorted_e] + jnp.arange(n_rows) - start[sorted_e]).astype(jnp.int32)
    n_blocks = -(-(n_rows + N_EXPERTS * (MOE_TM - 1)) // MOE_TM)
    n_used = (pad_end[-1] // MOE_TM).astype(jnp.int32)
    blk = jnp.minimum(jnp.arange(n_blocks), n_used - 1) * MOE_TM
    blk_e = jnp.minimum(jnp.sum(pad_end[None, :] <= blk[:, None], axis=1), N_EXPERTS - 1).astype(jnp.int32)
    row_e = jnp.repeat(blk_e, MOE_TM)
    off = jnp.arange(n_blocks * MOE_TM) - pad_start[row_e]
    src = jnp.clip(start[row_e] + off, 0, n_rows - 1)
    row_tok = jnp.where(off < counts[row_e], (order // TOP_K)[src], 0).astype(jnp.int32)
    xs = xf[row_tok]
    ys = _moe_experts(xs, blk_e, n_used.reshape(1), w1, b1, w2, b2)
    pos = dest[jnp.argsort(order)]
    return _moe_combine(ys[pos].reshape(N, TOP_K * D_MODEL), gate)


def _rmsnorm(x, g):
    y = x * lax.rsqrt(jnp.mean(x * x, axis=-1, keepdims=True) + EPS)
    return y * g


def _modulated_norm(x, g, shift, scale):
    return _rmsnorm(x, g) * (1 + scale[:, None, :]) + shift[:, None, :]


def _split(x, sizes):
    return jnp.split(x, np.cumsum(sizes)[:-1].tolist(), axis=-1)


def _gather_pages(pool, page_table):
    g = pool[page_table]
    return g.reshape((g.shape[0], g.shape[1] * g.shape[2]) + g.shape[3:])


def _attend(q, k, v, mask, bias=None):
    B, Q, H, D = q.shape
    G = k.shape[2]
    s = jnp.einsum('bqgnd,bkgd->bgnqk', q.reshape(B, Q, G, H // G, D), k).astype(F32) * D ** -0.5
    if bias is not None:
        s = s + bias
    p = jax.nn.softmax(jnp.where(mask, s, NEG_INF), axis=-1).astype(v.dtype)
    return jnp.einsum('bgnqk,bkgd->bqgnd', p, v).reshape(B, Q, H, D)


def _window_mask(q_pos, k_pos):
    d = q_pos[:, None] - k_pos[None, :]
    return (d >= 0) & (d < WINDOW) & (k_pos[None, :] >= 0)


def _to_heads(x, G):
    B, T, _ = x.shape
    return x.astype(BF16).reshape(B, T, G, HEAD_DIM).transpose(0, 2, 1, 3)


def _q_to_heads(q, G):
    B, T, W = q.shape
    R = W // HEAD_DIM // G
    return (q * HEAD_DIM ** -0.5).astype(BF16).reshape(B, T, G, R, HEAD_DIM).transpose(0, 2, 3, 1, 4)


def _from_heads(o):
    B, G, R, T, D = o.shape
    return o.transpose(0, 3, 1, 2, 4).reshape(B, T, G * R * D)


CMP_CHUNK_W = CMP_STRIDE * HEAD_DIM
CMP_HID = 256


def _cmp_chunk_proj(tokens, w1):
    assert CMP_BLK == 2 * CMP_STRIDE
    S, T, G, D = tokens.shape
    n = T // CMP_STRIDE
    ch = tokens.reshape(S, n, CMP_STRIDE, G, D).transpose(0, 1, 3, 2, 4).reshape(S * n * G, CMP_CHUNK_W)
    wcat = jnp.concatenate([w1[:CMP_CHUNK_W], w1[CMP_CHUNK_W:]], axis=1)
    return _mm(ch, wcat).reshape(S, n, G, 2 * CMP_HID)


def _cmp_out_kernel(a_ref, b_ref, bias_ref, w_ref, o_ref):
    hid = jax.nn.gelu(a_ref[...] + b_ref[...] + bias_ref[...])
    o_ref[...] = jnp.dot(hid.astype(BF16), w_ref[...].astype(BF16), preferred_element_type=F32)


def _cmp_finish(proj, pe, w1, w2):
    B, n, G, _ = proj.shape
    first = proj[:, :-1, :, :CMP_HID].reshape(-1, CMP_HID)
    second = proj[:, 1:, :, CMP_HID:].reshape(-1, CMP_HID)
    bias = _mm(pe.reshape(1, CMP_BLK * HEAD_DIM), w1)
    rows = first.shape[0]
    rows_p = _round_up(rows, 512)
    pad = lambda a: jnp.pad(a, ((0, rows_p - rows), (0, 0)))
    w2p = jnp.pad(w2, ((0, 0), (0, LANE - HEAD_DIM)))
    out = pl.pallas_call(
        _cmp_out_kernel,
        grid=(rows_p // 512,),
        in_specs=[pl.BlockSpec((512, CMP_HID), lambda i: (i, 0)), pl.BlockSpec((512, CMP_HID), lambda i: (i, 0)),
                  pl.BlockSpec((1, CMP_HID), lambda i: (0, 0)), pl.BlockSpec((CMP_HID, LANE), lambda i: (0, 0))],
        out_specs=pl.BlockSpec((512, LANE), lambda i: (i, 0)),
        out_shape=jax.ShapeDtypeStruct((rows_p, LANE), F32),
        compiler_params=pltpu.CompilerParams(dimension_semantics=("arbitrary",), vmem_limit_bytes=VMEM_LIMIT),
        name="cmp_out",
    )(pad(first), pad(second), bias, w2p)
    return out[:rows, :HEAD_DIM].reshape(B, n - 1, G, HEAD_DIM)


def _cmp_attention(q, kc, vc, q_pos):
    B, Q, H, D = q.shape
    N, G = kc.shape[1], kc.shape[2]
    c_end = jnp.arange(N) * CMP_STRIDE + CMP_BLK
    valid = c_end[None, :] <= q_pos[:, None] + 1
    s = jnp.einsum('bqgnd,bcgd->bgnqc', q.reshape(B, Q, G, H // G, D), kc).astype(F32) * D ** -0.5
    p = jax.nn.softmax(jnp.where(valid, s, NEG_INF), axis=-1) * valid
    o = jnp.einsum('bgnqc,bcgd->bqgnd', p.astype(vc.dtype), vc).reshape(B, Q, H, D)
    return o, p


def _cmp_to_sel(n_cmp, n_sel):
    c0 = jnp.arange(n_cmp)[:, None] * CMP_STRIDE
    s0 = jnp.arange(n_sel)[None, :] * SEL_BLK
    ov = jnp.minimum(c0 + CMP_BLK, s0 + SEL_BLK) - jnp.maximum(c0, s0)
    return jnp.maximum(ov, 0).astype(F32) / CMP_BLK


def _selection_scores(p_cmp, q_pos, n_sel):
    imp = jnp.einsum('bgnqc,cs->bgqs', p_cmp, _cmp_to_sel(p_cmp.shape[-1], n_sel))
    blk = jnp.arange(n_sel)[None, :]
    cur = (q_pos // SEL_BLK)[:, None]
    valid = blk * SEL_BLK <= q_pos[:, None]
    forced = (blk == 0) | (blk == cur) | (blk == cur - 1)
    return jnp.where(valid, jnp.where(forced, BIG, imp), -BIG)


def _selection_mask(score):
    n_sel = score.shape[-1]
    a = score[..., :, None]
    b = score[..., None, :]
    lower = jnp.arange(n_sel)[None, :] < jnp.arange(n_sel)[:, None]
    beats = (b > a) | ((b == a) & lower)
    rank = jnp.sum(beats, axis=-1)
    return (rank < min(SEL_TOPN, n_sel)).astype(F32)


def _selected_attention_gather(q, k, v, p_cmp, q_pos):
    B, Q, H, D = q.shape
    T_all, G = k.shape[1], k.shape[2]
    n_sel = -(-T_all // SEL_BLK)
    score = _selection_scores(p_cmp, q_pos, n_sel)
    _, idx = lax.top_k(score, min(SEL_TOPN, n_sel))
    tok = (idx[..., None] * SEL_BLK + jnp.arange(SEL_BLK)).reshape(B, G, Q, -1)
    pad = ((0, 0), (0, n_sel * SEL_BLK - T_all), (0, 0), (0, 0))
    kt = jnp.pad(k, pad).transpose(0, 2, 1, 3)
    vt = jnp.pad(v, pad).transpose(0, 2, 1, 3)
    b_ix = jnp.arange(B)[:, None, None]
    g_ix = jnp.arange(G)[None, :, None]
    L = tok.shape[3]
    flat = tok.reshape(B, G, Q * L)
    ks = kt[b_ix, g_ix, flat].reshape(B, G, Q, L, D)
    vs = vt[b_ix, g_ix, flat].reshape(B, G, Q, L, D)
    s = jnp.einsum('bqgnd,bgqld->bgnql', q.reshape(B, Q, G, H // G, D), ks).astype(F32) * D ** -0.5
    mask = (tok <= q_pos[None, None, :, None])[:, :, None]
    p = jax.nn.softmax(jnp.where(mask, s, NEG_INF), axis=-1)
    return jnp.einsum('bgnql,bgqld->bqgnd', p, vs).reshape(B, Q, H, D)


def _gla_kernel(q_ref, k_ref, v_ref, gr_ref, tail_ref, wa_ref, ba_ref, g_ref, s0_ref, o_ref, sT_ref, st_ref,
                *, C, n_chunks, n_valid):
    @pl.when(pl.program_id(1) == 0)
    def _():
        st_ref[...] = s0_ref[0]

    tri = (lax.broadcasted_iota(jnp.int32, (C, C), 0) >= lax.broadcasted_iota(jnp.int32, (C, C), 1))
    tri_f = jnp.where(tri, 1.0, 0.0)
    for c in range(n_chunks):
        rows = pl.ds(c * C, C)
        za = jnp.dot(tail_ref[0, rows, :].astype(BF16), wa_ref[...], preferred_element_type=F32) + ba_ref[...]
        log_a = jax.nn.log_sigmoid(za) / GLA_TAU
        q = q_ref[0, rows, :] * GLA_DK ** -0.5
        k = k_ref[0, rows, :]
        v = v_ref[0, rows, :]
        if n_valid < C:
            live = lax.broadcasted_iota(jnp.int32, (C, 1), 0) < n_valid
            log_a = jnp.where(live, log_a, 0.0)
            q, k, v = jnp.where(live, q, 0.0), jnp.where(live, k, 0.0), jnp.where(live, v, 0.0)
        b = jnp.dot(tri_f, log_a, preferred_element_type=F32, precision=lax.Precision.HIGHEST)
        b_last = b[C - 1:C, :]
        q_dec = (q * jnp.exp(b)).astype(BF16)
        k_inv = (k * jnp.exp(-b)).astype(BF16)
        k_end = (k * jnp.exp(b_last - b)).astype(BF16)
        decay = jnp.exp(b_last)
        v = v.astype(BF16)
        gr = gr_ref[0, rows, :]
        outs = []
        for h in range(GLA_HEADS):
            kh = slice(h * GLA_DK, (h + 1) * GLA_DK)
            vh = slice(h * GLA_DV, (h + 1) * GLA_DV)
            attn = lax.dot_general(q_dec[:, kh], k_inv[:, kh], (((1,), (1,)), ((), ())), preferred_element_type=F32)
            attn = jnp.where(tri, attn, 0.0).astype(BF16)
            sT = st_ref[h]
            o = jnp.dot(attn, v[:, vh], preferred_element_type=F32)
            o = o + lax.dot_general(q_dec[:, kh], sT.astype(BF16), (((1,), (1,)), ((), ())), preferred_element_type=F32)
            st_ref[h] = decay[:, kh] * sT + lax.dot_general(v[:, vh], k_end[:, kh], (((0,), (0,)), ((), ())),
                                                            preferred_element_type=F32)
            o = o * lax.rsqrt(jnp.mean(o * o, axis=-1, keepdims=True) + EPS) * g_ref[...]
            outs.append(o * jax.nn.silu(gr[:, vh]))
        o_ref[0, rows, :] = jnp.concatenate(outs, axis=1)

    @pl.when(pl.program_id(1) == pl.num_programs(1) - 1)
    def _():
        sT_ref[0] = st_ref[...]


def _gla(proj, cols, gla_wa, gla_ba, gla_g, s0, C, n_chunks, n_valid):
    B, T, _ = proj.shape
    tt = C * n_chunks
    s0T = s0.transpose(0, 1, 3, 2)
    cq, ck, cv, cr, ca = cols
    tail0 = ca // GLA_QK * GLA_QK
    assert cq % GLA_QK == 0 and ck % GLA_QK == 0 and cv % GLA_WIDTH == 0 and cr % GLA_WIDTH == 0
    assert ca + GLA_LOWRANK <= tail0 + GLA_QK
    wa_pad = jnp.zeros((GLA_QK, GLA_QK), F32).at[ca - tail0:ca - tail0 + GLA_LOWRANK].set(gla_wa).astype(BF16)
    col_spec = lambda width, col: pl.BlockSpec((1, tt, width), lambda b, i: (b, i, col // width))
    v_spec = pl.BlockSpec((1, tt, GLA_WIDTH), lambda b, i: (b, i, 0))
    st_spec = pl.BlockSpec((1, GLA_HEADS, GLA_DV, GLA_DK), lambda b, i: (b, 0, 0, 0))
    const2 = lambda b, i: (0, 0)
    o, sT = pl.pallas_call(
        functools.partial(_gla_kernel, C=C, n_chunks=n_chunks, n_valid=n_valid),
        grid=(B, T // tt),
        in_specs=[col_spec(GLA_QK, cq), col_spec(GLA_QK, ck), col_spec(GLA_WIDTH, cv), col_spec(GLA_WIDTH, cr),
                  col_spec(GLA_QK, tail0), pl.BlockSpec((GLA_QK, GLA_QK), const2), pl.BlockSpec((1, GLA_QK), const2),
                  pl.BlockSpec((1, GLA_DV), const2), st_spec],
        out_specs=[v_spec, st_spec],
        out_shape=[jax.ShapeDtypeStruct((B, T, GLA_WIDTH), F32),
                   jax.ShapeDtypeStruct((B, GLA_HEADS, GLA_DV, GLA_DK), F32)],
        scratch_shapes=[pltpu.VMEM((GLA_HEADS, GLA_DV, GLA_DK), F32)],
        compiler_params=pltpu.CompilerParams(
            dimension_semantics=("arbitrary", "arbitrary"), vmem_limit_bytes=VMEM_LIMIT),
        name="gla",
    )(proj, proj, proj, proj, proj, wa_pad, gla_ba.reshape(1, GLA_QK), gla_g.reshape(1, GLA_DV), s0T)
    return o, sT.transpose(0, 1, 3, 2)


S5_NK = 4
S5_GPK = S5_GROUPS // S5_NK
S5_HALF = S5_GPK * S5_STATE


def _s5_kernel(u_ref, wb_ref, wc_ref, coef_ref, d_ref, gw_ref, gb_ref, h0_ref, y_ref, hfin_ref, hst_ref, xb_ref,
               *, BT, Tc):
    @pl.when(pl.program_id(0) == 0)
    def _():
        hst_ref[...] = h0_ref[...]

    u = u_ref[...]
    ys = []
    for k in range(S5_NK):
        bu = jnp.dot(u[:, k * LANE:(k + 1) * LANE].astype(BF16), wb_ref[k], preferred_element_type=F32)
        bre, bim = bu[:, :S5_HALF], bu[:, S5_HALF:]
        ar, ai = coef_ref[k, 0:1, :], coef_ref[k, 1:2, :]
        cr, ci = coef_ref[k, 2:3, :], coef_ref[k, 3:4, :]
        xb_ref[:, :S5_HALF] = cr * bre - ci * bim
        xb_ref[:, S5_HALF:] = cr * bim + ci * bre

        def step(t, carry):
            hr, hi = carry
            rows = pl.ds(pl.multiple_of(t * BT, BT), BT)
            hr2 = ar * hr - ai * hi + xb_ref[rows, :S5_HALF]
            hi2 = ar * hi + ai * hr + xb_ref[rows, S5_HALF:]
            xb_ref[rows, :S5_HALF] = hr2
            xb_ref[rows, S5_HALF:] = hi2
            return hr2, hi2

        hr, hi = lax.fori_loop(0, Tc, step, (hst_ref[k, :, :S5_HALF], hst_ref[k, :, S5_HALF:]),
                               unroll=min(Tc, 8))
        hst_ref[k, :, :S5_HALF] = hr
        hst_ref[k, :, S5_HALF:] = hi
        ys.append(jnp.dot(xb_ref[...].astype(BF16), wc_ref[k], preferred_element_type=F32))
    y = jax.nn.gelu(jnp.concatenate(ys, axis=1) + d_ref[...] * u)
    z = jnp.dot(y.astype(BF16), gw_ref[...].astype(BF16), preferred_element_type=F32) + gb_ref[...]
    y_ref[...] = y * jax.nn.sigmoid(z)

    @pl.when(pl.program_id(0) == pl.num_programs(0) - 1)
    def _():
        hfin_ref[...] = hst_ref[...]


def _s5(u, a_re, a_im, b_re, b_im, c_re, c_im, d, log_dt, glu_w, glu_b, h0, Tc):
    B, T, _ = u.shape
    dt = jnp.exp(log_dt)[:, None]
    mag = jnp.exp(a_re * dt)
    ab_re, ab_im = mag * jnp.cos(a_im * dt), mag * jnp.sin(a_im * dt)
    den = a_re * a_re + a_im * a_im
    coef_re = ((ab_re - 1) * a_re + ab_im * a_im) / den
    coef_im = (ab_im * a_re - (ab_re - 1) * a_im) / den
    coefs = jnp.stack([ab_re, ab_im, coef_re, coef_im], axis=0).reshape(4, S5_NK, S5_HALF).transpose(1, 0, 2)
    eye = jnp.eye(S5_GPK, dtype=F32)

    def in_weights(bm):
        bk = bm.reshape(S5_NK, S5_GPK, S5_STATE, S5_CH)
        return jnp.einsum('kgpc,gh->kgchp', bk, eye).reshape(S5_NK, S5_GPK * S5_CH, S5_HALF)

    def out_weights(cm):
        ck = cm.reshape(S5_NK, S5_GPK, S5_CH, S5_STATE)
        return jnp.einsum('kgcp,gh->kgphc', ck, eye).reshape(S5_NK, S5_HALF, S5_GPK * S5_CH)

    wb = jnp.concatenate([in_weights(b_re), in_weights(b_im)], axis=2).astype(BF16)
    wc = jnp.concatenate([out_weights(c_re), -out_weights(c_im)], axis=1).astype(BF16)
    if h0 is None:
        hs0 = jnp.zeros((S5_NK, B, 2 * S5_HALF), F32)
    else:
        hs0 = h0.reshape(B, S5_NK, S5_HALF, 2).transpose(1, 0, 3, 2).reshape(S5_NK, B, 2 * S5_HALF)
    ut = u.transpose(1, 0, 2).reshape(T * B, S5_WIDTH)
    rows = Tc * B
    const2 = lambda i: (0, 0)
    const3 = lambda i: (0, 0, 0)
    y, hfin = pl.pallas_call(
        functools.partial(_s5_kernel, BT=B, Tc=Tc),
        grid=(T // Tc,),
        in_specs=[pl.BlockSpec((rows, S5_WIDTH), lambda i: (i, 0)),
                  pl.BlockSpec(wb.shape, const3), pl.BlockSpec(wc.shape, const3), pl.BlockSpec(coefs.shape, const3),
                  pl.BlockSpec((1, S5_WIDTH), const2), pl.BlockSpec((S5_WIDTH, S5_WIDTH), const2),
                  pl.BlockSpec((1, S5_WIDTH), const2), pl.BlockSpec(hs0.shape, const3)],
        out_specs=[pl.BlockSpec((rows, S5_WIDTH), lambda i: (i, 0)), pl.BlockSpec(hs0.shape, const3)],
        out_shape=[jax.ShapeDtypeStruct((T * B, S5_WIDTH), F32), jax.ShapeDtypeStruct(hs0.shape, F32)],
        scratch_shapes=[pltpu.VMEM(hs0.shape, F32), pltpu.VMEM((rows, 2 * S5_HALF), F32)],
        compiler_params=pltpu.CompilerParams(dimension_semantics=("arbitrary",), vmem_limit_bytes=VMEM_LIMIT),
        name="s5",
    )(ut, wb, wc, coefs, d.reshape(1, S5_WIDTH), glu_w, glu_b.reshape(1, S5_WIDTH), hs0)
    y = y.reshape(T, B, S5_WIDTH).transpose(1, 0, 2)
    hfin = hfin.reshape(S5_NK, B, 2, S5_HALF).transpose(1, 0, 3, 2).reshape(B, S5_GROUPS, S5_STATE, 2)
    return y, hfin


AB_ORDER = (0, 5, 6, 1, 3, 4, 2, 7)
AB_START = dict(zip(AB_ORDER, np.cumsum([0] + [AB_SPLITS[s] for s in AB_ORDER[:-1]]).tolist()))
AB_PERM = np.concatenate([np.arange(AB_SPLITS[s]) + sum(AB_SPLITS[:s]) for s in AB_ORDER])


def _mixer_ab(x, norm, w_in, w_out, cmp_pe, cmp_w1, cmp_w2, gla_wa, gla_ba, gla_g, past):
    B, T, _ = x.shape
    projp = _norm_proj(x, norm[0], norm[1], norm[2], w_in[:, AB_PERM])
    seg = lambda s: projp[..., AB_START[s]:AB_START[s] + AB_SPLITS[s]]
    q, kv6, gate_logit = seg(0), seg(1), seg(2)
    kv6 = kv6.reshape(B, T, 6, NSA_KV_HEADS, HEAD_DIM)
    new_nsa, new_win = kv6[:, :, :4], kv6[:, :, 4:]
    s0 = jnp.zeros((B, GLA_HEADS, GLA_DK, GLA_DV), F32)
    if past is None:
        t0 = 0
        kv_full, kv_win = new_nsa, new_win
        cmp_proj = [_cmp_chunk_proj(new_nsa[:, :T // CMP_STRIDE * CMP_STRIDE, c], cmp_w1[c]) for c in range(2)]
    else:
        pool, page_table, win_buf, s0 = past
        pool_t = pool.transpose(0, 2, 3, 4, 1)
        pool_kind = lambda c: pool[:, :, c]
        n_pages = page_table.shape[1]
        t0 = n_pages * PAGE_SIZE
        assert PAGE_SIZE % CMP_STRIDE == 0 and T < CMP_STRIDE
        cmp_proj = [_cmp_chunk_proj(pool_kind(c), cmp_w1[c])[page_table].reshape(
            B, n_pages * (PAGE_SIZE // CMP_STRIDE), NSA_KV_HEADS, 2 * CMP_HID) for c in range(2)]
        kv_win = jnp.concatenate([win_buf, new_win], axis=1)
    q_pos = t0 + jnp.arange(T)
    q4 = q.reshape(B, T, NSA_HEADS, HEAD_DIM)
    kc = _cmp_finish(cmp_proj[0], cmp_pe[0], cmp_w1[0], cmp_w2[0])
    vc = _cmp_finish(cmp_proj[1], cmp_pe[1], cmp_w1[1], cmp_w2[1])
    o_cmp, p_cmp = _cmp_attention(q4, kc, vc, q_pos)
    if past is None:
        n_sel = -(-T // SEL_BLK)
        sel = _selection_mask(_selection_scores(p_cmp, q_pos, n_sel))
        kv_col = lambda kind: AB_START[1] + kind * NSA_KVW
        hpg = NSA_HEADS // NSA_KV_HEADS
        o_slc = _attn(projp, 0, kv_col(2), kv_col(3), G=NSA_KV_HEADS, R=hpg, tq=128, tk=512, sel=sel)
        o_win = _attn(projp, 0, kv_col(4), kv_col(5), G=NSA_KV_HEADS, R=hpg, tq=128, tk=512, window=WINDOW)
        o_slc = o_slc.reshape(B, T, NSA_HEADS, HEAD_DIM)
        o_win = o_win.reshape(B, T, NSA_HEADS, HEAD_DIM)
        win_state = kv_win[:, -min(WINDOW, T):]
    else:
        assert t0 % SEL_BLK == 0 and T <= SEL_BLK
        n_sel = t0 // SEL_BLK + 1
        hpg = NSA_HEADS // NSA_KV_HEADS
        sel = _selection_mask(_selection_scores(p_cmp, q_pos, n_sel))
        sel_rows = jnp.repeat(sel.transpose(0, 2, 1, 3), hpg, axis=2).reshape(B, T * NSA_HEADS, n_sel)
        row_t = jnp.repeat(jnp.arange(T), NSA_HEADS)
        new_j = jnp.arange(LANE)
        bias_pages = jnp.where(jnp.repeat(sel_rows[:, :, :n_sel - 1], SEL_BLK, axis=2) > 0.5, 0.0, NEG_INF)
        causal_new = (new_j[None, :] <= row_t[:, None]) & (new_j[None, :] < T)
        bias_new = jnp.where((sel_rows[:, :, n_sel - 1:] > 0.5) & causal_new[None], 0.0, NEG_INF)
        pad_rows = lambda a, n: jnp.pad(a.reshape(B, a.shape[1], NSA_KVW), ((0, 0), (0, n - a.shape[1]), (0, 0)))
        qblk = _block_diag_queries(q, NSA_KV_HEADS)
        o_slc = _decode_attn(qblk, pad_rows(new_nsa[:, :, 2], LANE), pad_rows(new_nsa[:, :, 3], LANE), bias_new,
                             pool_t=pool_t, page_table=page_table, k_kind=2, v_kind=3, bias_pages=bias_pages)
        o_slc = _own_head_columns(o_slc, T, NSA_KV_HEADS).reshape(B, T, NSA_HEADS, HEAD_DIM)
        wb = win_buf.shape[1]
        n_win = _round_up(wb + T, LANE)
        k_pos = t0 - wb + jnp.arange(n_win)
        in_win = _window_mask(q_pos, k_pos) & (jnp.arange(n_win) < wb + T)[None, :]
        bias_win = jnp.where(in_win, 0.0, NEG_INF)[row_t][None]
        o_win = _decode_attn(qblk, pad_rows(kv_win[:, :, 0], n_win), pad_rows(kv_win[:, :, 1], n_win), bias_win)
        o_win = _own_head_columns(o_win, T, NSA_KV_HEADS).reshape(B, T, NSA_HEADS, HEAD_DIM)
        win_state = kv_win[:, -wb:]
    g = jax.nn.sigmoid(gate_logit).reshape(B, T, NSA_HEADS, 3, 1)
    o_nsa = (g[:, :, :, 0] * o_cmp + g[:, :, :, 1] * o_slc + g[:, :, :, 2] * o_win).reshape(B, T, NSA_WIDTH)
    gla_cols = (AB_START[3], AB_START[4], AB_START[5], AB_START[6], AB_START[7])
    if T % GLA_CHUNK == 0:
        o_gla, s_new = _gla(projp, gla_cols, gla_wa, gla_ba, gla_g, s0, GLA_CHUNK, 4, GLA_CHUNK)
    else:
        Tp = _round_up(T, 2 * SUBLANE)
        o_gla, s_new = _gla(jnp.pad(projp, ((0, 0), (0, Tp - T), (0, 0))), gla_cols, gla_wa, gla_ba, gla_g, s0,
                            Tp, 1, T)
        o_gla = o_gla[:, :T]
    return _out_proj(o_nsa, o_gla, x, norm[3], w_out), (new_nsa, win_state, s_new)


def _mixer_cd(x, norm, w_in, w_out, b_f, a_re, a_im, b_re, b_im, c_re, c_im, d, log_dt, glu_w, glu_b, past):
    B, T, _ = x.shape
    projp = _norm_proj(x, norm[0], norm[1], norm[2], w_in)
    q, k, v, f, u = _split(projp[..., :sum(CD_SPLITS)], CD_SPLITS)
    log_f = jax.nn.log_sigmoid(f + b_f)
    new_kv = jnp.stack([k.reshape(B, T, FOX_HEADS, HEAD_DIM), v.reshape(B, T, FOX_HEADS, HEAD_DIM)], axis=2)
    h0 = None
    if past is None:
        F = jnp.cumsum(log_f, axis=1)
        o_fox = _attn(projp, 0, FOX_WIDTH, 2 * FOX_WIDTH, G=FOX_HEADS, R=1, tq=256, tk=512,
                      fq=F, fk=F.transpose(0, 2, 1))
    else:
        pool_kv, pool_lf, page_table, h0 = past
        t0 = page_table.shape[1] * PAGE_SIZE
        lf_all = jnp.concatenate([_gather_pages(pool_lf, page_table), log_f], axis=1)
        F = jnp.cumsum(lf_all, axis=1)
        Fq = F[:, t0:]
        fq_rows = Fq.reshape(B, T * FOX_HEADS, 1)
        fk_rows = jnp.tile(F.transpose(0, 2, 1), (1, T, 1))
        row_t = jnp.repeat(jnp.arange(T), FOX_HEADS)
        new_j = jnp.arange(LANE)
        bias_pages = fq_rows - fk_rows[:, :, :t0]
        fk_new = jnp.pad(fk_rows[:, :, t0:], ((0, 0), (0, 0), (0, LANE - T)))
        causal_new = (new_j[None, :] <= row_t[:, None]) & (new_j[None, :] < T)
        bias_new = jnp.where(causal_new[None], fq_rows - fk_new, NEG_INF)
        pad_rows = lambda a: jnp.pad(a, ((0, 0), (0, LANE - T), (0, 0)))
        o = _decode_attn(_block_diag_queries(q, FOX_HEADS), pad_rows(k), pad_rows(v), bias_new,
                         pool_t=pool_kv.transpose(0, 2, 3, 4, 1),
                         page_table=page_table, k_kind=0, v_kind=1, bias_pages=bias_pages)
        o_fox = _own_head_columns(o, T, FOX_HEADS)
    y_s5, s5_state = _s5(u, a_re, a_im, b_re, b_im, c_re, c_im, d, log_dt, glu_w, glu_b, h0, min(T, 128))
    return _out_proj(o_fox, y_s5, x, norm[3], w_out), (new_kv, log_f, s5_state)


def kernel(x_prompt, x_sample, cache_ab_nsa_kv, cache_ab_win_kv, state_ab_gla, cache_cd_fox_kv, cache_cd_fox_logf, state_cd_s5, page_table, c_prompt, c_sample, ada_w, ada_b, norm_g, ab_w_in, ab_w_out, nsa_cmp_pe, nsa_cmp_w1, nsa_cmp_w2, gla_w_alpha, gla_b_alpha, gla_norm_g, cd_w_in, cd_w_out, fox_b_f, s5_a_re, s5_a_im, s5_b_re, s5_b_im, s5_c_re, s5_c_im, s5_d, s5_log_dt, s5_glu_w, s5_glu_b, router_w, router_b, moe_w1, moe_b1, moe_w2, moe_b2, final_norm_g):
    xs = [x_prompt, x_sample]
    cs = [c_prompt, c_sample]
    new = [{}, {}]
    n_prompt = x_prompt.shape[0] * x_prompt.shape[1]
    for l in range(DEPTH):
        j = l // 2
        mods = [jnp.split(_mm(jax.nn.silu(c), ada_w[l], ada_b[l]), 6, axis=-1) for c in cs]
        for grp in range(2):
            x = xs[grp]
            sh1, sc1, g1 = mods[grp][:3]
            norm = (norm_g[l, 0], sh1, sc1, g1)
            if l % 2 == 0:
                past = None
                if grp == 1:
                    past = (cache_ab_nsa_kv[j], page_table, cache_ab_win_kv[j], state_ab_gla[j])
                xs[grp], st = _mixer_ab(x, norm, ab_w_in[j], ab_w_out[j], nsa_cmp_pe[j], nsa_cmp_w1[j],
                                        nsa_cmp_w2[j], gla_w_alpha[j], gla_b_alpha[j], gla_norm_g[j], past)
                names = ('nsa_kv', 'win_kv', 'gla')
            else:
                past = None
                if grp == 1:
                    past = (cache_cd_fox_kv[j], cache_cd_fox_logf[j], page_table, state_cd_s5[j])
                xs[grp], st = _mixer_cd(x, norm, cd_w_in[j], cd_w_out[j], fox_b_f[j], s5_a_re[j], s5_a_im[j],
                                        s5_b_re[j], s5_b_im[j], s5_c_re[j], s5_c_im[j], s5_d[j], s5_log_dt[j],
                                        s5_glu_w[j], s5_glu_b[j], past)
                names = ('fox_kv', 'fox_logf', 's5')
            for name, s in zip(names, st):
                new[grp].setdefault(name, []).append(s)
        hn2 = [_modulated_norm(xs[grp], norm_g[l, 1], mods[grp][3], mods[grp][4]).reshape(-1, D_MODEL)
               for grp in range(2)]
        ym = _moe(jnp.concatenate(hn2, axis=0), router_w[l], router_b[l], moe_w1[l], moe_b1[l], moe_w2[l], moe_b2[l])
        yms = [ym[:n_prompt], ym[n_prompt:]]
        for grp in range(2):
            xs[grp] = xs[grp] + mods[grp][5][:, None, :] * yms[grp].reshape(xs[grp].shape)
    ys = [_rmsnorm(x, final_norm_g) for x in xs]
    names = ('nsa_kv', 'win_kv', 'gla', 'fox_kv', 'fox_logf', 's5')
    outs = [jnp.stack(new[grp][name]) for grp in range(2) for name in names]
    return (ys[0], ys[1]) + tuple(outs)
```

```python
import functools
import math

import numpy as np
import jax
import jax.numpy as jnp
from jax import lax
from jax.experimental import pallas as pl
from jax.experimental.pallas import tpu as pltpu

D_MODEL = 1024
DEPTH = 2
PAGE_SIZE = 128
HEAD_DIM = 64
NSA_HEADS = 8
NSA_KV_HEADS = 2
CMP_BLK = 32
CMP_STRIDE = 16
SEL_BLK = 64
SEL_TOPN = 16
WINDOW = 512
GLA_HEADS = 4
GLA_DK = 64
GLA_DV = 128
GLA_LOWRANK = 16
GLA_TAU = 16.0
GLA_CHUNK = 64
FOX_HEADS = 8
S5_GROUPS = 32
S5_CH = 16
S5_STATE = 64
N_EXPERTS = 32
TOP_K = 4
D_FF = D_MODEL
SWIGLU_LIMIT = 7.0
SWIGLU_ALPHA = 1.702
EPS = 1e-6
NEG_INF = -1e30
BIG = 1e9
NSA_WIDTH = NSA_HEADS * HEAD_DIM
NSA_KVW = NSA_KV_HEADS * HEAD_DIM
GLA_QK = GLA_HEADS * GLA_DK
GLA_WIDTH = GLA_HEADS * GLA_DV
FOX_WIDTH = FOX_HEADS * HEAD_DIM
S5_WIDTH = S5_GROUPS * S5_CH
AB_SPLITS = (NSA_WIDTH, 6 * NSA_KVW, 3 * NSA_HEADS, GLA_QK, GLA_QK, GLA_WIDTH, GLA_WIDTH, GLA_LOWRANK)
CD_SPLITS = (FOX_WIDTH, FOX_WIDTH, FOX_WIDTH, FOX_HEADS, S5_WIDTH)

LANE = 128
SUBLANE = 8
VMEM_LIMIT = 56 * 1024 * 1024
MOE_TM = 256

F32 = jnp.float32
BF16 = jnp.bfloat16


def _round_up(n, m):
    return -(-n // m) * m


def _pick_tile(n, candidates):
    for c in candidates:
        if n % c == 0:
            return c
    return n


def _mm_kernel(x_ref, w_ref, b_ref, o_ref, wb_ref, *, act, precise):
    if precise:
        acc = jnp.dot(x_ref[...], w_ref[...], preferred_element_type=F32, precision=lax.Precision.HIGHEST)
    else:
        @pl.when(pl.program_id(1) == 0)
        def _():
            wb_ref[...] = w_ref[...].astype(BF16)

        acc = jnp.dot(x_ref[...].astype(BF16), wb_ref[...], preferred_element_type=F32)
    acc = acc + b_ref[...]
    if act == 'gelu':
        acc = jax.nn.gelu(acc)
    o_ref[...] = acc.astype(o_ref.dtype)


def _mm(x, w, b=None, act=None, out_dtype=F32, precise=False, keep_cols=False):
    M, K = x.shape
    N = w.shape[1]
    Np = _round_up(N, 2 * LANE) if N > 2 * LANE else _round_up(N, LANE)
    Mp = _round_up(M, 512) if M > 256 else _round_up(M, SUBLANE)
    tm = min(Mp, 512)
    tn = _pick_tile(Np, (512, 256, 128))
    if Np != N:
        w = jnp.pad(w, ((0, 0), (0, Np - N)))
    if b is None:
        b = jnp.zeros((N,), F32)
    b = jnp.pad(b.astype(F32), (0, Np - N)).reshape(1, Np)
    if Mp != M:
        x = jnp.pad(x, ((0, Mp - M), (0, 0)))
    out = pl.pallas_call(
        functools.partial(_mm_kernel, act=act, precise=precise),
        grid=(Np // tn, Mp // tm),
        in_specs=[pl.BlockSpec((tm, K), lambda j, i: (i, 0)),
                  pl.BlockSpec((K, tn), lambda j, i: (0, j)),
                  pl.BlockSpec((1, tn), lambda j, i: (0, j))],
        out_specs=pl.BlockSpec((tm, tn), lambda j, i: (i, j)),
        out_shape=jax.ShapeDtypeStruct((Mp, Np), out_dtype),
        scratch_shapes=[pltpu.VMEM((K, tn), BF16)],
        compiler_params=pltpu.CompilerParams(
            dimension_semantics=("arbitrary", "arbitrary"), vmem_limit_bytes=VMEM_LIMIT),
        name="mm",
    )(x, w, b)
    return out[:M] if keep_cols else out[:M, :N]


PROJ_TM = 512


def _norm_proj_kernel(x_ref, g_ref, sh_ref, sc_ref, w_ref, o_ref):
    x = x_ref[0]
    y = x * lax.rsqrt(jnp.mean(x * x, axis=-1, keepdims=True) + EPS) * g_ref[...]
    hn = y * (1 + sc_ref[0]) + sh_ref[0]
    o_ref[0] = jnp.dot(hn.astype(BF16), w_ref[...], preferred_element_type=F32)


def _rows_and_mods(x, mods):
    B, T, D = x.shape
    if T % PROJ_TM == 0:
        return x, [m.reshape(B, 1, D) for m in mods]
    assert (B * T) % PROJ_TM == 0 or B * T <= PROJ_TM
    return x.reshape(1, B * T, D), [jnp.repeat(m, T, axis=0).reshape(1, B * T, D) for m in mods]


def _mod_spec(m, tm):
    if m.shape[1] == 1:
        return pl.BlockSpec((1, 1, m.shape[2]), lambda b, i: (b, 0, 0))
    return pl.BlockSpec((1, tm, m.shape[2]), lambda b, i: (b, i, 0))


def _norm_proj(x, g, shift, scale, w):
    B, T, D = x.shape
    N = w.shape[1]
    Np = _round_up(N, 2 * LANE)
    wb = jnp.pad(w, ((0, 0), (0, Np - N))).astype(BF16)
    xr, (sh, sc) = _rows_and_mods(x, [shift, scale])
    Bm, Tm, _ = xr.shape
    tm = min(PROJ_TM, Tm)
    out = pl.pallas_call(
        _norm_proj_kernel,
        grid=(Bm, Tm // tm),
        in_specs=[pl.BlockSpec((1, tm, D), lambda b, i: (b, i, 0)),
                  pl.BlockSpec((1, D), lambda b, i: (0, 0)),
                  _mod_spec(sh, tm), _mod_spec(sc, tm),
                  pl.BlockSpec((D, Np), lambda b, i: (0, 0))],
        out_specs=pl.BlockSpec((1, tm, Np), lambda b, i: (b, i, 0)),
        out_shape=jax.ShapeDtypeStruct((Bm, Tm, Np), F32),
        compiler_params=pltpu.CompilerParams(
            dimension_semantics=("arbitrary", "arbitrary"), vmem_limit_bytes=VMEM_LIMIT),
        name="norm_proj",
    )(xr, g.reshape(1, D), sh, sc, wb)
    return out.reshape(B, T, Np)


def _out_proj_kernel(a_ref, b_ref, x_ref, g_ref, w_ref, o_ref):
    wa = a_ref.shape[2]
    y = jnp.dot(a_ref[0].astype(BF16), w_ref[:wa, :], preferred_element_type=F32)
    y = y + jnp.dot(b_ref[0].astype(BF16), w_ref[wa:, :], preferred_element_type=F32)
    o_ref[0] = x_ref[0] + g_ref[0] * y


def _out_proj(a, b, x, gate, w):
    B, T, D = x.shape
    wa, wb_ = a.shape[2], b.shape[2]
    xr, (g,) = _rows_and_mods(x, [gate])
    Bm, Tm, _ = xr.shape
    tm = min(PROJ_TM, Tm)
    row = lambda b_, i: (b_, i, 0)
    out = pl.pallas_call(
        _out_proj_kernel,
        grid=(Bm, Tm // tm),
        in_specs=[pl.BlockSpec((1, tm, wa), row), pl.BlockSpec((1, tm, wb_), row), pl.BlockSpec((1, tm, D), row),
                  _mod_spec(g, tm), pl.BlockSpec((wa + wb_, D), lambda b_, i: (0, 0))],
        out_specs=pl.BlockSpec((1, tm, D), row),
        out_shape=jax.ShapeDtypeStruct((Bm, Tm, D), F32),
        compiler_params=pltpu.CompilerParams(
            dimension_semantics=("arbitrary", "arbitrary"), vmem_limit_bytes=VMEM_LIMIT),
        name="out_proj",
    )(a.reshape(Bm, Tm, wa), b.reshape(Bm, Tm, wb_), xr, g, w.astype(BF16))
    return out.reshape(B, T, D)


def _flash_kernel(*refs, tq, tk, R, window, use_bias, use_sel):
    q_ref, k_ref, v_ref = refs[:3]
    n = 3
    if use_bias:
        qb_ref, kb_ref = refs[n:n + 2]
        n += 2
    if use_sel:
        sel_ref = refs[n]
        n += 1
    o_ref = refs[n]
    i = pl.program_id(2)
    rows = R * tq
    q = q_ref[0, 0].reshape(rows, HEAD_DIM)
    q_pos = i * tq + jnp.bitwise_and(lax.broadcasted_iota(jnp.int32, (rows, 1), 0), tq - 1)
    hi = ((i + 1) * tq + tk - 1) // tk
    lo = jnp.maximum(i * tq - window + 1, 0) // tk if window else 0
    if use_bias:
        qb = qb_ref[0, 0]
        qb = jnp.concatenate([qb] * (tk // LANE), axis=1)
    if use_sel:
        sel = sel_ref[0, 0].astype(BF16)
        n_sel = sel.shape[1]

    def body(j, carry):
        m, l, acc = carry
        start = pl.multiple_of(j * tk, tk)
        k = k_ref[0, 0, pl.ds(start, tk), :]
        v = v_ref[0, 0, pl.ds(start, tk), :]
        s = lax.dot_general(q, k, (((1,), (1,)), ((), ())), preferred_element_type=F32)
        k_pos = j * tk + lax.broadcasted_iota(jnp.int32, (1, tk), 1)
        if use_bias:
            s = s + qb - kb_ref[0, 0, j]
        s = jnp.where(k_pos <= q_pos, s, NEG_INF)
        if window:
            s = jnp.where(q_pos - k_pos < window, s, NEG_INF)
        if use_sel:
            blk_of_key = (j * tk + lax.broadcasted_iota(jnp.int32, (n_sel, tk), 1)) // SEL_BLK
            expand = jnp.where(blk_of_key == lax.broadcasted_iota(jnp.int32, (n_sel, tk), 0), 1.0, 0.0).astype(BF16)
            chosen = jnp.dot(sel, expand, preferred_element_type=F32)
            chosen = jnp.concatenate([chosen] * R, axis=0)
            s = jnp.where(chosen > 0.5, s, NEG_INF)
        m_new = jnp.maximum(m, jnp.max(s, axis=1, keepdims=True))
        p = jnp.where(s > 0.5 * NEG_INF, jnp.exp(s - m_new), 0.0)
        alpha = jnp.exp(m - m_new)
        l = alpha * l + jnp.sum(p, axis=1, keepdims=True)
        acc = alpha * acc + jnp.dot(p.astype(BF16), v, preferred_element_type=F32)
        return m_new, l, acc

    m0 = jnp.full((rows, 1), NEG_INF, F32)
    l0 = jnp.zeros((rows, 1), F32)
    a0 = jnp.zeros((rows, HEAD_DIM), F32)
    m, l, acc = lax.fori_loop(lo, hi, body, (m0, l0, a0))
    o_ref[0, 0] = (acc / l).reshape(R, tq, HEAD_DIM)


def _flash(q, k, v, *, tq, tk, window=0, qbias=None, kbias=None, sel=None):
    B, G, R, T, _ = q.shape
    grid = (B, G, T // tq)
    in_specs = [pl.BlockSpec((1, 1, R, tq, HEAD_DIM), lambda b, g, i: (b, g, 0, i, 0)),
                pl.BlockSpec((1, 1, T, HEAD_DIM), lambda b, g, i: (b, g, 0, 0)),
                pl.BlockSpec((1, 1, T, HEAD_DIM), lambda b, g, i: (b, g, 0, 0))]
    args = [q, k, v]
    if qbias is not None:
        in_specs += [pl.BlockSpec((1, 1, tq, LANE), lambda b, g, i: (b, g, i, 0)),
                     pl.BlockSpec((1, 1, T // tk, 1, tk), lambda b, g, i: (b, g, 0, 0, 0))]
        args += [qbias, kbias.reshape(B, G, T // tk, 1, tk)]
    if sel is not None:
        in_specs += [pl.BlockSpec((1, 1, tq, sel.shape[-1]), lambda b, g, i: (b, g, i, 0))]
        args += [sel]
    return pl.pallas_call(
        functools.partial(_flash_kernel, tq=tq, tk=tk, R=R, window=window,
                          use_bias=qbias is not None, use_sel=sel is not None),
        grid=grid,
        in_specs=in_specs,
        out_specs=pl.BlockSpec((1, 1, R, tq, HEAD_DIM), lambda b, g, i: (b, g, 0, i, 0)),
        out_shape=jax.ShapeDtypeStruct((B, G, R, T, HEAD_DIM), F32),
        compiler_params=pltpu.CompilerParams(
            dimension_semantics=("arbitrary", "arbitrary", "arbitrary"), vmem_limit_bytes=VMEM_LIMIT),
        name="flash",
    )(*args)


def _attn_kernel(*refs, tq, tk, G, R, window, use_bias, use_sel):
    q_ref, k_ref, v_ref = refs[:3]
    n = 3
    if use_bias:
        fq_ref, fk_ref = refs[n:n + 2]
        n += 2
    if use_sel:
        sel_ref = refs[n]
        n += 1
    o_ref, kb_ref, vb_ref = refs[n:n + 3]
    i = pl.program_id(1)

    @pl.when(i == 0)
    def _():
        kb_ref[...] = k_ref[0].astype(BF16)
        vb_ref[...] = v_ref[0].astype(BF16)

    rows = R * tq
    q_pos = i * tq + jnp.bitwise_and(lax.broadcasted_iota(jnp.int32, (rows, 1), 0), tq - 1)
    hi = ((i + 1) * tq + tk - 1) // tk
    lo = jnp.maximum(i * tq - window + 1, 0) // tk if window else 0
    outs = []
    for g in range(G):
        cols = slice(g * HEAD_DIM, (g + 1) * HEAD_DIM)
        qg = q_ref[0, :, g * R * HEAD_DIM:(g + 1) * R * HEAD_DIM] * HEAD_DIM ** -0.5
        q = jnp.concatenate([qg[:, r * HEAD_DIM:(r + 1) * HEAD_DIM] for r in range(R)], axis=0).astype(BF16)
        if use_bias:
            fq = fq_ref[0, :, g:g + 1]
        if use_sel:
            sel = sel_ref[0, g].astype(BF16)
            n_sel = sel.shape[1]

        def body(j, carry):
            m, l, acc = carry
            start = pl.multiple_of(j * tk, tk)
            k = kb_ref[pl.ds(start, tk), cols]
            v = vb_ref[pl.ds(start, tk), cols]
            s = lax.dot_general(q, k, (((1,), (1,)), ((), ())), preferred_element_type=F32)
            k_pos = j * tk + lax.broadcasted_iota(jnp.int32, (1, tk), 1)
            if use_bias:
                s = s + fq - fk_ref[0, g, j]
            s = jnp.where(k_pos <= q_pos, s, NEG_INF)
            if window:
                s = jnp.where(q_pos - k_pos < window, s, NEG_INF)
            if use_sel:
                blk_of_key = (j * tk + lax.broadcasted_iota(jnp.int32, (n_sel, tk), 1)) // SEL_BLK
                expand = jnp.where(blk_of_key == lax.broadcasted_iota(jnp.int32, (n_sel, tk), 0), 1.0, 0.0)
                chosen = jnp.dot(sel, expand.astype(BF16), preferred_element_type=F32)
                chosen = jnp.concatenate([chosen] * R, axis=0)
                s = jnp.where(chosen > 0.5, s, NEG_INF)
            m_new = jnp.maximum(m, jnp.max(s, axis=1, keepdims=True))
            p = jnp.where(s > 0.5 * NEG_INF, jnp.exp(s - m_new), 0.0)
            alpha = jnp.exp(m - m_new)
            l = alpha * l + jnp.sum(p, axis=1, keepdims=True)
            acc = alpha * acc + jnp.dot(p.astype(BF16), v, preferred_element_type=F32)
            return m_new, l, acc

        m0 = jnp.full((rows, 1), NEG_INF, F32)
        l0 = jnp.zeros((rows, 1), F32)
        a0 = jnp.zeros((rows, HEAD_DIM), F32)
        m, l, acc = lax.fori_loop(lo, hi, body, (m0, l0, a0))
        o = acc / l
        outs += [o[r * tq:(r + 1) * tq] for r in range(R)]
    o_ref[0] = jnp.concatenate(outs, axis=1)


def _attn(proj, q_col, k_col, v_col, *, G, R, tq, tk, window=0, fq=None, fk=None, sel=None):
    B, T, _ = proj.shape
    qw, kw = G * R * HEAD_DIM, G * HEAD_DIM
    assert q_col % qw == 0 and k_col % kw == 0 and v_col % kw == 0
    in_specs = [pl.BlockSpec((1, tq, qw), lambda b, i: (b, i, q_col // qw)),
                pl.BlockSpec((1, T, kw), lambda b, i: (b, 0, k_col // kw)),
                pl.BlockSpec((1, T, kw), lambda b, i: (b, 0, v_col // kw))]
    args = [proj, proj, proj]
    if fq is not None:
        in_specs += [pl.BlockSpec((1, tq, G), lambda b, i: (b, i, 0)),
                     pl.BlockSpec((1, G, T // tk, 1, tk), lambda b, i: (b, 0, 0, 0, 0))]
        args += [fq, fk.reshape(B, G, T // tk, 1, tk)]
    if sel is not None:
        in_specs += [pl.BlockSpec((1, G, tq, sel.shape[-1]), lambda b, i: (b, 0, i, 0))]
        args += [sel]
    return pl.pallas_call(
        functools.partial(_attn_kernel, tq=tq, tk=tk, G=G, R=R, window=window,
                          use_bias=fq is not None, use_sel=sel is not None),
        grid=(B, T // tq),
        in_specs=in_specs,
        out_specs=pl.BlockSpec((1, tq, qw), lambda b, i: (b, i, 0)),
        out_shape=jax.ShapeDtypeStruct((B, T, qw), F32),
        scratch_shapes=[pltpu.VMEM((T, kw), BF16), pltpu.VMEM((T, kw), BF16)],
        compiler_params=pltpu.CompilerParams(
            dimension_semantics=("arbitrary", "arbitrary"), vmem_limit_bytes=VMEM_LIMIT),
        name="attn",
    )(*args)


def _decode_attn_kernel(*refs, n_pages):
    pt_ref, q_ref = refs[0], refs[1]
    del pt_ref
    k_refs = refs[2:2 + n_pages]
    v_refs = refs[2 + n_pages:2 + 2 * n_pages]
    n = 2 + 2 * n_pages
    if n_pages:
        bp_ref = refs[n]
        n += 1
    ke_ref, ve_ref, be_ref, o_ref = refs[n:n + 4]
    nt = (((1,), (1,)), ((), ()))
    q = q_ref[0].astype(BF16)
    kw = q.shape[1]
    se = lax.dot_general(q, ke_ref[0].astype(BF16), nt, preferred_element_type=F32) + be_ref[0]
    m = jnp.max(se, axis=1, keepdims=True)
    if n_pages:
        s = jnp.concatenate([jnp.dot(q, k_refs[p][0, 0].reshape(kw, PAGE_SIZE).astype(BF16),
                                     preferred_element_type=F32) for p in range(n_pages)], axis=1) + bp_ref[0]
        m = jnp.maximum(m, jnp.max(s, axis=1, keepdims=True))
    pe = jnp.exp(se - m)
    l = jnp.sum(pe, axis=1, keepdims=True)
    o = jnp.dot(pe.astype(BF16), ve_ref[0].astype(BF16), preferred_element_type=F32)
    if n_pages:
        p = jnp.exp(s - m)
        l = l + jnp.sum(p, axis=1, keepdims=True)
        pb = p.astype(BF16)
        for pg in range(n_pages):
            o = o + lax.dot_general(pb[:, pg * PAGE_SIZE:(pg + 1) * PAGE_SIZE],
                                    v_refs[pg][0, 0].reshape(kw, PAGE_SIZE).astype(BF16), nt,
                                    preferred_element_type=F32)
    o_ref[0] = o / l


def _decode_attn(qblk, k_extra, v_extra, bias_extra, pool_t=None, page_table=None, k_kind=0, v_kind=0,
                 bias_pages=None):
    B, R, KW = qblk.shape
    NE = k_extra.shape[1]
    n_pages = 0 if pool_t is None else page_table.shape[1]
    row3 = lambda b, pt: (b, 0, 0)
    in_specs = [pl.BlockSpec((1, R, KW), row3)]
    args = [qblk]
    if n_pages:
        G = pool_t.shape[2]
        assert G * HEAD_DIM == KW
        for kind in (k_kind, v_kind):
            for p in range(n_pages):
                in_specs.append(pl.BlockSpec((1, 1, G, HEAD_DIM, PAGE_SIZE),
                                             functools.partial(lambda b, pt, p, c: (pt[b, p], c, 0, 0, 0), p=p, c=kind)))
                args.append(pool_t)
        in_specs.append(pl.BlockSpec((1, R, n_pages * PAGE_SIZE), row3))
        args.append(bias_pages)
    else:
        page_table = jnp.zeros((1, 1), jnp.int32)
    be_map = row3 if bias_extra.shape[0] == B else (lambda b, pt: (0, 0, 0))
    in_specs += [pl.BlockSpec((1, NE, KW), row3), pl.BlockSpec((1, NE, KW), row3), pl.BlockSpec((1, R, NE), be_map)]
    args += [k_extra, v_extra, bias_extra]
    grid_spec = pltpu.PrefetchScalarGridSpec(
        num_scalar_prefetch=1, grid=(B,), in_specs=in_specs,
        out_specs=pl.BlockSpec((1, R, KW), row3))
    return pl.pallas_call(
        functools.partial(_decode_attn_kernel, n_pages=n_pages),
        grid_spec=grid_spec,
        out_shape=jax.ShapeDtypeStruct((B, R, KW), F32),
        compiler_params=pltpu.CompilerParams(dimension_semantics=("arbitrary",), vmem_limit_bytes=VMEM_LIMIT),
        name="decode_attn",
    )(page_table, *args)


def _block_diag_queries(q, G):
    B, T, W = q.shape
    H = W // HEAD_DIM
    own = (jnp.arange(H)[:, None] // (H // G) == jnp.arange(G)[None, :]).astype(F32)
    qs = (q * HEAD_DIM ** -0.5).reshape(B, T, H, 1, HEAD_DIM) * own[None, None, :, :, None]
    return qs.reshape(B, T * H, G * HEAD_DIM)


def _own_head_columns(o, T, G):
    B, R, _ = o.shape
    H = R // T
    own = (jnp.arange(H)[:, None] // (H // G) == jnp.arange(G)[None, :]).astype(F32)
    o5 = o.reshape(B, T, H, G, HEAD_DIM) * own[None, None, :, :, None]
    return jnp.sum(o5, axis=3).reshape(B, T, H * HEAD_DIM)


def _moe_kernel(be_ref, nb_ref, x_ref, w1_ref, b1_ref, w2_ref, b2_ref, o_ref, w1b_ref, w2b_ref):
    i = pl.program_id(0)
    prev = be_ref[jnp.maximum(i - 1, 0)]

    @pl.when(jnp.logical_or(i == 0, be_ref[i] != prev))
    def _():
        w1b_ref[...] = w1_ref[0].astype(BF16)
        w2b_ref[...] = w2_ref[0].astype(BF16)

    @pl.when(i < nb_ref[0])
    def _():
        gu = jnp.dot(x_ref[...].astype(BF16), w1b_ref[...], preferred_element_type=F32) + b1_ref[0]
        g = jnp.minimum(gu[:, :D_FF], SWIGLU_LIMIT)
        up = jnp.clip(gu[:, D_FF:], -SWIGLU_LIMIT, SWIGLU_LIMIT)
        h = (up + 1) * g * jax.nn.sigmoid(SWIGLU_ALPHA * g)
        o_ref[...] = jnp.dot(h.astype(BF16), w2b_ref[...], preferred_element_type=F32) + b2_ref[0]

    @pl.when(i >= nb_ref[0])
    def _():
        o_ref[...] = jnp.zeros_like(o_ref)


def _moe_experts(xs, blk_e, n_used, w1, b1, w2, b2):
    n_blocks = xs.shape[0] // MOE_TM
    grid_spec = pltpu.PrefetchScalarGridSpec(
        num_scalar_prefetch=2,
        grid=(n_blocks,),
        in_specs=[pl.BlockSpec((MOE_TM, D_MODEL), lambda i, be, nb: (i, 0)),
                  pl.BlockSpec((1, D_MODEL, 2 * D_FF), lambda i, be, nb: (be[i], 0, 0)),
                  pl.BlockSpec((1, 1, 2 * D_FF), lambda i, be, nb: (be[i], 0, 0)),
                  pl.BlockSpec((1, D_FF, D_MODEL), lambda i, be, nb: (be[i], 0, 0)),
                  pl.BlockSpec((1, 1, D_MODEL), lambda i, be, nb: (be[i], 0, 0))],
        out_specs=pl.BlockSpec((MOE_TM, D_MODEL), lambda i, be, nb: (i, 0)),
        scratch_shapes=[pltpu.VMEM((D_MODEL, 2 * D_FF), BF16), pltpu.VMEM((D_FF, D_MODEL), BF16)],
    )
    return pl.pallas_call(
        _moe_kernel,
        grid_spec=grid_spec,
        out_shape=jax.ShapeDtypeStruct((n_blocks * MOE_TM, D_MODEL), F32),
        compiler_params=pltpu.CompilerParams(
            dimension_semantics=("arbitrary",), vmem_limit_bytes=VMEM_LIMIT),
        name="moe_experts",
    )(blk_e, n_used, xs, w1, b1.reshape(N_EXPERTS, 1, 2 * D_FF), w2, b2.reshape(N_EXPERTS, 1, D_MODEL))


def _moe_combine_kernel(y_ref, g_ref, o_ref):
    g = g_ref[...]
    acc = g[:, 0:1] * y_ref[:, :D_MODEL]
    for k in range(1, TOP_K):
        acc = acc + g[:, k:k + 1] * y_ref[:, k * D_MODEL:(k + 1) * D_MODEL]
    o_ref[...] = acc


def _moe_combine(yg, gate):
    N = yg.shape[0]
    tn = _pick_tile(N, (512, 256, 128, 64, 32, 16, 8))
    return pl.pallas_call(
        _moe_combine_kernel,
        grid=(N // tn,),
        in_specs=[pl.BlockSpec((tn, TOP_K * D_MODEL), lambda i: (i, 0)),
                  pl.BlockSpec((tn, TOP_K), lambda i: (i, 0))],
        out_specs=pl.BlockSpec((tn, D_MODEL), lambda i: (i, 0)),
        out_shape=jax.ShapeDtypeStruct((N, D_MODEL), F32),
        compiler_params=pltpu.CompilerParams(dimension_semantics=("arbitrary",), vmem_limit_bytes=VMEM_LIMIT),
        name="moe_combine",
    )(yg, gate)


def _moe(xf, w_r, b_r, w1, b1, w2, b2):
    N = xf.shape[0]
    n_rows = N * TOP_K
    logits = _mm(xf, w_r, b_r, precise=True)
    top_v, top_i = lax.top_k(logits, TOP_K)
    gate = jax.nn.softmax(top_v, axis=-1)
    flat_e = top_i.reshape(-1)
    order = jnp.argsort(flat_e)
    sorted_e = flat_e[order]
    counts = jnp.bincount(flat_e, length=N_EXPERTS)
    padded = (counts + MOE_TM - 1) // MOE_TM * MOE_TM
    pad_end = jnp.cumsum(padded)
    pad_start = pad_end - padded
    start = jnp.cumsum(counts) - counts
    dest = (pad_start[sorted_e] + jnp.arange(n_rows) - start[sorted_e]).astype(jnp.int32)
    n_blocks = -(-(n_rows + N_EXPERTS * (MOE_TM - 1)) // MOE_TM)
    n_used = (pad_end[-1] // MOE_TM).astype(jnp.int32)
    blk = jnp.minimum(jnp.arange(n_blocks), n_used - 1) * MOE_TM
    blk_e = jnp.minimum(jnp.sum(pad_end[None, :] <= blk[:, None], axis=1), N_EXPERTS - 1).astype(jnp.int32)
    row_e = jnp.repeat(blk_e, MOE_TM)
    off = jnp.arange(n_blocks * MOE_TM) - pad_start[row_e]
    src = jnp.clip(start[row_e] + off, 0, n_rows - 1)
    row_tok = jnp.where(off < counts[row_e], (order // TOP_K)[src], 0).astype(jnp.int32)
    xs = xf[row_tok]
    ys = _moe_experts(xs, blk_e, n_used.reshape(1), w1, b1, w2, b2)
    pos = dest[jnp.argsort(order)]
    return _moe_combine(ys[pos].reshape(N, TOP_K * D_MODEL), gate)


def _rmsnorm(x, g):
    y = x * lax.rsqrt(jnp.mean(x * x, axis=-1, keepdims=True) + EPS)
    return y * g


def _modulated_norm(x, g, shift, scale):
    return _rmsnorm(x, g) * (1 + scale[:, None, :]) + shift[:, None, :]


def _split(x, sizes):
    return jnp.split(x, np.cumsum(sizes)[:-1].tolist(), axis=-1)


def _gather_pages(pool, page_table):
    g = pool[page_table]
    return g.reshape((g.shape[0], g.shape[1] * g.shape[2]) + g.shape[3:])


def _attend(q, k, v, mask, bias=None):
    B, Q, H, D = q.shape
    G = k.shape[2]
    s = jnp.einsum('bqgnd,bkgd->bgnqk', q.reshape(B, Q, G, H // G, D), k).astype(F32) * D ** -0.5
    if bias is not None:
        s = s + bias
    p = jax.nn.softmax(jnp.where(mask, s, NEG_INF), axis=-1).astype(v.dtype)
    return jnp.einsum('bgnqk,bkgd->bqgnd', p, v).reshape(B, Q, H, D)


def _window_mask(q_pos, k_pos):
    d = q_pos[:, None] - k_pos[None, :]
    return (d >= 0) & (d < WINDOW) & (k_pos[None, :] >= 0)


def _to_heads(x, G):
    B, T, _ = x.shape
    return x.astype(BF16).reshape(B, T, G, HEAD_DIM).transpose(0, 2, 1, 3)


def _q_to_heads(q, G):
    B, T, W = q.shape
    R = W // HEAD_DIM // G
    return (q * HEAD_DIM ** -0.5).astype(BF16).reshape(B, T, G, R, HEAD_DIM).transpose(0, 2, 3, 1, 4)


def _from_heads(o):
    B, G, R, T, D = o.shape
    return o.transpose(0, 3, 1, 2, 4).reshape(B, T, G * R * D)


CMP_CHUNK_W = CMP_STRIDE * HEAD_DIM
CMP_HID = 256


def _cmp_chunk_proj(tokens, w1):
    assert CMP_BLK == 2 * CMP_STRIDE
    S, T, G, D = tokens.shape
    n = T // CMP_STRIDE
    ch = tokens.reshape(S, n, CMP_STRIDE, G, D).transpose(0, 1, 3, 2, 4).reshape(S * n * G, CMP_CHUNK_W)
    wcat = jnp.concatenate([w1[:CMP_CHUNK_W], w1[CMP_CHUNK_W:]], axis=1)
    return _mm(ch, wcat).reshape(S, n, G, 2 * CMP_HID)


CMP_PAGES = 32


def _cmp_pages_kernel(p_ref, w_ref, o_ref, mt_ref):
    P = p_ref.shape[0]
    chunks = PAGE_SIZE // CMP_STRIDE
    for kind in range(2):
        for g in range(NSA_KV_HEADS):
            for p in range(P):
                mt_ref[p * PAGE_SIZE:(p + 1) * PAGE_SIZE, :] = p_ref[p, kind, g].T
            acc = jnp.zeros((P * chunks, 2 * CMP_HID), F32)
            for j in range(CMP_STRIDE):
                rows = mt_ref[pl.ds(j, P * chunks, stride=CMP_STRIDE), :].astype(BF16)
                acc = acc + jnp.dot(rows, w_ref[kind, j * HEAD_DIM:(j + 1) * HEAD_DIM, :], preferred_element_type=F32)
            o_ref[kind, :, g * 2 * CMP_HID:(g + 1) * 2 * CMP_HID] = acc


def _cmp_pages_proj(pool_t, w1):
    n_phys = pool_t.shape[0]
    chunks = PAGE_SIZE // CMP_STRIDE
    assert n_phys % CMP_PAGES == 0
    wcat = jnp.concatenate([w1[:, :CMP_CHUNK_W], w1[:, CMP_CHUNK_W:]], axis=2).astype(BF16)
    width = NSA_KV_HEADS * 2 * CMP_HID
    out = pl.pallas_call(
        _cmp_pages_kernel,
        grid=(n_phys // CMP_PAGES,),
        in_specs=[pl.BlockSpec((CMP_PAGES, 2, NSA_KV_HEADS, HEAD_DIM, PAGE_SIZE), lambda i: (i, 0, 0, 0, 0)),
                  pl.BlockSpec(wcat.shape, lambda i: (0, 0, 0))],
        out_specs=pl.BlockSpec((2, CMP_PAGES * chunks, width), lambda i: (0, i, 0)),
        out_shape=jax.ShapeDtypeStruct((2, n_phys * chunks, width), F32),
        scratch_shapes=[pltpu.VMEM((CMP_PAGES * PAGE_SIZE, HEAD_DIM), F32)],
        compiler_params=pltpu.CompilerParams(dimension_semantics=("arbitrary",), vmem_limit_bytes=VMEM_LIMIT),
        name="cmp_pages",
    )(pool_t, wcat)
    return out.reshape(2, n_phys, chunks, NSA_KV_HEADS, 2 * CMP_HID)


def _cmp_out_kernel(a_ref, b_ref, bias_ref, w_ref, o_ref):
    hid = jax.nn.gelu(a_ref[...] + b_ref[...] + bias_ref[...])
    o_ref[...] = jnp.dot(hid.astype(BF16), w_ref[...].astype(BF16), preferred_element_type=F32)


def _cmp_finish(proj, pe, w1, w2):
    B, n, G, _ = proj.shape
    first = proj[:, :-1, :, :CMP_HID].reshape(-1, CMP_HID)
    second = proj[:, 1:, :, CMP_HID:].reshape(-1, CMP_HID)
    bias = _mm(pe.reshape(1, CMP_BLK * HEAD_DIM), w1)
    rows = first.shape[0]
    rows_p = _round_up(rows, 512)
    pad = lambda a: jnp.pad(a, ((0, rows_p - rows), (0, 0)))
    w2p = jnp.pad(w2, ((0, 0), (0, LANE - HEAD_DIM)))
    out = pl.pallas_call(
        _cmp_out_kernel,
        grid=(rows_p // 512,),
        in_specs=[pl.BlockSpec((512, CMP_HID), lambda i: (i, 0)), pl.BlockSpec((512, CMP_HID), lambda i: (i, 0)),
                  pl.BlockSpec((1, CMP_HID), lambda i: (0, 0)), pl.BlockSpec((CMP_HID, LANE), lambda i: (0, 0))],
        out_specs=pl.BlockSpec((512, LANE), lambda i: (i, 0)),
        out_shape=jax.ShapeDtypeStruct((rows_p, LANE), F32),
        compiler_params=pltpu.CompilerParams(dimension_semantics=("arbitrary",), vmem_limit_bytes=VMEM_LIMIT),
        name="cmp_out",
    )(pad(first), pad(second), bias, w2p)
    return out[:rows, :HEAD_DIM].reshape(B, n - 1, G, HEAD_DIM)


def _cmp_attention(q, kc, vc, q_pos):
    B, Q, H, D = q.shape
    N, G = kc.shape[1], kc.shape[2]
    c_end = jnp.arange(N) * CMP_STRIDE + CMP_BLK
    valid = c_end[None, :] <= q_pos[:, None] + 1
    s = jnp.einsum('bqgnd,bcgd->bgnqc', q.reshape(B, Q, G, H // G, D), kc).astype(F32) * D ** -0.5
    p = jax.nn.softmax(jnp.where(valid, s, NEG_INF), axis=-1) * valid
    o = jnp.einsum('bgnqc,bcgd->bqgnd', p.astype(vc.dtype), vc).reshape(B, Q, H, D)
    return o, p


def _cmp_to_sel(n_cmp, n_sel):
    c0 = jnp.arange(n_cmp)[:, None] * CMP_STRIDE
    s0 = jnp.arange(n_sel)[None, :] * SEL_BLK
    ov = jnp.minimum(c0 + CMP_BLK, s0 + SEL_BLK) - jnp.maximum(c0, s0)
    return jnp.maximum(ov, 0).astype(F32) / CMP_BLK


def _selection_scores(p_cmp, q_pos, n_sel):
    imp = jnp.einsum('bgnqc,cs->bgqs', p_cmp, _cmp_to_sel(p_cmp.shape[-1], n_sel))
    blk = jnp.arange(n_sel)[None, :]
    cur = (q_pos // SEL_BLK)[:, None]
    valid = blk * SEL_BLK <= q_pos[:, None]
    forced = (blk == 0) | (blk == cur) | (blk == cur - 1)
    return jnp.where(valid, jnp.where(forced, BIG, imp), -BIG)


def _selection_mask(score):
    n_sel = score.shape[-1]
    a = score[..., :, None]
    b = score[..., None, :]
    lower = jnp.arange(n_sel)[None, :] < jnp.arange(n_sel)[:, None]
    beats = (b > a) | ((b == a) & lower)
    rank = jnp.sum(beats, axis=-1)
    return (rank < min(SEL_TOPN, n_sel)).astype(F32)


def _selected_attention_gather(q, k, v, p_cmp, q_pos):
    B, Q, H, D = q.shape
    T_all, G = k.shape[1], k.shape[2]
    n_sel = -(-T_all // SEL_BLK)
    score = _selection_scores(p_cmp, q_pos, n_sel)
    _, idx = lax.top_k(score, min(SEL_TOPN, n_sel))
    tok = (idx[..., None] * SEL_BLK + jnp.arange(SEL_BLK)).reshape(B, G, Q, -1)
    pad = ((0, 0), (0, n_sel * SEL_BLK - T_all), (0, 0), (0, 0))
    kt = jnp.pad(k, pad).transpose(0, 2, 1, 3)
    vt = jnp.pad(v, pad).transpose(0, 2, 1, 3)
    b_ix = jnp.arange(B)[:, None, None]
    g_ix = jnp.arange(G)[None, :, None]
    L = tok.shape[3]
    flat = tok.reshape(B, G, Q * L)
    ks = kt[b_ix, g_ix, flat].reshape(B, G, Q, L, D)
    vs = vt[b_ix, g_ix, flat].reshape(B, G, Q, L, D)
    s = jnp.einsum('bqgnd,bgqld->bgnql', q.reshape(B, Q, G, H // G, D), ks).astype(F32) * D ** -0.5
    mask = (tok <= q_pos[None, None, :, None])[:, :, None]
    p = jax.nn.softmax(jnp.where(mask, s, NEG_INF), axis=-1)
    return jnp.einsum('bgnql,bgqld->bqgnd', p, vs).reshape(B, Q, H, D)


def _gla_kernel(q_ref, k_ref, v_ref, gr_ref, tail_ref, wa_ref, ba_ref, g_ref, s0_ref, o_ref, sT_ref, st_ref,
                *, C, n_chunks, n_valid):
    @pl.when(pl.program_id(1) == 0)
    def _():
        st_ref[...] = s0_ref[0]

    tri = (lax.broadcasted_iota(jnp.int32, (C, C), 0) >= lax.broadcasted_iota(jnp.int32, (C, C), 1))
    tri_f = jnp.where(tri, 1.0, 0.0)
    for c in range(n_chunks):
        rows = pl.ds(c * C, C)
        za = jnp.dot(tail_ref[0, rows, :].astype(BF16), wa_ref[...], preferred_element_type=F32) + ba_ref[...]
        log_a = jax.nn.log_sigmoid(za) / GLA_TAU
        q = q_ref[0, rows, :] * GLA_DK ** -0.5
        k = k_ref[0, rows, :]
        v = v_ref[0, rows, :]
        if n_valid < C:
            live = lax.broadcasted_iota(jnp.int32, (C, 1), 0) < n_valid
            log_a = jnp.where(live, log_a, 0.0)
            q, k, v = jnp.where(live, q, 0.0), jnp.where(live, k, 0.0), jnp.where(live, v, 0.0)
        b = jnp.dot(tri_f, log_a, preferred_element_type=F32, precision=lax.Precision.HIGHEST)
        b_last = b[C - 1:C, :]
        q_dec = (q * jnp.exp(b)).astype(BF16)
        k_inv = (k * jnp.exp(-b)).astype(BF16)
        k_end = (k * jnp.exp(b_last - b)).astype(BF16)
        decay = jnp.exp(b_last)
        v = v.astype(BF16)
        gr = gr_ref[0, rows, :]
        outs = []
        for h in range(GLA_HEADS):
            kh = slice(h * GLA_DK, (h + 1) * GLA_DK)
            vh = slice(h * GLA_DV, (h + 1) * GLA_DV)
            attn = lax.dot_general(q_dec[:, kh], k_inv[:, kh], (((1,), (1,)), ((), ())), preferred_element_type=F32)
            attn = jnp.where(tri, attn, 0.0).astype(BF16)
            sT = st_ref[h]
            o = jnp.dot(attn, v[:, vh], preferred_element_type=F32)
            o = o + lax.dot_general(q_dec[:, kh], sT.astype(BF16), (((1,), (1,)), ((), ())), preferred_element_type=F32)
            st_ref[h] = decay[:, kh] * sT + lax.dot_general(v[:, vh], k_end[:, kh], (((0,), (0,)), ((), ())),
                                                            preferred_element_type=F32)
            o = o * lax.rsqrt(jnp.mean(o * o, axis=-1, keepdims=True) + EPS) * g_ref[...]
            outs.append(o * jax.nn.silu(gr[:, vh]))
        o_ref[0, rows, :] = jnp.concatenate(outs, axis=1)

    @pl.when(pl.program_id(1) == pl.num_programs(1) - 1)
    def _():
        sT_ref[0] = st_ref[...]


def _gla(proj, cols, gla_wa, gla_ba, gla_g, s0, C, n_chunks, n_valid):
    B, T, _ = proj.shape
    tt = C * n_chunks
    s0T = s0.transpose(0, 1, 3, 2)
    cq, ck, cv, cr, ca = cols
    tail0 = ca // GLA_QK * GLA_QK
    assert cq % GLA_QK == 0 and ck % GLA_QK == 0 and cv % GLA_WIDTH == 0 and cr % GLA_WIDTH == 0
    assert ca + GLA_LOWRANK <= tail0 + GLA_QK
    wa_pad = jnp.zeros((GLA_QK, GLA_QK), F32).at[ca - tail0:ca - tail0 + GLA_LOWRANK].set(gla_wa).astype(BF16)
    col_spec = lambda width, col: pl.BlockSpec((1, tt, width), lambda b, i: (b, i, col // width))
    v_spec = pl.BlockSpec((1, tt, GLA_WIDTH), lambda b, i: (b, i, 0))
    st_spec = pl.BlockSpec((1, GLA_HEADS, GLA_DV, GLA_DK), lambda b, i: (b, 0, 0, 0))
    const2 = lambda b, i: (0, 0)
    o, sT = pl.pallas_call(
        functools.partial(_gla_kernel, C=C, n_chunks=n_chunks, n_valid=n_valid),
        grid=(B, T // tt),
        in_specs=[col_spec(GLA_QK, cq), col_spec(GLA_QK, ck), col_spec(GLA_WIDTH, cv), col_spec(GLA_WIDTH, cr),
                  col_spec(GLA_QK, tail0), pl.BlockSpec((GLA_QK, GLA_QK), const2), pl.BlockSpec((1, GLA_QK), const2),
                  pl.BlockSpec((1, GLA_DV), const2), st_spec],
        out_specs=[v_spec, st_spec],
        out_shape=[jax.ShapeDtypeStruct((B, T, GLA_WIDTH), F32),
                   jax.ShapeDtypeStruct((B, GLA_HEADS, GLA_DV, GLA_DK), F32)],
        scratch_shapes=[pltpu.VMEM((GLA_HEADS, GLA_DV, GLA_DK), F32)],
        compiler_params=pltpu.CompilerParams(
            dimension_semantics=("arbitrary", "arbitrary"), vmem_limit_bytes=VMEM_LIMIT),
        name="gla",
    )(proj, proj, proj, proj, proj, wa_pad, gla_ba.reshape(1, GLA_QK), gla_g.reshape(1, GLA_DV), s0T)
    return o, sT.transpose(0, 1, 3, 2)


S5_NK = 4
S5_GPK = S5_GROUPS // S5_NK
S5_HALF = S5_GPK * S5_STATE


def _s5_kernel(u_ref, wb_ref, wc_ref, coef_ref, d_ref, gw_ref, gb_ref, h0_ref, y_ref, hfin_ref, hst_ref, xb_ref,
               *, BT, Tc):
    @pl.when(pl.program_id(0) == 0)
    def _():
        hst_ref[...] = h0_ref[...]

    u = u_ref[...]
    ys = []
    for k in range(S5_NK):
        bu = jnp.dot(u[:, k * LANE:(k + 1) * LANE].astype(BF16), wb_ref[k], preferred_element_type=F32)
        bre, bim = bu[:, :S5_HALF], bu[:, S5_HALF:]
        ar, ai = coef_ref[k, 0:1, :], coef_ref[k, 1:2, :]
        cr, ci = coef_ref[k, 2:3, :], coef_ref[k, 3:4, :]
        xb_ref[:, :S5_HALF] = cr * bre - ci * bim
        xb_ref[:, S5_HALF:] = cr * bim + ci * bre

        def step(t, carry):
            hr, hi = carry
            rows = pl.ds(pl.multiple_of(t * BT, BT), BT)
            hr2 = ar * hr - ai * hi + xb_ref[rows, :S5_HALF]
            hi2 = ar * hi + ai * hr + xb_ref[rows, S5_HALF:]
            xb_ref[rows, :S5_HALF] = hr2
            xb_ref[rows, S5_HALF:] = hi2
            return hr2, hi2

        hr, hi = lax.fori_loop(0, Tc, step, (hst_ref[k, :, :S5_HALF], hst_ref[k, :, S5_HALF:]),
                               unroll=min(Tc, 8))
        hst_ref[k, :, :S5_HALF] = hr
        hst_ref[k, :, S5_HALF:] = hi
        ys.append(jnp.dot(xb_ref[...].astype(BF16), wc_ref[k], preferred_element_type=F32))
    y = jax.nn.gelu(jnp.concatenate(ys, axis=1) + d_ref[...] * u)
    z = jnp.dot(y.astype(BF16), gw_ref[...].astype(BF16), preferred_element_type=F32) + gb_ref[...]
    y_ref[...] = y * jax.nn.sigmoid(z)

    @pl.when(pl.program_id(0) == pl.num_programs(0) - 1)
    def _():
        hfin_ref[...] = hst_ref[...]


def _s5(u, a_re, a_im, b_re, b_im, c_re, c_im, d, log_dt, glu_w, glu_b, h0, Tc):
    B, T, _ = u.shape
    dt = jnp.exp(log_dt)[:, None]
    mag = jnp.exp(a_re * dt)
    ab_re, ab_im = mag * jnp.cos(a_im * dt), mag * jnp.sin(a_im * dt)
    den = a_re * a_re + a_im * a_im
    coef_re = ((ab_re - 1) * a_re + ab_im * a_im) / den
    coef_im = (ab_im * a_re - (ab_re - 1) * a_im) / den
    coefs = jnp.stack([ab_re, ab_im, coef_re, coef_im], axis=0).reshape(4, S5_NK, S5_HALF).transpose(1, 0, 2)
    eye = jnp.eye(S5_GPK, dtype=F32)

    def in_weights(bm):
        bk = bm.reshape(S5_NK, S5_GPK, S5_STATE, S5_CH)
        return jnp.einsum('kgpc,gh->kgchp', bk, eye).reshape(S5_NK, S5_GPK * S5_CH, S5_HALF)

    def out_weights(cm):
        ck = cm.reshape(S5_NK, S5_GPK, S5_CH, S5_STATE)
        return jnp.einsum('kgcp,gh->kgphc', ck, eye).reshape(S5_NK, S5_HALF, S5_GPK * S5_CH)

    wb = jnp.concatenate([in_weights(b_re), in_weights(b_im)], axis=2).astype(BF16)
    wc = jnp.concatenate([out_weights(c_re), -out_weights(c_im)], axis=1).astype(BF16)
    if h0 is None:
        hs0 = jnp.zeros((S5_NK, B, 2 * S5_HALF), F32)
    else:
        hs0 = h0.reshape(B, S5_NK, S5_HALF, 2).transpose(1, 0, 3, 2).reshape(S5_NK, B, 2 * S5_HALF)
    ut = u.transpose(1, 0, 2).reshape(T * B, S5_WIDTH)
    rows = Tc * B
    const2 = lambda i: (0, 0)
    const3 = lambda i: (0, 0, 0)
    y, hfin = pl.pallas_call(
        functools.partial(_s5_kernel, BT=B, Tc=Tc),
        grid=(T // Tc,),
        in_specs=[pl.BlockSpec((rows, S5_WIDTH), lambda i: (i, 0)),
                  pl.BlockSpec(wb.shape, const3), pl.BlockSpec(wc.shape, const3), pl.BlockSpec(coefs.shape, const3),
                  pl.BlockSpec((1, S5_WIDTH), const2), pl.BlockSpec((S5_WIDTH, S5_WIDTH), const2),
                  pl.BlockSpec((1, S5_WIDTH), const2), pl.BlockSpec(hs0.shape, const3)],
        out_specs=[pl.BlockSpec((rows, S5_WIDTH), lambda i: (i, 0)), pl.BlockSpec(hs0.shape, const3)],
        out_shape=[jax.ShapeDtypeStruct((T * B, S5_WIDTH), F32), jax.ShapeDtypeStruct(hs0.shape, F32)],
        scratch_shapes=[pltpu.VMEM(hs0.shape, F32), pltpu.VMEM((rows, 2 * S5_HALF), F32)],
        compiler_params=pltpu.CompilerParams(dimension_semantics=("arbitrary",), vmem_limit_bytes=VMEM_LIMIT),
        name="s5",
    )(ut, wb, wc, coefs, d.reshape(1, S5_WIDTH), glu_w, glu_b.reshape(1, S5_WIDTH), hs0)
    y = y.reshape(T, B, S5_WIDTH).transpose(1, 0, 2)
    hfin = hfin.reshape(S5_NK, B, 2, S5_HALF).transpose(1, 0, 3, 2).reshape(B, S5_GROUPS, S5_STATE, 2)
    return y, hfin


AB_ORDER = (0, 5, 6, 1, 3, 4, 2, 7)
AB_START = dict(zip(AB_ORDER, np.cumsum([0] + [AB_SPLITS[s] for s in AB_ORDER[:-1]]).tolist()))
AB_PERM = np.concatenate([np.arange(AB_SPLITS[s]) + sum(AB_SPLITS[:s]) for s in AB_ORDER])


def _mixer_ab(x, norm, w_in, w_out, cmp_pe, cmp_w1, cmp_w2, gla_wa, gla_ba, gla_g, past):
    B, T, _ = x.shape
    projp = _norm_proj(x, norm[0], norm[1], norm[2], w_in[:, AB_PERM])
    seg = lambda s: projp[..., AB_START[s]:AB_START[s] + AB_SPLITS[s]]
    q, kv6, gate_logit = seg(0), seg(1), seg(2)
    kv6 = kv6.reshape(B, T, 6, NSA_KV_HEADS, HEAD_DIM)
    new_nsa, new_win = kv6[:, :, :4], kv6[:, :, 4:]
    s0 = jnp.zeros((B, GLA_HEADS, GLA_DK, GLA_DV), F32)
    if past is None:
        t0 = 0
        kv_full, kv_win = new_nsa, new_win
        cmp_proj = [_cmp_chunk_proj(new_nsa[:, :T // CMP_STRIDE * CMP_STRIDE, c], cmp_w1[c]) for c in range(2)]
    else:
        pool, page_table, win_buf, s0 = past
        pool_t = pool.transpose(0, 2, 3, 4, 1)
        n_pages = page_table.shape[1]
        t0 = n_pages * PAGE_SIZE
        assert PAGE_SIZE % CMP_STRIDE == 0 and T < CMP_STRIDE
        pages_proj = _cmp_pages_proj(pool_t, cmp_w1)
        cmp_proj = [pages_proj[c][page_table].reshape(
            B, n_pages * (PAGE_SIZE // CMP_STRIDE), NSA_KV_HEADS, 2 * CMP_HID) for c in range(2)]
        kv_win = jnp.concatenate([win_buf, new_win], axis=1)
    q_pos = t0 + jnp.arange(T)
    q4 = q.reshape(B, T, NSA_HEADS, HEAD_DIM)
    kc = _cmp_finish(cmp_proj[0], cmp_pe[0], cmp_w1[0], cmp_w2[0])
    vc = _cmp_finish(cmp_proj[1], cmp_pe[1], cmp_w1[1], cmp_w2[1])
    o_cmp, p_cmp = _cmp_attention(q4, kc, vc, q_pos)
    if past is None:
        n_sel = -(-T // SEL_BLK)
        sel = _selection_mask(_selection_scores(p_cmp, q_pos, n_sel))
        kv_col = lambda kind: AB_START[1] + kind * NSA_KVW
        hpg = NSA_HEADS // NSA_KV_HEADS
        o_slc = _attn(projp, 0, kv_col(2), kv_col(3), G=NSA_KV_HEADS, R=hpg, tq=128, tk=512, sel=sel)
        o_win = _attn(projp, 0, kv_col(4), kv_col(5), G=NSA_KV_HEADS, R=hpg, tq=128, tk=512, window=WINDOW)
        o_slc = o_slc.reshape(B, T, NSA_HEADS, HEAD_DIM)
        o_win = o_win.reshape(B, T, NSA_HEADS, HEAD_DIM)
        win_state = kv_win[:, -min(WINDOW, T):]
    else:
        assert t0 % SEL_BLK == 0 and T <= SEL_BLK
        n_sel = t0 // SEL_BLK + 1
        hpg = NSA_HEADS // NSA_KV_HEADS
        sel = _selection_mask(_selection_scores(p_cmp, q_pos, n_sel))
        sel_rows = jnp.repeat(sel.transpose(0, 2, 1, 3), hpg, axis=2).reshape(B, T * NSA_HEADS, n_sel)
        row_t = jnp.repeat(jnp.arange(T), NSA_HEADS)
        new_j = jnp.arange(LANE)
        bias_pages = jnp.where(jnp.repeat(sel_rows[:, :, :n_sel - 1], SEL_BLK, axis=2) > 0.5, 0.0, NEG_INF)
        causal_new = (new_j[None, :] <= row_t[:, None]) & (new_j[None, :] < T)
        bias_new = jnp.where((sel_rows[:, :, n_sel - 1:] > 0.5) & causal_new[None], 0.0, NEG_INF)
        pad_rows = lambda a, n: jnp.pad(a.reshape(B, a.shape[1], NSA_KVW), ((0, 0), (0, n - a.shape[1]), (0, 0)))
        qblk = _block_diag_queries(q, NSA_KV_HEADS)
        o_slc = _decode_attn(qblk, pad_rows(new_nsa[:, :, 2], LANE), pad_rows(new_nsa[:, :, 3], LANE), bias_new,
                             pool_t=pool_t, page_table=page_table, k_kind=2, v_kind=3, bias_pages=bias_pages)
        o_slc = _own_head_columns(o_slc, T, NSA_KV_HEADS).reshape(B, T, NSA_HEADS, HEAD_DIM)
        wb = win_buf.shape[1]
        n_win = _round_up(wb + T, LANE)
        k_pos = t0 - wb + jnp.arange(n_win)
        in_win = _window_mask(q_pos, k_pos) & (jnp.arange(n_win) < wb + T)[None, :]
        bias_win = jnp.where(in_win, 0.0, NEG_INF)[row_t][None]
        o_win = _decode_attn(qblk, pad_rows(kv_win[:, :, 0], n_win), pad_rows(kv_win[:, :, 1], n_win), bias_win)
        o_win = _own_head_columns(o_win, T, NSA_KV_HEADS).reshape(B, T, NSA_HEADS, HEAD_DIM)
        win_state = kv_win[:, -wb:]
    g = jax.nn.sigmoid(gate_logit).reshape(B, T, NSA_HEADS, 3, 1)
    o_nsa = (g[:, :, :, 0] * o_cmp + g[:, :, :, 1] * o_slc + g[:, :, :, 2] * o_win).reshape(B, T, NSA_WIDTH)
    gla_cols = (AB_START[3], AB_START[4], AB_START[5], AB_START[6], AB_START[7])
    if T % GLA_CHUNK == 0:
        o_gla, s_new = _gla(projp, gla_cols, gla_wa, gla_ba, gla_g, s0, GLA_CHUNK, 4, GLA_CHUNK)
    else:
        Tp = _round_up(T, 2 * SUBLANE)
        o_gla, s_new = _gla(jnp.pad(projp, ((0, 0), (0, Tp - T), (0, 0))), gla_cols, gla_wa, gla_ba, gla_g, s0,
                            Tp, 1, T)
        o_gla = o_gla[:, :T]
    return _out_proj(o_nsa, o_gla, x, norm[3], w_out), (new_nsa, win_state, s_new)


def _mixer_cd(x, norm, w_in, w_out, b_f, a_re, a_im, b_re, b_im, c_re, c_im, d, log_dt, glu_w, glu_b, past):
    B, T, _ = x.shape
    projp = _norm_proj(x, norm[0], norm[1], norm[2], w_in)
    q, k, v, f, u = _split(projp[..., :sum(CD_SPLITS)], CD_SPLITS)
    log_f = jax.nn.log_sigmoid(f + b_f)
    new_kv = jnp.stack([k.reshape(B, T, FOX_HEADS, HEAD_DIM), v.reshape(B, T, FOX_HEADS, HEAD_DIM)], axis=2)
    h0 = None
    if past is None:
        F = jnp.cumsum(log_f, axis=1)
        o_fox = _attn(projp, 0, FOX_WIDTH, 2 * FOX_WIDTH, G=FOX_HEADS, R=1, tq=256, tk=512,
                      fq=F, fk=F.transpose(0, 2, 1))
    else:
        pool_kv, pool_lf, page_table, h0 = past
        t0 = page_table.shape[1] * PAGE_SIZE
        lf_all = jnp.concatenate([_gather_pages(pool_lf, page_table), log_f], axis=1)
        F = jnp.cumsum(lf_all, axis=1)
        Fq = F[:, t0:]
        fq_rows = Fq.reshape(B, T * FOX_HEADS, 1)
        fk_rows = jnp.tile(F.transpose(0, 2, 1), (1, T, 1))
        row_t = jnp.repeat(jnp.arange(T), FOX_HEADS)
        new_j = jnp.arange(LANE)
        bias_pages = fq_rows - fk_rows[:, :, :t0]
        fk_new = jnp.pad(fk_rows[:, :, t0:], ((0, 0), (0, 0), (0, LANE - T)))
        causal_new = (new_j[None, :] <= row_t[:, None]) & (new_j[None, :] < T)
        bias_new = jnp.where(causal_new[None], fq_rows - fk_new, NEG_INF)
        pad_rows = lambda a: jnp.pad(a, ((0, 0), (0, LANE - T), (0, 0)))
        o = _decode_attn(_block_diag_queries(q, FOX_HEADS), pad_rows(k), pad_rows(v), bias_new,
                         pool_t=pool_kv.transpose(0, 2, 3, 4, 1),
                         page_table=page_table, k_kind=0, v_kind=1, bias_pages=bias_pages)
        o_fox = _own_head_columns(o, T, FOX_HEADS)
    y_s5, s5_state = _s5(u, a_re, a_im, b_re, b_im, c_re, c_im, d, log_dt, glu_w, glu_b, h0, min(T, 128))
    return _out_proj(o_fox, y_s5, x, norm[3], w_out), (new_kv, log_f, s5_state)


def kernel(x_prompt, x_sample, cache_ab_nsa_kv, cache_ab_win_kv, state_ab_gla, cache_cd_fox_kv, cache_cd_fox_logf, state_cd_s5, page_table, c_prompt, c_sample, ada_w, ada_b, norm_g, ab_w_in, ab_w_out, nsa_cmp_pe, nsa_cmp_w1, nsa_cmp_w2, gla_w_alpha, gla_b_alpha, gla_norm_g, cd_w_in, cd_w_out, fox_b_f, s5_a_re, s5_a_im, s5_b_re, s5_b_im, s5_c_re, s5_c_im, s5_d, s5_log_dt, s5_glu_w, s5_glu_b, router_w, router_b, moe_w1, moe_b1, moe_w2, moe_b2, final_norm_g):
    xs = [x_prompt, x_sample]
    cs = [c_prompt, c_sample]
    new = [{}, {}]
    n_prompt = x_prompt.shape[0] * x_prompt.shape[1]
    for l in range(DEPTH):
        j = l // 2
        mods = [jnp.split(_mm(jax.nn.silu(c), ada_w[l], ada_b[l]), 6, axis=-1) for c in cs]
        for grp in range(2):
            x = xs[grp]
            sh1, sc1, g1 = mods[grp][:3]
            norm = (norm_g[l, 0], sh1, sc1, g1)
            if l % 2 == 0:
                past = None
                if grp == 1:
                    past = (cache_ab_nsa_kv[j], page_table, cache_ab_win_kv[j], state_ab_gla[j])
                xs[grp], st = _mixer_ab(x, norm, ab_w_in[j], ab_w_out[j], nsa_cmp_pe[j], nsa_cmp_w1[j],
                                        nsa_cmp_w2[j], gla_w_alpha[j], gla_b_alpha[j], gla_norm_g[j], past)
                names = ('nsa_kv', 'win_kv', 'gla')
            else:
                past = None
                if grp == 1:
                    past = (cache_cd_fox_kv[j], cache_cd_fox_logf[j], page_table, state_cd_s5[j])
                xs[grp], st = _mixer_cd(x, norm, cd_w_in[j], cd_w_out[j], fox_b_f[j], s5_a_re[j], s5_a_im[j],
                                        s5_b_re[j], s5_b_im[j], s5_c_re[j], s5_c_im[j], s5_d[j], s5_log_dt[j],
                                        s5_glu_w[j], s5_glu_b[j], past)
                names = ('fox_kv', 'fox_logf', 's5')
            for name, s in zip(names, st):
                new[grp].setdefault(name, []).append(s)
        hn2 = [_modulated_norm(xs[grp], norm_g[l, 1], mods[grp][3], mods[grp][4]).reshape(-1, D_MODEL)
               for grp in range(2)]
        ym = _moe(jnp.concatenate(hn2, axis=0), router_w[l], router_b[l], moe_w1[l], moe_b1[l], moe_w2[l], moe_b2[l])
        yms = [ym[:n_prompt], ym[n_prompt:]]
        for grp in range(2):
            xs[grp] = xs[grp] + mods[grp][5][:, None, :] * yms[grp].reshape(xs[grp].shape)
    ys = [_rmsnorm(x, final_norm_g) for x in xs]
    names = ('nsa_kv', 'win_kv', 'gla', 'fox_kv', 'fox_logf', 's5')
    outs = [jnp.stack(new[grp][name]) for grp in range(2) for name in names]
    return (ys[0], ys[1]) + tuple(outs)
```

```python
import functools
import math

import numpy as np
import jax
import jax.numpy as jnp
from jax import lax
from jax.experimental import pallas as pl
from jax.experimental.pallas import tpu as pltpu

D_MODEL = 1024
DEPTH = 2
PAGE_SIZE = 128
HEAD_DIM = 64
NSA_HEADS = 8
NSA_KV_HEADS = 2
CMP_BLK = 32
CMP_STRIDE = 16
SEL_BLK = 64
SEL_TOPN = 16
WINDOW = 512
GLA_HEADS = 4
GLA_DK = 64
GLA_DV = 128
GLA_LOWRANK = 16
GLA_TAU = 16.0
GLA_CHUNK = 64
FOX_HEADS = 8
S5_GROUPS = 32
S5_CH = 16
S5_STATE = 64
N_EXPERTS = 32
TOP_K = 4
D_FF = D_MODEL
SWIGLU_LIMIT = 7.0
SWIGLU_ALPHA = 1.702
EPS = 1e-6
NEG_INF = -1e30
BIG = 1e9
NSA_WIDTH = NSA_HEADS * HEAD_DIM
NSA_KVW = NSA_KV_HEADS * HEAD_DIM
GLA_QK = GLA_HEADS * GLA_DK
GLA_WIDTH = GLA_HEADS * GLA_DV
FOX_WIDTH = FOX_HEADS * HEAD_DIM
S5_WIDTH = S5_GROUPS * S5_CH
AB_SPLITS = (NSA_WIDTH, 6 * NSA_KVW, 3 * NSA_HEADS, GLA_QK, GLA_QK, GLA_WIDTH, GLA_WIDTH, GLA_LOWRANK)
CD_SPLITS = (FOX_WIDTH, FOX_WIDTH, FOX_WIDTH, FOX_HEADS, S5_WIDTH)

LANE = 128
SUBLANE = 8
VMEM_LIMIT = 56 * 1024 * 1024
MOE_TM = 256

F32 = jnp.float32
BF16 = jnp.bfloat16


def _round_up(n, m):
    return -(-n // m) * m


def _pick_tile(n, candidates):
    for c in candidates:
        if n % c == 0:
            return c
    return n


def _mm_kernel(x_ref, w_ref, b_ref, o_ref, wb_ref, *, act, precise):
    if precise:
        acc = jnp.dot(x_ref[...], w_ref[...], preferred_element_type=F32, precision=lax.Precision.HIGHEST)
    else:
        @pl.when(pl.program_id(1) == 0)
        def _():
            wb_ref[...] = w_ref[...].astype(BF16)

        acc = jnp.dot(x_ref[...].astype(BF16), wb_ref[...], preferred_element_type=F32)
    acc = acc + b_ref[...]
    if act == 'gelu':
        acc = jax.nn.gelu(acc)
    o_ref[...] = acc.astype(o_ref.dtype)


def _mm(x, w, b=None, act=None, out_dtype=F32, precise=False, keep_cols=False):
    M, K = x.shape
    N = w.shape[1]
    Np = _round_up(N, 2 * LANE) if N > 2 * LANE else _round_up(N, LANE)
    Mp = _round_up(M, 512) if M > 256 else _round_up(M, SUBLANE)
    tm = min(Mp, 512)
    tn = _pick_tile(Np, (512, 256, 128))
    if Np != N:
        w = jnp.pad(w, ((0, 0), (0, Np - N)))
    if b is None:
        b = jnp.zeros((N,), F32)
    b = jnp.pad(b.astype(F32), (0, Np - N)).reshape(1, Np)
    if Mp != M:
        x = jnp.pad(x, ((0, Mp - M), (0, 0)))
    out = pl.pallas_call(
        functools.partial(_mm_kernel, act=act, precise=precise),
        grid=(Np // tn, Mp // tm),
        in_specs=[pl.BlockSpec((tm, K), lambda j, i: (i, 0)),
                  pl.BlockSpec((K, tn), lambda j, i: (0, j)),
                  pl.BlockSpec((1, tn), lambda j, i: (0, j))],
        out_specs=pl.BlockSpec((tm, tn), lambda j, i: (i, j)),
        out_shape=jax.ShapeDtypeStruct((Mp, Np), out_dtype),
        scratch_shapes=[pltpu.VMEM((K, tn), BF16)],
        compiler_params=pltpu.CompilerParams(
            dimension_semantics=("arbitrary", "arbitrary"), vmem_limit_bytes=VMEM_LIMIT),
        name="mm",
    )(x, w, b)
    return out[:M] if keep_cols else out[:M, :N]


PROJ_TM = 512


def _norm_proj_kernel(x_ref, g_ref, sh_ref, sc_ref, w_ref, *rest):
    x = x_ref[0]
    y = x * lax.rsqrt(jnp.mean(x * x, axis=-1, keepdims=True) + EPS) * g_ref[...]
    hn = (y * (1 + sc_ref[0]) + sh_ref[0]).astype(BF16)
    if len(rest) == 1:
        o_ref, = rest
    else:
        wt_ref, o_ref, ot_ref = rest
        ot_ref[0] = lax.dot_general(wt_ref[...], hn, (((1,), (1,)), ((), ())), preferred_element_type=F32)
    o_ref[0] = jnp.dot(hn, w_ref[...], preferred_element_type=F32)


def _rows_and_mods(x, mods):
    B, T, D = x.shape
    if T % PROJ_TM == 0:
        return x, [m.reshape(B, 1, D) for m in mods]
    assert (B * T) % PROJ_TM == 0 or B * T <= PROJ_TM
    return x.reshape(1, B * T, D), [jnp.repeat(m, T, axis=0).reshape(1, B * T, D) for m in mods]


def _mod_spec(m, tm):
    if m.shape[1] == 1:
        return pl.BlockSpec((1, 1, m.shape[2]), lambda b, i: (b, 0, 0))
    return pl.BlockSpec((1, tm, m.shape[2]), lambda b, i: (b, i, 0))


def _norm_proj(x, g, shift, scale, w, t_cols=None):
    B, T, D = x.shape
    N = w.shape[1]
    Np = _round_up(N, 2 * LANE)
    wb = jnp.pad(w, ((0, 0), (0, Np - N))).astype(BF16)
    xr, (sh, sc) = _rows_and_mods(x, [shift, scale])
    Bm, Tm, _ = xr.shape
    tm = min(PROJ_TM, Tm)
    const2 = lambda b, i: (0, 0)
    in_specs = [pl.BlockSpec((1, tm, D), lambda b, i: (b, i, 0)), pl.BlockSpec((1, D), const2),
                _mod_spec(sh, tm), _mod_spec(sc, tm), pl.BlockSpec((D, Np), const2)]
    args = [xr, g.reshape(1, D), sh, sc, wb]
    out_specs = [pl.BlockSpec((1, tm, Np), lambda b, i: (b, i, 0))]
    out_shape = [jax.ShapeDtypeStruct((Bm, Tm, Np), F32)]
    if t_cols is not None:
        assert Bm == B
        start, width = t_cols
        in_specs.append(pl.BlockSpec((width, D), const2))
        args.append(w[:, start:start + width].T.astype(BF16))
        out_specs.append(pl.BlockSpec((1, width, tm), lambda b, i: (b, 0, i)))
        out_shape.append(jax.ShapeDtypeStruct((B, width, T), F32))
    outs = pl.pallas_call(
        _norm_proj_kernel,
        grid=(Bm, Tm // tm),
        in_specs=in_specs,
        out_specs=out_specs,
        out_shape=out_shape,
        compiler_params=pltpu.CompilerParams(
            dimension_semantics=("arbitrary", "arbitrary"), vmem_limit_bytes=VMEM_LIMIT),
        name="norm_proj",
    )(*args)
    if t_cols is None:
        return outs[0].reshape(B, T, Np)
    return outs[0].reshape(B, T, Np), outs[1]


def _out_proj_kernel(a_ref, b_ref, x_ref, g_ref, w_ref, o_ref):
    wa = a_ref.shape[2]
    y = jnp.dot(a_ref[0].astype(BF16), w_ref[:wa, :], preferred_element_type=F32)
    y = y + jnp.dot(b_ref[0].astype(BF16), w_ref[wa:, :], preferred_element_type=F32)
    o_ref[0] = x_ref[0] + g_ref[0] * y


def _out_proj(a, b, x, gate, w):
    B, T, D = x.shape
    wa, wb_ = a.shape[2], b.shape[2]
    xr, (g,) = _rows_and_mods(x, [gate])
    Bm, Tm, _ = xr.shape
    tm = min(PROJ_TM, Tm)
    row = lambda b_, i: (b_, i, 0)
    out = pl.pallas_call(
        _out_proj_kernel,
        grid=(Bm, Tm // tm),
        in_specs=[pl.BlockSpec((1, tm, wa), row), pl.BlockSpec((1, tm, wb_), row), pl.BlockSpec((1, tm, D), row),
                  _mod_spec(g, tm), pl.BlockSpec((wa + wb_, D), lambda b_, i: (0, 0))],
        out_specs=pl.BlockSpec((1, tm, D), row),
        out_shape=jax.ShapeDtypeStruct((Bm, Tm, D), F32),
        compiler_params=pltpu.CompilerParams(
            dimension_semantics=("arbitrary", "arbitrary"), vmem_limit_bytes=VMEM_LIMIT),
        name="out_proj",
    )(a.reshape(Bm, Tm, wa), b.reshape(Bm, Tm, wb_), xr, g, w.astype(BF16))
    return out.reshape(B, T, D)


def _flash_kernel(*refs, tq, tk, R, window, use_bias, use_sel):
    q_ref, k_ref, v_ref = refs[:3]
    n = 3
    if use_bias:
        qb_ref, kb_ref = refs[n:n + 2]
        n += 2
    if use_sel:
        sel_ref = refs[n]
        n += 1
    o_ref = refs[n]
    i = pl.program_id(2)
    rows = R * tq
    q = q_ref[0, 0].reshape(rows, HEAD_DIM)
    q_pos = i * tq + jnp.bitwise_and(lax.broadcasted_iota(jnp.int32, (rows, 1), 0), tq - 1)
    hi = ((i + 1) * tq + tk - 1) // tk
    lo = jnp.maximum(i * tq - window + 1, 0) // tk if window else 0
    if use_bias:
        qb = qb_ref[0, 0]
        qb = jnp.concatenate([qb] * (tk // LANE), axis=1)
    if use_sel:
        sel = sel_ref[0, 0].astype(BF16)
        n_sel = sel.shape[1]

    def body(j, carry):
        m, l, acc = carry
        start = pl.multiple_of(j * tk, tk)
        k = k_ref[0, 0, pl.ds(start, tk), :]
        v = v_ref[0, 0, pl.ds(start, tk), :]
        s = lax.dot_general(q, k, (((1,), (1,)), ((), ())), preferred_element_type=F32)
        k_pos = j * tk + lax.broadcasted_iota(jnp.int32, (1, tk), 1)
        if use_bias:
            s = s + qb - kb_ref[0, 0, j]
        s = jnp.where(k_pos <= q_pos, s, NEG_INF)
        if window:
            s = jnp.where(q_pos - k_pos < window, s, NEG_INF)
        if use_sel:
            blk_of_key = (j * tk + lax.broadcasted_iota(jnp.int32, (n_sel, tk), 1)) // SEL_BLK
            expand = jnp.where(blk_of_key == lax.broadcasted_iota(jnp.int32, (n_sel, tk), 0), 1.0, 0.0).astype(BF16)
            chosen = jnp.dot(sel, expand, preferred_element_type=F32)
            chosen = jnp.concatenate([chosen] * R, axis=0)
            s = jnp.where(chosen > 0.5, s, NEG_INF)
        m_new = jnp.maximum(m, jnp.max(s, axis=1, keepdims=True))
        p = jnp.where(s > 0.5 * NEG_INF, jnp.exp(s - m_new), 0.0)
        alpha = jnp.exp(m - m_new)
        l = alpha * l + jnp.sum(p, axis=1, keepdims=True)
        acc = alpha * acc + jnp.dot(p.astype(BF16), v, preferred_element_type=F32)
        return m_new, l, acc

    m0 = jnp.full((rows, 1), NEG_INF, F32)
    l0 = jnp.zeros((rows, 1), F32)
    a0 = jnp.zeros((rows, HEAD_DIM), F32)
    m, l, acc = lax.fori_loop(lo, hi, body, (m0, l0, a0))
    o_ref[0, 0] = (acc / l).reshape(R, tq, HEAD_DIM)


def _flash(q, k, v, *, tq, tk, window=0, qbias=None, kbias=None, sel=None):
    B, G, R, T, _ = q.shape
    grid = (B, G, T // tq)
    in_specs = [pl.BlockSpec((1, 1, R, tq, HEAD_DIM), lambda b, g, i: (b, g, 0, i, 0)),
                pl.BlockSpec((1, 1, T, HEAD_DIM), lambda b, g, i: (b, g, 0, 0)),
                pl.BlockSpec((1, 1, T, HEAD_DIM), lambda b, g, i: (b, g, 0, 0))]
    args = [q, k, v]
    if qbias is not None:
        in_specs += [pl.BlockSpec((1, 1, tq, LANE), lambda b, g, i: (b, g, i, 0)),
                     pl.BlockSpec((1, 1, T // tk, 1, tk), lambda b, g, i: (b, g, 0, 0, 0))]
        args += [qbias, kbias.reshape(B, G, T // tk, 1, tk)]
    if sel is not None:
        in_specs += [pl.BlockSpec((1, 1, tq, sel.shape[-1]), lambda b, g, i: (b, g, i, 0))]
        args += [sel]
    return pl.pallas_call(
        functools.partial(_flash_kernel, tq=tq, tk=tk, R=R, window=window,
                          use_bias=qbias is not None, use_sel=sel is not None),
        grid=grid,
        in_specs=in_specs,
        out_specs=pl.BlockSpec((1, 1, R, tq, HEAD_DIM), lambda b, g, i: (b, g, 0, i, 0)),
        out_shape=jax.ShapeDtypeStruct((B, G, R, T, HEAD_DIM), F32),
        compiler_params=pltpu.CompilerParams(
            dimension_semantics=("arbitrary", "arbitrary", "arbitrary"), vmem_limit_bytes=VMEM_LIMIT),
        name="flash",
    )(*args)


def _attn_kernel(*refs, tq, tk, G, R, window, use_bias, use_sel):
    q_ref, k_ref, v_ref = refs[:3]
    n = 3
    if use_bias:
        fq_ref, fk_ref = refs[n:n + 2]
        n += 2
    if use_sel:
        sel_ref = refs[n]
        n += 1
    o_ref, kb_ref, vb_ref = refs[n:n + 3]
    i = pl.program_id(1)

    @pl.when(i == 0)
    def _():
        kb_ref[...] = k_ref[0].astype(BF16)
        vb_ref[...] = v_ref[0].astype(BF16)

    rows = R * tq
    q_pos = i * tq + jnp.bitwise_and(lax.broadcasted_iota(jnp.int32, (rows, 1), 0), tq - 1)
    hi = ((i + 1) * tq + tk - 1) // tk
    lo = jnp.maximum(i * tq - window + 1, 0) // tk if window else 0
    outs = []
    for g in range(G):
        cols = slice(g * HEAD_DIM, (g + 1) * HEAD_DIM)
        qg = q_ref[0, :, g * R * HEAD_DIM:(g + 1) * R * HEAD_DIM] * HEAD_DIM ** -0.5
        q = jnp.concatenate([qg[:, r * HEAD_DIM:(r + 1) * HEAD_DIM] for r in range(R)], axis=0).astype(BF16)
        if use_bias:
            fq = fq_ref[0, :, g:g + 1]
        if use_sel:
            sel = sel_ref[0, g].astype(BF16)
            n_sel = sel.shape[1]

        def body(j, carry):
            m, l, acc = carry
            start = pl.multiple_of(j * tk, tk)
            k = kb_ref[pl.ds(start, tk), cols]
            v = vb_ref[pl.ds(start, tk), cols]
            s = lax.dot_general(q, k, (((1,), (1,)), ((), ())), preferred_element_type=F32)
            k_pos = j * tk + lax.broadcasted_iota(jnp.int32, (1, tk), 1)
            if use_bias:
                s = s + fq - fk_ref[0, g, j]
            s = jnp.where(k_pos <= q_pos, s, NEG_INF)
            if window:
                s = jnp.where(q_pos - k_pos < window, s, NEG_INF)
            if use_sel:
                blk_of_key = (j * tk + lax.broadcasted_iota(jnp.int32, (n_sel, tk), 1)) // SEL_BLK
                expand = jnp.where(blk_of_key == lax.broadcasted_iota(jnp.int32, (n_sel, tk), 0), 1.0, 0.0)
                chosen = jnp.dot(sel, expand.astype(BF16), preferred_element_type=F32)
                chosen = jnp.concatenate([chosen] * R, axis=0)
                s = jnp.where(chosen > 0.5, s, NEG_INF)
            m_new = jnp.maximum(m, jnp.max(s, axis=1, keepdims=True))
            p = jnp.where(s > 0.5 * NEG_INF, jnp.exp(s - m_new), 0.0)
            alpha = jnp.exp(m - m_new)
            l = alpha * l + jnp.sum(p, axis=1, keepdims=True)
            acc = alpha * acc + jnp.dot(p.astype(BF16), v, preferred_element_type=F32)
            return m_new, l, acc

        m0 = jnp.full((rows, 1), NEG_INF, F32)
        l0 = jnp.zeros((rows, 1), F32)
        a0 = jnp.zeros((rows, HEAD_DIM), F32)
        m, l, acc = lax.fori_loop(lo, hi, body, (m0, l0, a0))
        o = acc / l
        outs += [o[r * tq:(r + 1) * tq] for r in range(R)]
    o_ref[0] = jnp.concatenate(outs, axis=1)


def _attn(proj, q_col, k_col, v_col, *, G, R, tq, tk, window=0, fq=None, fk=None, sel=None):
    B, T, _ = proj.shape
    qw, kw = G * R * HEAD_DIM, G * HEAD_DIM
    assert q_col % qw == 0 and k_col % kw == 0 and v_col % kw == 0
    in_specs = [pl.BlockSpec((1, tq, qw), lambda b, i: (b, i, q_col // qw)),
                pl.BlockSpec((1, T, kw), lambda b, i: (b, 0, k_col // kw)),
                pl.BlockSpec((1, T, kw), lambda b, i: (b, 0, v_col // kw))]
    args = [proj, proj, proj]
    if fq is not None:
        in_specs += [pl.BlockSpec((1, tq, G), lambda b, i: (b, i, 0)),
                     pl.BlockSpec((1, G, T // tk, 1, tk), lambda b, i: (b, 0, 0, 0, 0))]
        args += [fq, fk.reshape(B, G, T // tk, 1, tk)]
    if sel is not None:
        in_specs += [pl.BlockSpec((1, G, tq, sel.shape[-1]), lambda b, i: (b, 0, i, 0))]
        args += [sel]
    return pl.pallas_call(
        functools.partial(_attn_kernel, tq=tq, tk=tk, G=G, R=R, window=window,
                          use_bias=fq is not None, use_sel=sel is not None),
        grid=(B, T // tq),
        in_specs=in_specs,
        out_specs=pl.BlockSpec((1, tq, qw), lambda b, i: (b, i, 0)),
        out_shape=jax.ShapeDtypeStruct((B, T, qw), F32),
        scratch_shapes=[pltpu.VMEM((T, kw), BF16), pltpu.VMEM((T, kw), BF16)],
        compiler_params=pltpu.CompilerParams(
            dimension_semantics=("arbitrary", "arbitrary"), vmem_limit_bytes=VMEM_LIMIT),
        name="attn",
    )(*args)


def _decode_attn_kernel(*refs, n_pages):
    pt_ref, q_ref = refs[0], refs[1]
    del pt_ref
    k_refs = refs[2:2 + n_pages]
    v_refs = refs[2 + n_pages:2 + 2 * n_pages]
    n = 2 + 2 * n_pages
    if n_pages:
        bp_ref = refs[n]
        n += 1
    ke_ref, ve_ref, be_ref, o_ref = refs[n:n + 4]
    nt = (((1,), (1,)), ((), ()))
    q = q_ref[0].astype(BF16)
    kw = q.shape[1]
    se = lax.dot_general(q, ke_ref[0].astype(BF16), nt, preferred_element_type=F32) + be_ref[0]
    m = jnp.max(se, axis=1, keepdims=True)
    if n_pages:
        s = jnp.concatenate([jnp.dot(q, k_refs[p][0, 0].reshape(kw, PAGE_SIZE).astype(BF16),
                                     preferred_element_type=F32) for p in range(n_pages)], axis=1) + bp_ref[0]
        m = jnp.maximum(m, jnp.max(s, axis=1, keepdims=True))
    pe = jnp.exp(se - m)
    l = jnp.sum(pe, axis=1, keepdims=True)
    o = jnp.dot(pe.astype(BF16), ve_ref[0].astype(BF16), preferred_element_type=F32)
    if n_pages:
        p = jnp.exp(s - m)
        l = l + jnp.sum(p, axis=1, keepdims=True)
        pb = p.astype(BF16)
        for pg in range(n_pages):
            o = o + lax.dot_general(pb[:, pg * PAGE_SIZE:(pg + 1) * PAGE_SIZE],
                                    v_refs[pg][0, 0].reshape(kw, PAGE_SIZE).astype(BF16), nt,
                                    preferred_element_type=F32)
    o_ref[0] = o / l


def _decode_attn(qblk, k_extra, v_extra, bias_extra, pool_t=None, page_table=None, k_kind=0, v_kind=0,
                 bias_pages=None):
    B, R, KW = qblk.shape
    NE = k_extra.shape[1]
    n_pages = 0 if pool_t is None else page_table.shape[1]
    row3 = lambda b, pt: (b, 0, 0)
    in_specs = [pl.BlockSpec((1, R, KW), row3)]
    args = [qblk]
    if n_pages:
        G = pool_t.shape[2]
        assert G * HEAD_DIM == KW
        for kind in (k_kind, v_kind):
            for p in range(n_pages):
                in_specs.append(pl.BlockSpec((1, 1, G, HEAD_DIM, PAGE_SIZE),
                                             functools.partial(lambda b, pt, p, c: (pt[b, p], c, 0, 0, 0), p=p, c=kind)))
                args.append(pool_t)
        in_specs.append(pl.BlockSpec((1, R, n_pages * PAGE_SIZE), row3))
        args.append(bias_pages)
    else:
        page_table = jnp.zeros((1, 1), jnp.int32)
    be_map = row3 if bias_extra.shape[0] == B else (lambda b, pt: (0, 0, 0))
    in_specs += [pl.BlockSpec((1, NE, KW), row3), pl.BlockSpec((1, NE, KW), row3), pl.BlockSpec((1, R, NE), be_map)]
    args += [k_extra, v_extra, bias_extra]
    grid_spec = pltpu.PrefetchScalarGridSpec(
        num_scalar_prefetch=1, grid=(B,), in_specs=in_specs,
        out_specs=pl.BlockSpec((1, R, KW), row3))
    return pl.pallas_call(
        functools.partial(_decode_attn_kernel, n_pages=n_pages),
        grid_spec=grid_spec,
        out_shape=jax.ShapeDtypeStruct((B, R, KW), F32),
        compiler_params=pltpu.CompilerParams(dimension_semantics=("arbitrary",), vmem_limit_bytes=VMEM_LIMIT),
        name="decode_attn",
    )(page_table, *args)


def _block_diag_queries(q, G):
    B, T, W = q.shape
    H = W // HEAD_DIM
    own = (jnp.arange(H)[:, None] // (H // G) == jnp.arange(G)[None, :]).astype(F32)
    qs = (q * HEAD_DIM ** -0.5).reshape(B, T, H, 1, HEAD_DIM) * own[None, None, :, :, None]
    return qs.reshape(B, T * H, G * HEAD_DIM)


def _own_head_columns(o, T, G):
    B, R, _ = o.shape
    H = R // T
    own = (jnp.arange(H)[:, None] // (H // G) == jnp.arange(G)[None, :]).astype(F32)
    o5 = o.reshape(B, T, H, G, HEAD_DIM) * own[None, None, :, :, None]
    return jnp.sum(o5, axis=3).reshape(B, T, H * HEAD_DIM)


def _moe_kernel(be_ref, nb_ref, x_ref, w1_ref, b1_ref, w2_ref, b2_ref, o_ref, w1b_ref, w2b_ref):
    i = pl.program_id(0)
    prev = be_ref[jnp.maximum(i - 1, 0)]

    @pl.when(jnp.logical_or(i == 0, be_ref[i] != prev))
    def _():
        w1b_ref[...] = w1_ref[0].astype(BF16)
        w2b_ref[...] = w2_ref[0].astype(BF16)

    @pl.when(i < nb_ref[0])
    def _():
        gu = jnp.dot(x_ref[...].astype(BF16), w1b_ref[...], preferred_element_type=F32) + b1_ref[0]
        g = jnp.minimum(gu[:, :D_FF], SWIGLU_LIMIT)
        up = jnp.clip(gu[:, D_FF:], -SWIGLU_LIMIT, SWIGLU_LIMIT)
        h = (up + 1) * g * jax.nn.sigmoid(SWIGLU_ALPHA * g)
        o_ref[...] = jnp.dot(h.astype(BF16), w2b_ref[...], preferred_element_type=F32) + b2_ref[0]

    @pl.when(i >= nb_ref[0])
    def _():
        o_ref[...] = jnp.zeros_like(o_ref)


def _moe_experts(xs, blk_e, n_used, w1, b1, w2, b2):
    n_blocks = xs.shape[0] // MOE_TM
    grid_spec = pltpu.PrefetchScalarGridSpec(
        num_scalar_prefetch=2,
        grid=(n_blocks,),
        in_specs=[pl.BlockSpec((MOE_TM, D_MODEL), lambda i, be, nb: (i, 0)),
                  pl.BlockSpec((1, D_MODEL, 2 * D_FF), lambda i, be, nb: (be[i], 0, 0)),
                  pl.BlockSpec((1, 1, 2 * D_FF), lambda i, be, nb: (be[i], 0, 0)),
                  pl.BlockSpec((1, D_FF, D_MODEL), lambda i, be, nb: (be[i], 0, 0)),
                  pl.BlockSpec((1, 1, D_MODEL), lambda i, be, nb: (be[i], 0, 0))],
        out_specs=pl.BlockSpec((MOE_TM, D_MODEL), lambda i, be, nb: (i, 0)),
        scratch_shapes=[pltpu.VMEM((D_MODEL, 2 * D_FF), BF16), pltpu.VMEM((D_FF, D_MODEL), BF16)],
    )
    return pl.pallas_call(
        _moe_kernel,
        grid_spec=grid_spec,
        out_shape=jax.ShapeDtypeStruct((n_blocks * MOE_TM, D_MODEL), F32),
        compiler_params=pltpu.CompilerParams(
            dimension_semantics=("arbitrary",), vmem_limit_bytes=VMEM_LIMIT),
        name="moe_experts",
    )(blk_e, n_used, xs, w1, b1.reshape(N_EXPERTS, 1, 2 * D_FF), w2, b2.reshape(N_EXPERTS, 1, D_MODEL))


def _moe_combine_kernel(y_ref, g_ref, o_ref):
    g = g_ref[...]
    acc = g[:, 0:1] * y_ref[:, :D_MODEL]
    for k in range(1, TOP_K):
        acc = acc + g[:, k:k + 1] * y_ref[:, k * D_MODEL:(k + 1) * D_MODEL]
    o_ref[...] = acc


def _moe_combine(yg, gate):
    N = yg.shape[0]
    tn = _pick_tile(N, (512, 256, 128, 64, 32, 16, 8))
    return pl.pallas_call(
        _moe_combine_kernel,
        grid=(N // tn,),
        in_specs=[pl.BlockSpec((tn, TOP_K * D_MODEL), lambda i: (i, 0)),
                  pl.BlockSpec((tn, TOP_K), lambda i: (i, 0))],
        out_specs=pl.BlockSpec((tn, D_MODEL), lambda i: (i, 0)),
        out_shape=jax.ShapeDtypeStruct((N, D_MODEL), F32),
        compiler_params=pltpu.CompilerParams(dimension_semantics=("arbitrary",), vmem_limit_bytes=VMEM_LIMIT),
        name="moe_combine",
    )(yg, gate)


def _moe(xf, w_r, b_r, w1, b1, w2, b2):
    N = xf.shape[0]
    n_rows = N * TOP_K
    logits = _mm(xf, w_r, b_r, precise=True)
    top_v, top_i = lax.top_k(logits, TOP_K)
    gate = jax.nn.softmax(top_v, axis=-1)
    flat_e = top_i.reshape(-1)
    order = jnp.argsort(flat_e)
    sorted_e = flat_e[order]
    counts = jnp.bincount(flat_e, length=N_EXPERTS)
    padded = (counts + MOE_TM - 1) // MOE_TM * MOE_TM
    pad_end = jnp.cumsum(padded)
    pad_start = pad_end - padded
    start = jnp.cumsum(counts) - counts
    dest = (pad_start[sorted_e] + jnp.arange(n_rows) - start[sorted_e]).astype(jnp.int32)
    n_blocks = -(-(n_rows + N_EXPERTS * (MOE_TM - 1)) // MOE_TM)
    n_used = (pad_end[-1] // MOE_TM).astype(jnp.int32)
    blk = jnp.minimum(jnp.arange(n_blocks), n_used - 1) * MOE_TM
    blk_e = jnp.minimum(jnp.sum(pad_end[None, :] <= blk[:, None], axis=1), N_EXPERTS - 1).astype(jnp.int32)
    row_e = jnp.repeat(blk_e, MOE_TM)
    off = jnp.arange(n_blocks * MOE_TM) - pad_start[row_e]
    src = jnp.clip(start[row_e] + off, 0, n_rows - 1)
    row_tok = jnp.where(off < counts[row_e], (order // TOP_K)[src], 0).astype(jnp.int32)
    xs = xf[row_tok]
    ys = _moe_experts(xs, blk_e, n_used.reshape(1), w1, b1, w2, b2)
    pos = dest[jnp.argsort(order)]
    return _moe_combine(ys[pos].reshape(N, TOP_K * D_MODEL), gate)


def _rmsnorm(x, g):
    y = x * lax.rsqrt(jnp.mean(x * x, axis=-1, keepdims=True) + EPS)
    return y * g


def _modulated_norm(x, g, shift, scale):
    return _rmsnorm(x, g) * (1 + scale[:, None, :]) + shift[:, None, :]


def _split(x, sizes):
    return jnp.split(x, np.cumsum(sizes)[:-1].tolist(), axis=-1)


def _gather_pages(pool, page_table):
    g = pool[page_table]
    return g.reshape((g.shape[0], g.shape[1] * g.shape[2]) + g.shape[3:])


def _attend(q, k, v, mask, bias=None):
    B, Q, H, D = q.shape
    G = k.shape[2]
    s = jnp.einsum('bqgnd,bkgd->bgnqk', q.reshape(B, Q, G, H // G, D), k).astype(F32) * D ** -0.5
    if bias is not None:
        s = s + bias
    p = jax.nn.softmax(jnp.where(mask, s, NEG_INF), axis=-1).astype(v.dtype)
    return jnp.einsum('bgnqk,bkgd->bqgnd', p, v).reshape(B, Q, H, D)


def _window_mask(q_pos, k_pos):
    d = q_pos[:, None] - k_pos[None, :]
    return (d >= 0) & (d < WINDOW) & (k_pos[None, :] >= 0)


def _to_heads(x, G):
    B, T, _ = x.shape
    return x.astype(BF16).reshape(B, T, G, HEAD_DIM).transpose(0, 2, 1, 3)


def _q_to_heads(q, G):
    B, T, W = q.shape
    R = W // HEAD_DIM // G
    return (q * HEAD_DIM ** -0.5).astype(BF16).reshape(B, T, G, R, HEAD_DIM).transpose(0, 2, 3, 1, 4)


def _from_heads(o):
    B, G, R, T, D = o.shape
    return o.transpose(0, 3, 1, 2, 4).reshape(B, T, G * R * D)


CMP_CHUNK_W = CMP_STRIDE * HEAD_DIM
CMP_HID = 256


def _cmp_chunk_proj(tokens, w1):
    assert CMP_BLK == 2 * CMP_STRIDE
    S, T, G, D = tokens.shape
    n = T // CMP_STRIDE
    ch = tokens.reshape(S, n, CMP_STRIDE, G, D).transpose(0, 1, 3, 2, 4).reshape(S * n * G, CMP_CHUNK_W)
    wcat = jnp.concatenate([w1[:CMP_CHUNK_W], w1[CMP_CHUNK_W:]], axis=1)
    return _mm(ch, wcat).reshape(S, n, G, 2 * CMP_HID)


CMP_PAGES = 32


def _cmp_pages_kernel(p_ref, w_ref, o_ref, mt_ref):
    P = p_ref.shape[0]
    chunks = PAGE_SIZE // CMP_STRIDE
    for kind in range(2):
        for g in range(NSA_KV_HEADS):
            for p in range(P):
                mt_ref[p * PAGE_SIZE:(p + 1) * PAGE_SIZE, :] = p_ref[p, kind, g].T
            acc = jnp.zeros((P * chunks, 2 * CMP_HID), F32)
            for j in range(CMP_STRIDE):
                rows = mt_ref[pl.ds(j, P * chunks, stride=CMP_STRIDE), :].astype(BF16)
                acc = acc + jnp.dot(rows, w_ref[kind, j * HEAD_DIM:(j + 1) * HEAD_DIM, :], preferred_element_type=F32)
            o_ref[kind, :, g * 2 * CMP_HID:(g + 1) * 2 * CMP_HID] = acc


def _cmp_pages_proj(pool_t, w1):
    n_phys = pool_t.shape[0]
    chunks = PAGE_SIZE // CMP_STRIDE
    assert n_phys % CMP_PAGES == 0
    wcat = jnp.concatenate([w1[:, :CMP_CHUNK_W], w1[:, CMP_CHUNK_W:]], axis=2).astype(BF16)
    width = NSA_KV_HEADS * 2 * CMP_HID
    out = pl.pallas_call(
        _cmp_pages_kernel,
        grid=(n_phys // CMP_PAGES,),
        in_specs=[pl.BlockSpec((CMP_PAGES, 2, NSA_KV_HEADS, HEAD_DIM, PAGE_SIZE), lambda i: (i, 0, 0, 0, 0)),
                  pl.BlockSpec(wcat.shape, lambda i: (0, 0, 0))],
        out_specs=pl.BlockSpec((2, CMP_PAGES * chunks, width), lambda i: (0, i, 0)),
        out_shape=jax.ShapeDtypeStruct((2, n_phys * chunks, width), F32),
        scratch_shapes=[pltpu.VMEM((CMP_PAGES * PAGE_SIZE, HEAD_DIM), F32)],
        compiler_params=pltpu.CompilerParams(dimension_semantics=("arbitrary",), vmem_limit_bytes=VMEM_LIMIT),
        name="cmp_pages",
    )(pool_t, wcat)
    return out.reshape(2, n_phys, chunks, NSA_KV_HEADS, 2 * CMP_HID)


def _cmp_out_kernel(a_ref, b_ref, bias_ref, w_ref, o_ref):
    hid = jax.nn.gelu(a_ref[...] + b_ref[...] + bias_ref[...])
    o_ref[...] = jnp.dot(hid.astype(BF16), w_ref[...].astype(BF16), preferred_element_type=F32)


def _cmp_finish(proj, pe, w1, w2):
    B, n, G, _ = proj.shape
    first = proj[:, :-1, :, :CMP_HID].reshape(-1, CMP_HID)
    second = proj[:, 1:, :, CMP_HID:].reshape(-1, CMP_HID)
    bias = _mm(pe.reshape(1, CMP_BLK * HEAD_DIM), w1)
    rows = first.shape[0]
    rows_p = _round_up(rows, 512)
    pad = lambda a: jnp.pad(a, ((0, rows_p - rows), (0, 0)))
    w2p = jnp.pad(w2, ((0, 0), (0, LANE - HEAD_DIM)))
    out = pl.pallas_call(
        _cmp_out_kernel,
        grid=(rows_p // 512,),
        in_specs=[pl.BlockSpec((512, CMP_HID), lambda i: (i, 0)), pl.BlockSpec((512, CMP_HID), lambda i: (i, 0)),
                  pl.BlockSpec((1, CMP_HID), lambda i: (0, 0)), pl.BlockSpec((CMP_HID, LANE), lambda i: (0, 0))],
        out_specs=pl.BlockSpec((512, LANE), lambda i: (i, 0)),
        out_shape=jax.ShapeDtypeStruct((rows_p, LANE), F32),
        compiler_params=pltpu.CompilerParams(dimension_semantics=("arbitrary",), vmem_limit_bytes=VMEM_LIMIT),
        name="cmp_out",
    )(pad(first), pad(second), bias, w2p)
    return out[:rows, :HEAD_DIM].reshape(B, n - 1, G, HEAD_DIM)


def _cmp_attention(q, kc, vc, q_pos):
    B, Q, H, D = q.shape
    N, G = kc.shape[1], kc.shape[2]
    c_end = jnp.arange(N) * CMP_STRIDE + CMP_BLK
    valid = c_end[None, :] <= q_pos[:, None] + 1
    s = jnp.einsum('bqgnd,bcgd->bgnqc', q.reshape(B, Q, G, H // G, D), kc).astype(F32) * D ** -0.5
    p = jax.nn.softmax(jnp.where(valid, s, NEG_INF), axis=-1) * valid
    o = jnp.einsum('bgnqc,bcgd->bqgnd', p.astype(vc.dtype), vc).reshape(B, Q, H, D)
    return o, p


def _cmp_to_sel(n_cmp, n_sel):
    c0 = jnp.arange(n_cmp)[:, None] * CMP_STRIDE
    s0 = jnp.arange(n_sel)[None, :] * SEL_BLK
    ov = jnp.minimum(c0 + CMP_BLK, s0 + SEL_BLK) - jnp.maximum(c0, s0)
    return jnp.maximum(ov, 0).astype(F32) / CMP_BLK


def _selection_scores(p_cmp, q_pos, n_sel):
    imp = jnp.einsum('bgnqc,cs->bgqs', p_cmp, _cmp_to_sel(p_cmp.shape[-1], n_sel))
    blk = jnp.arange(n_sel)[None, :]
    cur = (q_pos // SEL_BLK)[:, None]
    valid = blk * SEL_BLK <= q_pos[:, None]
    forced = (blk == 0) | (blk == cur) | (blk == cur - 1)
    return jnp.where(valid, jnp.where(forced, BIG, imp), -BIG)


def _selection_mask(score):
    n_sel = score.shape[-1]
    a = score[..., :, None]
    b = score[..., None, :]
    lower = jnp.arange(n_sel)[None, :] < jnp.arange(n_sel)[:, None]
    beats = (b > a) | ((b == a) & lower)
    rank = jnp.sum(beats, axis=-1)
    return (rank < min(SEL_TOPN, n_sel)).astype(F32)


def _selected_attention_gather(q, k, v, p_cmp, q_pos):
    B, Q, H, D = q.shape
    T_all, G = k.shape[1], k.shape[2]
    n_sel = -(-T_all // SEL_BLK)
    score = _selection_scores(p_cmp, q_pos, n_sel)
    _, idx = lax.top_k(score, min(SEL_TOPN, n_sel))
    tok = (idx[..., None] * SEL_BLK + jnp.arange(SEL_BLK)).reshape(B, G, Q, -1)
    pad = ((0, 0), (0, n_sel * SEL_BLK - T_all), (0, 0), (0, 0))
    kt = jnp.pad(k, pad).transpose(0, 2, 1, 3)
    vt = jnp.pad(v, pad).transpose(0, 2, 1, 3)
    b_ix = jnp.arange(B)[:, None, None]
    g_ix = jnp.arange(G)[None, :, None]
    L = tok.shape[3]
    flat = tok.reshape(B, G, Q * L)
    ks = kt[b_ix, g_ix, flat].reshape(B, G, Q, L, D)
    vs = vt[b_ix, g_ix, flat].reshape(B, G, Q, L, D)
    s = jnp.einsum('bqgnd,bgqld->bgnql', q.reshape(B, Q, G, H // G, D), ks).astype(F32) * D ** -0.5
    mask = (tok <= q_pos[None, None, :, None])[:, :, None]
    p = jax.nn.softmax(jnp.where(mask, s, NEG_INF), axis=-1)
    return jnp.einsum('bgnql,bgqld->bqgnd', p, vs).reshape(B, Q, H, D)


def _gla_kernel(q_ref, k_ref, v_ref, gr_ref, tail_ref, wa_ref, ba_ref, g_ref, s0_ref, o_ref, sT_ref, st_ref,
                *, C, n_chunks, n_valid):
    @pl.when(pl.program_id(1) == 0)
    def _():
        st_ref[...] = s0_ref[0]

    tri = (lax.broadcasted_iota(jnp.int32, (C, C), 0) >= lax.broadcasted_iota(jnp.int32, (C, C), 1))
    tri_f = jnp.where(tri, 1.0, 0.0)
    for c in range(n_chunks):
        rows = pl.ds(c * C, C)
        za = jnp.dot(tail_ref[0, rows, :].astype(BF16), wa_ref[...], preferred_element_type=F32) + ba_ref[...]
        log_a = jax.nn.log_sigmoid(za) / GLA_TAU
        q = q_ref[0, rows, :] * GLA_DK ** -0.5
        k = k_ref[0, rows, :]
        v = v_ref[0, rows, :]
        if n_valid < C:
            live = lax.broadcasted_iota(jnp.int32, (C, 1), 0) < n_valid
            log_a = jnp.where(live, log_a, 0.0)
            q, k, v = jnp.where(live, q, 0.0), jnp.where(live, k, 0.0), jnp.where(live, v, 0.0)
        b = jnp.dot(tri_f, log_a, preferred_element_type=F32, precision=lax.Precision.HIGHEST)
        b_last = b[C - 1:C, :]
        q_dec = (q * jnp.exp(b)).astype(BF16)
        k_inv = (k * jnp.exp(-b)).astype(BF16)
        k_end = (k * jnp.exp(b_last - b)).astype(BF16)
        decay = jnp.exp(b_last)
        v = v.astype(BF16)
        gr = gr_ref[0, rows, :]
        outs = []
        for h in range(GLA_HEADS):
            kh = slice(h * GLA_DK, (h + 1) * GLA_DK)
            vh = slice(h * GLA_DV, (h + 1) * GLA_DV)
            attn = lax.dot_general(q_dec[:, kh], k_inv[:, kh], (((1,), (1,)), ((), ())), preferred_element_type=F32)
            attn = jnp.where(tri, attn, 0.0).astype(BF16)
            sT = st_ref[h]
            o = jnp.dot(attn, v[:, vh], preferred_element_type=F32)
            o = o + lax.dot_general(q_dec[:, kh], sT.astype(BF16), (((1,), (1,)), ((), ())), preferred_element_type=F32)
            st_ref[h] = decay[:, kh] * sT + lax.dot_general(v[:, vh], k_end[:, kh], (((0,), (0,)), ((), ())),
                                                            preferred_element_type=F32)
            o = o * lax.rsqrt(jnp.mean(o * o, axis=-1, keepdims=True) + EPS) * g_ref[...]
            outs.append(o * jax.nn.silu(gr[:, vh]))
        o_ref[0, rows, :] = jnp.concatenate(outs, axis=1)

    @pl.when(pl.program_id(1) == pl.num_programs(1) - 1)
    def _():
        sT_ref[0] = st_ref[...]


def _gla(proj, cols, gla_wa, gla_ba, gla_g, s0, C, n_chunks, n_valid):
    B, T, _ = proj.shape
    tt = C * n_chunks
    s0T = s0.transpose(0, 1, 3, 2)
    cq, ck, cv, cr, ca = cols
    tail0 = ca // GLA_QK * GLA_QK
    assert cq % GLA_QK == 0 and ck % GLA_QK == 0 and cv % GLA_WIDTH == 0 and cr % GLA_WIDTH == 0
    assert ca + GLA_LOWRANK <= tail0 + GLA_QK
    wa_pad = jnp.zeros((GLA_QK, GLA_QK), F32).at[ca - tail0:ca - tail0 + GLA_LOWRANK].set(gla_wa).astype(BF16)
    col_spec = lambda width, col: pl.BlockSpec((1, tt, width), lambda b, i: (b, i, col // width))
    v_spec = pl.BlockSpec((1, tt, GLA_WIDTH), lambda b, i: (b, i, 0))
    st_spec = pl.BlockSpec((1, GLA_HEADS, GLA_DV, GLA_DK), lambda b, i: (b, 0, 0, 0))
    const2 = lambda b, i: (0, 0)
    o, sT = pl.pallas_call(
        functools.partial(_gla_kernel, C=C, n_chunks=n_chunks, n_valid=n_valid),
        grid=(B, T // tt),
        in_specs=[col_spec(GLA_QK, cq), col_spec(GLA_QK, ck), col_spec(GLA_WIDTH, cv), col_spec(GLA_WIDTH, cr),
                  col_spec(GLA_QK, tail0), pl.BlockSpec((GLA_QK, GLA_QK), const2), pl.BlockSpec((1, GLA_QK), const2),
                  pl.BlockSpec((1, GLA_DV), const2), st_spec],
        out_specs=[v_spec, st_spec],
        out_shape=[jax.ShapeDtypeStruct((B, T, GLA_WIDTH), F32),
                   jax.ShapeDtypeStruct((B, GLA_HEADS, GLA_DV, GLA_DK), F32)],
        scratch_shapes=[pltpu.VMEM((GLA_HEADS, GLA_DV, GLA_DK), F32)],
        compiler_params=pltpu.CompilerParams(
            dimension_semantics=("arbitrary", "arbitrary"), vmem_limit_bytes=VMEM_LIMIT),
        name="gla",
    )(proj, proj, proj, proj, proj, wa_pad, gla_ba.reshape(1, GLA_QK), gla_g.reshape(1, GLA_DV), s0T)
    return o, sT.transpose(0, 1, 3, 2)


S5_NK = 4
S5_GPK = S5_GROUPS // S5_NK
S5_HALF = S5_GPK * S5_STATE


def _s5_kernel(u_ref, wb_ref, wc_ref, coef_ref, d_ref, gw_ref, gb_ref, h0_ref, y_ref, hfin_ref, hst_ref, xb_ref,
               *, BT, Tc):
    @pl.when(pl.program_id(0) == 0)
    def _():
        hst_ref[...] = h0_ref[...]

    u = u_ref[...]
    ys = []
    for k in range(S5_NK):
        bu = jnp.dot(u[:, k * LANE:(k + 1) * LANE].astype(BF16), wb_ref[k], preferred_element_type=F32)
        bre, bim = bu[:, :S5_HALF], bu[:, S5_HALF:]
        ar, ai = coef_ref[k, 0:1, :], coef_ref[k, 1:2, :]
        cr, ci = coef_ref[k, 2:3, :], coef_ref[k, 3:4, :]
        xb_ref[:, :S5_HALF] = cr * bre - ci * bim
        xb_ref[:, S5_HALF:] = cr * bim + ci * bre

        def step(t, carry):
            hr, hi = carry
            rows = pl.ds(pl.multiple_of(t * BT, BT), BT)
            hr2 = ar * hr - ai * hi + xb_ref[rows, :S5_HALF]
            hi2 = ar * hi + ai * hr + xb_ref[rows, S5_HALF:]
            xb_ref[rows, :S5_HALF] = hr2
            xb_ref[rows, S5_HALF:] = hi2
            return hr2, hi2

        hr, hi = lax.fori_loop(0, Tc, step, (hst_ref[k, :, :S5_HALF], hst_ref[k, :, S5_HALF:]),
                               unroll=min(Tc, 8))
        hst_ref[k, :, :S5_HALF] = hr
        hst_ref[k, :, S5_HALF:] = hi
        ys.append(jnp.dot(xb_ref[...].astype(BF16), wc_ref[k], preferred_element_type=F32))
    y = jax.nn.gelu(jnp.concatenate(ys, axis=1) + d_ref[...] * u)
    z = jnp.dot(y.astype(BF16), gw_ref[...].astype(BF16), preferred_element_type=F32) + gb_ref[...]
    y_ref[...] = y * jax.nn.sigmoid(z)

    @pl.when(pl.program_id(0) == pl.num_programs(0) - 1)
    def _():
        hfin_ref[...] = hst_ref[...]


def _s5(u, a_re, a_im, b_re, b_im, c_re, c_im, d, log_dt, glu_w, glu_b, h0, Tc):
    B, T, _ = u.shape
    dt = jnp.exp(log_dt)[:, None]
    mag = jnp.exp(a_re * dt)
    ab_re, ab_im = mag * jnp.cos(a_im * dt), mag * jnp.sin(a_im * dt)
    den = a_re * a_re + a_im * a_im
    coef_re = ((ab_re - 1) * a_re + ab_im * a_im) / den
    coef_im = (ab_im * a_re - (ab_re - 1) * a_im) / den
    coefs = jnp.stack([ab_re, ab_im, coef_re, coef_im], axis=0).reshape(4, S5_NK, S5_HALF).transpose(1, 0, 2)
    eye = jnp.eye(S5_GPK, dtype=F32)

    def in_weights(bm):
        bk = bm.reshape(S5_NK, S5_GPK, S5_STATE, S5_CH)
        return jnp.einsum('kgpc,gh->kgchp', bk, eye).reshape(S5_NK, S5_GPK * S5_CH, S5_HALF)

    def out_weights(cm):
        ck = cm.reshape(S5_NK, S5_GPK, S5_CH, S5_STATE)
        return jnp.einsum('kgcp,gh->kgphc', ck, eye).reshape(S5_NK, S5_HALF, S5_GPK * S5_CH)

    wb = jnp.concatenate([in_weights(b_re), in_weights(b_im)], axis=2).astype(BF16)
    wc = jnp.concatenate([out_weights(c_re), -out_weights(c_im)], axis=1).astype(BF16)
    if h0 is None:
        hs0 = jnp.zeros((S5_NK, B, 2 * S5_HALF), F32)
    else:
        hs0 = h0.reshape(B, S5_NK, S5_HALF, 2).transpose(1, 0, 3, 2).reshape(S5_NK, B, 2 * S5_HALF)
    ut = u.transpose(1, 0, 2).reshape(T * B, S5_WIDTH)
    rows = Tc * B
    const2 = lambda i: (0, 0)
    const3 = lambda i: (0, 0, 0)
    y, hfin = pl.pallas_call(
        functools.partial(_s5_kernel, BT=B, Tc=Tc),
        grid=(T // Tc,),
        in_specs=[pl.BlockSpec((rows, S5_WIDTH), lambda i: (i, 0)),
                  pl.BlockSpec(wb.shape, const3), pl.BlockSpec(wc.shape, const3), pl.BlockSpec(coefs.shape, const3),
                  pl.BlockSpec((1, S5_WIDTH), const2), pl.BlockSpec((S5_WIDTH, S5_WIDTH), const2),
                  pl.BlockSpec((1, S5_WIDTH), const2), pl.BlockSpec(hs0.shape, const3)],
        out_specs=[pl.BlockSpec((rows, S5_WIDTH), lambda i: (i, 0)), pl.BlockSpec(hs0.shape, const3)],
        out_shape=[jax.ShapeDtypeStruct((T * B, S5_WIDTH), F32), jax.ShapeDtypeStruct(hs0.shape, F32)],
        scratch_shapes=[pltpu.VMEM(hs0.shape, F32), pltpu.VMEM((rows, 2 * S5_HALF), F32)],
        compiler_params=pltpu.CompilerParams(dimension_semantics=("arbitrary",), vmem_limit_bytes=VMEM_LIMIT),
        name="s5",
    )(ut, wb, wc, coefs, d.reshape(1, S5_WIDTH), glu_w, glu_b.reshape(1, S5_WIDTH), hs0)
    y = y.reshape(T, B, S5_WIDTH).transpose(1, 0, 2)
    hfin = hfin.reshape(S5_NK, B, 2, S5_HALF).transpose(1, 0, 3, 2).reshape(B, S5_GROUPS, S5_STATE, 2)
    return y, hfin


AB_ORDER = (0, 5, 6, 1, 3, 4, 2, 7)
AB_START = dict(zip(AB_ORDER, np.cumsum([0] + [AB_SPLITS[s] for s in AB_ORDER[:-1]]).tolist()))
AB_PERM = np.concatenate([np.arange(AB_SPLITS[s]) + sum(AB_SPLITS[:s]) for s in AB_ORDER])


def _mixer_ab(x, norm, w_in, w_out, cmp_pe, cmp_w1, cmp_w2, gla_wa, gla_ba, gla_g, past):
    B, T, _ = x.shape
    if past is None:
        projp, kv_t = _norm_proj(x, norm[0], norm[1], norm[2], w_in[:, AB_PERM], t_cols=(AB_START[1], AB_SPLITS[1]))
    else:
        projp = _norm_proj(x, norm[0], norm[1], norm[2], w_in[:, AB_PERM])
    seg = lambda s: projp[..., AB_START[s]:AB_START[s] + AB_SPLITS[s]]
    q, kv6, gate_logit = seg(0), seg(1), seg(2)
    kv6 = kv6.reshape(B, T, 6, NSA_KV_HEADS, HEAD_DIM)
    new_nsa, new_win = kv6[:, :, :4], kv6[:, :, 4:]
    s0 = jnp.zeros((B, GLA_HEADS, GLA_DK, GLA_DV), F32)
    if past is None:
        t0 = 0
        kv_full, kv_win = new_nsa, new_win
        cmp_proj = [_cmp_chunk_proj(new_nsa[:, :T // CMP_STRIDE * CMP_STRIDE, c], cmp_w1[c]) for c in range(2)]
    else:
        pool, page_table, win_buf, s0 = past
        pool_t = pool.transpose(0, 2, 3, 4, 1)
        n_pages = page_table.shape[1]
        t0 = n_pages * PAGE_SIZE
        assert PAGE_SIZE % CMP_STRIDE == 0 and T < CMP_STRIDE
        pages_proj = _cmp_pages_proj(pool_t, cmp_w1)
        cmp_proj = [pages_proj[c][page_table].reshape(
            B, n_pages * (PAGE_SIZE // CMP_STRIDE), NSA_KV_HEADS, 2 * CMP_HID) for c in range(2)]
        kv_win = jnp.concatenate([win_buf, new_win], axis=1)
    q_pos = t0 + jnp.arange(T)
    q4 = q.reshape(B, T, NSA_HEADS, HEAD_DIM)
    kc = _cmp_finish(cmp_proj[0], cmp_pe[0], cmp_w1[0], cmp_w2[0])
    vc = _cmp_finish(cmp_proj[1], cmp_pe[1], cmp_w1[1], cmp_w2[1])
    o_cmp, p_cmp = _cmp_attention(q4, kc, vc, q_pos)
    if past is None:
        n_sel = -(-T // SEL_BLK)
        sel = _selection_mask(_selection_scores(p_cmp, q_pos, n_sel))
        kv_col = lambda kind: AB_START[1] + kind * NSA_KVW
        hpg = NSA_HEADS // NSA_KV_HEADS
        o_slc = _attn(projp, 0, kv_col(2), kv_col(3), G=NSA_KV_HEADS, R=hpg, tq=128, tk=512, sel=sel)
        o_win = _attn(projp, 0, kv_col(4), kv_col(5), G=NSA_KV_HEADS, R=hpg, tq=128, tk=512, window=WINDOW)
        o_slc = o_slc.reshape(B, T, NSA_HEADS, HEAD_DIM)
        o_win = o_win.reshape(B, T, NSA_HEADS, HEAD_DIM)
        n_win = min(WINDOW, T)
        rows_out = lambda a, kinds: a.reshape(B, kinds, NSA_KV_HEADS, HEAD_DIM, -1).transpose(0, 4, 1, 2, 3)
        new_nsa = rows_out(kv_t[:, :4 * NSA_KVW], 4)
        win_state = rows_out(kv_t[:, 4 * NSA_KVW:, T - n_win:], 2)
    else:
        assert t0 % SEL_BLK == 0 and T <= SEL_BLK
        n_sel = t0 // SEL_BLK + 1
        hpg = NSA_HEADS // NSA_KV_HEADS
        sel = _selection_mask(_selection_scores(p_cmp, q_pos, n_sel))
        sel_rows = jnp.repeat(sel.transpose(0, 2, 1, 3), hpg, axis=2).reshape(B, T * NSA_HEADS, n_sel)
        row_t = jnp.repeat(jnp.arange(T), NSA_HEADS)
        new_j = jnp.arange(LANE)
        bias_pages = jnp.where(jnp.repeat(sel_rows[:, :, :n_sel - 1], SEL_BLK, axis=2) > 0.5, 0.0, NEG_INF)
        causal_new = (new_j[None, :] <= row_t[:, None]) & (new_j[None, :] < T)
        bias_new = jnp.where((sel_rows[:, :, n_sel - 1:] > 0.5) & causal_new[None], 0.0, NEG_INF)
        pad_rows = lambda a, n: jnp.pad(a.reshape(B, a.shape[1], NSA_KVW), ((0, 0), (0, n - a.shape[1]), (0, 0)))
        qblk = _block_diag_queries(q, NSA_KV_HEADS)
        o_slc = _decode_attn(qblk, pad_rows(new_nsa[:, :, 2], LANE), pad_rows(new_nsa[:, :, 3], LANE), bias_new,
                             pool_t=pool_t, page_table=page_table, k_kind=2, v_kind=3, bias_pages=bias_pages)
        o_slc = _own_head_columns(o_slc, T, NSA_KV_HEADS).reshape(B, T, NSA_HEADS, HEAD_DIM)
        wb = win_buf.shape[1]
        n_win = _round_up(wb + T, LANE)
        k_pos = t0 - wb + jnp.arange(n_win)
        in_win = _window_mask(q_pos, k_pos) & (jnp.arange(n_win) < wb + T)[None, :]
        bias_win = jnp.where(in_win, 0.0, NEG_INF)[row_t][None]
        o_win = _decode_attn(qblk, pad_rows(kv_win[:, :, 0], n_win), pad_rows(kv_win[:, :, 1], n_win), bias_win)
        o_win = _own_head_columns(o_win, T, NSA_KV_HEADS).reshape(B, T, NSA_HEADS, HEAD_DIM)
        win_state = kv_win[:, -wb:]
    g = jax.nn.sigmoid(gate_logit).reshape(B, T, NSA_HEADS, 3, 1)
    o_nsa = (g[:, :, :, 0] * o_cmp + g[:, :, :, 1] * o_slc + g[:, :, :, 2] * o_win).reshape(B, T, NSA_WIDTH)
    gla_cols = (AB_START[3], AB_START[4], AB_START[5], AB_START[6], AB_START[7])
    if T % GLA_CHUNK == 0:
        o_gla, s_new = _gla(projp, gla_cols, gla_wa, gla_ba, gla_g, s0, GLA_CHUNK, 4, GLA_CHUNK)
    else:
        Tp = _round_up(T, 2 * SUBLANE)
        o_gla, s_new = _gla(jnp.pad(projp, ((0, 0), (0, Tp - T), (0, 0))), gla_cols, gla_wa, gla_ba, gla_g, s0,
                            Tp, 1, T)
        o_gla = o_gla[:, :T]
    return _out_proj(o_nsa, o_gla, x, norm[3], w_out), (new_nsa, win_state, s_new)


def _mixer_cd(x, norm, w_in, w_out, b_f, a_re, a_im, b_re, b_im, c_re, c_im, d, log_dt, glu_w, glu_b, past):
    B, T, _ = x.shape
    if past is None:
        projp, kv_t = _norm_proj(x, norm[0], norm[1], norm[2], w_in, t_cols=(FOX_WIDTH, 2 * FOX_WIDTH))
    else:
        projp = _norm_proj(x, norm[0], norm[1], norm[2], w_in)
    q, k, v, f, u = _split(projp[..., :sum(CD_SPLITS)], CD_SPLITS)
    log_f = jax.nn.log_sigmoid(f + b_f)
    if past is None:
        new_kv = kv_t.reshape(B, 2, FOX_HEADS, HEAD_DIM, T).transpose(0, 4, 1, 2, 3)
    else:
        new_kv = jnp.stack([k.reshape(B, T, FOX_HEADS, HEAD_DIM), v.reshape(B, T, FOX_HEADS, HEAD_DIM)], axis=2)
    h0 = None
    if past is None:
        F = jnp.cumsum(log_f, axis=1)
        o_fox = _attn(projp, 0, FOX_WIDTH, 2 * FOX_WIDTH, G=FOX_HEADS, R=1, tq=256, tk=512,
                      fq=F, fk=F.transpose(0, 2, 1))
    else:
        pool_kv, pool_lf, page_table, h0 = past
        t0 = page_table.shape[1] * PAGE_SIZE
        lf_all = jnp.concatenate([_gather_pages(pool_lf, page_table), log_f], axis=1)
        F = jnp.cumsum(lf_all, axis=1)
        Fq = F[:, t0:]
        fq_rows = Fq.reshape(B, T * FOX_HEADS, 1)
        fk_rows = jnp.tile(F.transpose(0, 2, 1), (1, T, 1))
        row_t = jnp.repeat(jnp.arange(T), FOX_HEADS)
        new_j = jnp.arange(LANE)
        bias_pages = fq_rows - fk_rows[:, :, :t0]
        fk_new = jnp.pad(fk_rows[:, :, t0:], ((0, 0), (0, 0), (0, LANE - T)))
        causal_new = (new_j[None, :] <= row_t[:, None]) & (new_j[None, :] < T)
        bias_new = jnp.where(causal_new[None], fq_rows - fk_new, NEG_INF)
        pad_rows = lambda a: jnp.pad(a, ((0, 0), (0, LANE - T), (0, 0)))
        o = _decode_attn(_block_diag_queries(q, FOX_HEADS), pad_rows(k), pad_rows(v), bias_new,
                         pool_t=pool_kv.transpose(0, 2, 3, 4, 1),
                         page_table=page_table, k_kind=0, v_kind=1, bias_pages=bias_pages)
        o_fox = _own_head_columns(o, T, FOX_HEADS)
    y_s5, s5_state = _s5(u, a_re, a_im, b_re, b_im, c_re, c_im, d, log_dt, glu_w, glu_b, h0, min(T, 128))
    return _out_proj(o_fox, y_s5, x, norm[3], w_out), (new_kv, log_f, s5_state)


def kernel(x_prompt, x_sample, cache_ab_nsa_kv, cache_ab_win_kv, state_ab_gla, cache_cd_fox_kv, cache_cd_fox_logf, state_cd_s5, page_table, c_prompt, c_sample, ada_w, ada_b, norm_g, ab_w_in, ab_w_out, nsa_cmp_pe, nsa_cmp_w1, nsa_cmp_w2, gla_w_alpha, gla_b_alpha, gla_norm_g, cd_w_in, cd_w_out, fox_b_f, s5_a_re, s5_a_im, s5_b_re, s5_b_im, s5_c_re, s5_c_im, s5_d, s5_log_dt, s5_glu_w, s5_glu_b, router_w, router_b, moe_w1, moe_b1, moe_w2, moe_b2, final_norm_g):
    xs = [x_prompt, x_sample]
    cs = [c_prompt, c_sample]
    new = [{}, {}]
    n_prompt = x_prompt.shape[0] * x_prompt.shape[1]
    for l in range(DEPTH):
        j = l // 2
        mods = [jnp.split(_mm(jax.nn.silu(c), ada_w[l], ada_b[l]), 6, axis=-1) for c in cs]
        for grp in range(2):
            x = xs[grp]
            sh1, sc1, g1 = mods[grp][:3]
            norm = (norm_g[l, 0], sh1, sc1, g1)
            if l % 2 == 0:
                past = None
                if grp == 1:
                    past = (cache_ab_nsa_kv[j], page_table, cache_ab_win_kv[j], state_ab_gla[j])
                xs[grp], st = _mixer_ab(x, norm, ab_w_in[j], ab_w_out[j], nsa_cmp_pe[j], nsa_cmp_w1[j],
                                        nsa_cmp_w2[j], gla_w_alpha[j], gla_b_alpha[j], gla_norm_g[j], past)
                names = ('nsa_kv', 'win_kv', 'gla')
            else:
                past = None
                if grp == 1:
                    past = (cache_cd_fox_kv[j], cache_cd_fox_logf[j], page_table, state_cd_s5[j])
                xs[grp], st = _mixer_cd(x, norm, cd_w_in[j], cd_w_out[j], fox_b_f[j], s5_a_re[j], s5_a_im[j],
                                        s5_b_re[j], s5_b_im[j], s5_c_re[j], s5_c_im[j], s5_d[j], s5_log_dt[j],
                                        s5_glu_w[j], s5_glu_b[j], past)
                names = ('fox_kv', 'fox_logf', 's5')
            for name, s in zip(names, st):
                new[grp].setdefault(name, []).append(s)
        hn2 = [_modulated_norm(xs[grp], norm_g[l, 1], mods[grp][3], mods[grp][4]).reshape(-1, D_MODEL)
               for grp in range(2)]
        ym = _moe(jnp.concatenate(hn2, axis=0), router_w[l], router_b[l], moe_w1[l], moe_b1[l], moe_w2[l], moe_b2[l])
        yms = [ym[:n_prompt], ym[n_prompt:]]
        for grp in range(2):
            xs[grp] = xs[grp] + mods[grp][5][:, None, :] * yms[grp].reshape(xs[grp].shape)
    ys = [_rmsnorm(x, final_norm_g) for x in xs]
    names = ('nsa_kv', 'win_kv', 'gla', 'fox_kv', 'fox_logf', 's5')
    outs = [jnp.stack(new[grp][name]) for grp in range(2) for name in names]
    return (ys[0], ys[1]) + tuple(outs)
```

```python
import functools
import math

import numpy as np
import jax
import jax.numpy as jnp
from jax import lax
from jax.experimental import pallas as pl
from jax.experimental.pallas import tpu as pltpu

D_MODEL = 1024
DEPTH = 2
PAGE_SIZE = 128
HEAD_DIM = 64
NSA_HEADS = 8
NSA_KV_HEADS = 2
CMP_BLK = 32
CMP_STRIDE = 16
SEL_BLK = 64
SEL_TOPN = 16
WINDOW = 512
GLA_HEADS = 4
GLA_DK = 64
GLA_DV = 128
GLA_LOWRANK = 16
GLA_TAU = 16.0
GLA_CHUNK = 64
FOX_HEADS = 8
S5_GROUPS = 32
S5_CH = 16
S5_STATE = 64
N_EXPERTS = 32
TOP_K = 4
D_FF = D_MODEL
SWIGLU_LIMIT = 7.0
SWIGLU_ALPHA = 1.702
EPS = 1e-6
NEG_INF = -1e30
BIG = 1e9
NSA_WIDTH = NSA_HEADS * HEAD_DIM
NSA_KVW = NSA_KV_HEADS * HEAD_DIM
GLA_QK = GLA_HEADS * GLA_DK
GLA_WIDTH = GLA_HEADS * GLA_DV
FOX_WIDTH = FOX_HEADS * HEAD_DIM
S5_WIDTH = S5_GROUPS * S5_CH
AB_SPLITS = (NSA_WIDTH, 6 * NSA_KVW, 3 * NSA_HEADS, GLA_QK, GLA_QK, GLA_WIDTH, GLA_WIDTH, GLA_LOWRANK)
CD_SPLITS = (FOX_WIDTH, FOX_WIDTH, FOX_WIDTH, FOX_HEADS, S5_WIDTH)

LANE = 128
SUBLANE = 8
VMEM_LIMIT = 56 * 1024 * 1024
MOE_TM = 256

F32 = jnp.float32
BF16 = jnp.bfloat16


def _round_up(n, m):
    return -(-n // m) * m


def _pick_tile(n, candidates):
    for c in candidates:
        if n % c == 0:
            return c
    return n


def _mm_kernel(x_ref, w_ref, b_ref, o_ref, wb_ref, *, act, precise):
    if precise:
        acc = jnp.dot(x_ref[...], w_ref[...], preferred_element_type=F32, precision=lax.Precision.HIGHEST)
    else:
        @pl.when(pl.program_id(1) == 0)
        def _():
            wb_ref[...] = w_ref[...].astype(BF16)

        acc = jnp.dot(x_ref[...].astype(BF16), wb_ref[...], preferred_element_type=F32)
    acc = acc + b_ref[...]
    if act == 'gelu':
        acc = jax.nn.gelu(acc)
    o_ref[...] = acc.astype(o_ref.dtype)


def _mm(x, w, b=None, act=None, out_dtype=F32, precise=False, keep_cols=False):
    M, K = x.shape
    N = w.shape[1]
    Np = _round_up(N, 2 * LANE) if N > 2 * LANE else _round_up(N, LANE)
    Mp = _round_up(M, 512) if M > 256 else _round_up(M, SUBLANE)
    tm = min(Mp, 512)
    tn = _pick_tile(Np, (512, 256, 128))
    if Np != N:
        w = jnp.pad(w, ((0, 0), (0, Np - N)))
    if b is None:
        b = jnp.zeros((N,), F32)
    b = jnp.pad(b.astype(F32), (0, Np - N)).reshape(1, Np)
    if Mp != M:
        x = jnp.pad(x, ((0, Mp - M), (0, 0)))
    out = pl.pallas_call(
        functools.partial(_mm_kernel, act=act, precise=precise),
        grid=(Np // tn, Mp // tm),
        in_specs=[pl.BlockSpec((tm, K), lambda j, i: (i, 0)),
                  pl.BlockSpec((K, tn), lambda j, i: (0, j)),
                  pl.BlockSpec((1, tn), lambda j, i: (0, j))],
        out_specs=pl.BlockSpec((tm, tn), lambda j, i: (i, j)),
        out_shape=jax.ShapeDtypeStruct((Mp, Np), out_dtype),
        scratch_shapes=[pltpu.VMEM((K, tn), BF16)],
        compiler_params=pltpu.CompilerParams(
            dimension_semantics=("arbitrary", "arbitrary"), vmem_limit_bytes=VMEM_LIMIT),
        name="mm",
    )(x, w, b)
    return out[:M] if keep_cols else out[:M, :N]


PROJ_TM = 512


def _norm_proj_kernel(x_ref, g_ref, sh_ref, sc_ref, w_ref, *rest):
    x = x_ref[0]
    y = x * lax.rsqrt(jnp.mean(x * x, axis=-1, keepdims=True) + EPS) * g_ref[...]
    hn = (y * (1 + sc_ref[0]) + sh_ref[0]).astype(BF16)
    if len(rest) == 1:
        o_ref, = rest
    else:
        wt_ref, o_ref, ot_ref = rest
        ot_ref[0] = lax.dot_general(wt_ref[...], hn, (((1,), (1,)), ((), ())), preferred_element_type=F32)
    o_ref[0] = jnp.dot(hn, w_ref[...], preferred_element_type=F32)


def _rows_and_mods(x, mods):
    B, T, D = x.shape
    if T % PROJ_TM == 0:
        return x, [m.reshape(B, 1, D) for m in mods]
    assert (B * T) % PROJ_TM == 0 or B * T <= PROJ_TM
    return x.reshape(1, B * T, D), [jnp.repeat(m, T, axis=0).reshape(1, B * T, D) for m in mods]


def _mod_spec(m, tm):
    if m.shape[1] == 1:
        return pl.BlockSpec((1, 1, m.shape[2]), lambda b, i: (b, 0, 0))
    return pl.BlockSpec((1, tm, m.shape[2]), lambda b, i: (b, i, 0))


def _norm_proj(x, g, shift, scale, w, t_cols=None):
    B, T, D = x.shape
    N = w.shape[1]
    Np = _round_up(N, 2 * LANE)
    wb = jnp.pad(w, ((0, 0), (0, Np - N))).astype(BF16)
    xr, (sh, sc) = _rows_and_mods(x, [shift, scale])
    Bm, Tm, _ = xr.shape
    tm = min(PROJ_TM, Tm)
    const2 = lambda b, i: (0, 0)
    in_specs = [pl.BlockSpec((1, tm, D), lambda b, i: (b, i, 0)), pl.BlockSpec((1, D), const2),
                _mod_spec(sh, tm), _mod_spec(sc, tm), pl.BlockSpec((D, Np), const2)]
    args = [xr, g.reshape(1, D), sh, sc, wb]
    out_specs = [pl.BlockSpec((1, tm, Np), lambda b, i: (b, i, 0))]
    out_shape = [jax.ShapeDtypeStruct((Bm, Tm, Np), F32)]
    if t_cols is not None:
        assert Bm == B
        start, width = t_cols
        in_specs.append(pl.BlockSpec((width, D), const2))
        args.append(w[:, start:start + width].T.astype(BF16))
        out_specs.append(pl.BlockSpec((1, width, tm), lambda b, i: (b, 0, i)))
        out_shape.append(jax.ShapeDtypeStruct((B, width, T), F32))
    outs = pl.pallas_call(
        _norm_proj_kernel,
        grid=(Bm, Tm // tm),
        in_specs=in_specs,
        out_specs=out_specs,
        out_shape=out_shape,
        compiler_params=pltpu.CompilerParams(
            dimension_semantics=("arbitrary", "arbitrary"), vmem_limit_bytes=VMEM_LIMIT),
        name="norm_proj",
    )(*args)
    if t_cols is None:
        return outs[0].reshape(B, T, Np)
    return outs[0].reshape(B, T, Np), outs[1]


def _out_proj_kernel(a_ref, b_ref, x_ref, g_ref, w_ref, o_ref):
    wa = a_ref.shape[2]
    y = jnp.dot(a_ref[0].astype(BF16), w_ref[:wa, :], preferred_element_type=F32)
    y = y + jnp.dot(b_ref[0].astype(BF16), w_ref[wa:, :], preferred_element_type=F32)
    o_ref[0] = x_ref[0] + g_ref[0] * y


def _out_proj(a, b, x, gate, w):
    B, T, D = x.shape
    wa, wb_ = a.shape[2], b.shape[2]
    xr, (g,) = _rows_and_mods(x, [gate])
    Bm, Tm, _ = xr.shape
    tm = min(PROJ_TM, Tm)
    row = lambda b_, i: (b_, i, 0)
    out = pl.pallas_call(
        _out_proj_kernel,
        grid=(Bm, Tm // tm),
        in_specs=[pl.BlockSpec((1, tm, wa), row), pl.BlockSpec((1, tm, wb_), row), pl.BlockSpec((1, tm, D), row),
                  _mod_spec(g, tm), pl.BlockSpec((wa + wb_, D), lambda b_, i: (0, 0))],
        out_specs=pl.BlockSpec((1, tm, D), row),
        out_shape=jax.ShapeDtypeStruct((Bm, Tm, D), F32),
        compiler_params=pltpu.CompilerParams(
            dimension_semantics=("arbitrary", "arbitrary"), vmem_limit_bytes=VMEM_LIMIT),
        name="out_proj",
    )(a.reshape(Bm, Tm, wa), b.reshape(Bm, Tm, wb_), xr, g, w.astype(BF16))
    return out.reshape(B, T, D)


def _flash_kernel(*refs, tq, tk, R, window, use_bias, use_sel):
    q_ref, k_ref, v_ref = refs[:3]
    n = 3
    if use_bias:
        qb_ref, kb_ref = refs[n:n + 2]
        n += 2
    if use_sel:
        sel_ref = refs[n]
        n += 1
    o_ref = refs[n]
    i = pl.program_id(2)
    rows = R * tq
    q = q_ref[0, 0].reshape(rows, HEAD_DIM)
    q_pos = i * tq + jnp.bitwise_and(lax.broadcasted_iota(jnp.int32, (rows, 1), 0), tq - 1)
    hi = ((i + 1) * tq + tk - 1) // tk
    lo = jnp.maximum(i * tq - window + 1, 0) // tk if window else 0
    if use_bias:
        qb = qb_ref[0, 0]
        qb = jnp.concatenate([qb] * (tk // LANE), axis=1)
    if use_sel:
        sel = sel_ref[0, 0].astype(BF16)
        n_sel = sel.shape[1]

    def body(j, carry):
        m, l, acc = carry
        start = pl.multiple_of(j * tk, tk)
        k = k_ref[0, 0, pl.ds(start, tk), :]
        v = v_ref[0, 0, pl.ds(start, tk), :]
        s = lax.dot_general(q, k, (((1,), (1,)), ((), ())), preferred_element_type=F32)
        k_pos = j * tk + lax.broadcasted_iota(jnp.int32, (1, tk), 1)
        if use_bias:
            s = s + qb - kb_ref[0, 0, j]
        s = jnp.where(k_pos <= q_pos, s, NEG_INF)
        if window:
            s = jnp.where(q_pos - k_pos < window, s, NEG_INF)
        if use_sel:
            blk_of_key = (j * tk + lax.broadcasted_iota(jnp.int32, (n_sel, tk), 1)) // SEL_BLK
            expand = jnp.where(blk_of_key == lax.broadcasted_iota(jnp.int32, (n_sel, tk), 0), 1.0, 0.0).astype(BF16)
            chosen = jnp.dot(sel, expand, preferred_element_type=F32)
            chosen = jnp.concatenate([chosen] * R, axis=0)
            s = jnp.where(chosen > 0.5, s, NEG_INF)
        m_new = jnp.maximum(m, jnp.max(s, axis=1, keepdims=True))
        p = jnp.where(s > 0.5 * NEG_INF, jnp.exp(s - m_new), 0.0)
        alpha = jnp.exp(m - m_new)
        l = alpha * l + jnp.sum(p, axis=1, keepdims=True)
        acc = alpha * acc + jnp.dot(p.astype(BF16), v, preferred_element_type=F32)
        return m_new, l, acc

    m0 = jnp.full((rows, 1), NEG_INF, F32)
    l0 = jnp.zeros((rows, 1), F32)
    a0 = jnp.zeros((rows, HEAD_DIM), F32)
    m, l, acc = lax.fori_loop(lo, hi, body, (m0, l0, a0))
    o_ref[0, 0] = (acc / l).reshape(R, tq, HEAD_DIM)


def _flash(q, k, v, *, tq, tk, window=0, qbias=None, kbias=None, sel=None):
    B, G, R, T, _ = q.shape
    grid = (B, G, T // tq)
    in_specs = [pl.BlockSpec((1, 1, R, tq, HEAD_DIM), lambda b, g, i: (b, g, 0, i, 0)),
                pl.BlockSpec((1, 1, T, HEAD_DIM), lambda b, g, i: (b, g, 0, 0)),
                pl.BlockSpec((1, 1, T, HEAD_DIM), lambda b, g, i: (b, g, 0, 0))]
    args = [q, k, v]
    if qbias is not None:
        in_specs += [pl.BlockSpec((1, 1, tq, LANE), lambda b, g, i: (b, g, i, 0)),
                     pl.BlockSpec((1, 1, T // tk, 1, tk), lambda b, g, i: (b, g, 0, 0, 0))]
        args += [qbias, kbias.reshape(B, G, T // tk, 1, tk)]
    if sel is not None:
        in_specs += [pl.BlockSpec((1, 1, tq, sel.shape[-1]), lambda b, g, i: (b, g, i, 0))]
        args += [sel]
    return pl.pallas_call(
        functools.partial(_flash_kernel, tq=tq, tk=tk, R=R, window=window,
                          use_bias=qbias is not None, use_sel=sel is not None),
        grid=grid,
        in_specs=in_specs,
        out_specs=pl.BlockSpec((1, 1, R, tq, HEAD_DIM), lambda b, g, i: (b, g, 0, i, 0)),
        out_shape=jax.ShapeDtypeStruct((B, G, R, T, HEAD_DIM), F32),
        compiler_params=pltpu.CompilerParams(
            dimension_semantics=("arbitrary", "arbitrary", "arbitrary"), vmem_limit_bytes=VMEM_LIMIT),
        name="flash",
    )(*args)


def _attn_kernel(*refs, tq, tk, G, R, window, use_bias, use_sel):
    q_ref, k_ref, v_ref = refs[:3]
    n = 3
    if use_bias:
        fq_ref, fk_ref = refs[n:n + 2]
        n += 2
    if use_sel:
        sel_ref = refs[n]
        n += 1
    o_ref, kb_ref, vb_ref = refs[n:n + 3]
    i = pl.program_id(1)

    @pl.when(i == 0)
    def _():
        kb_ref[...] = k_ref[0].astype(BF16)
        vb_ref[...] = v_ref[0].astype(BF16)

    rows = R * tq
    q_pos = i * tq + jnp.bitwise_and(lax.broadcasted_iota(jnp.int32, (rows, 1), 0), tq - 1)
    hi = ((i + 1) * tq + tk - 1) // tk
    lo = jnp.maximum(i * tq - window + 1, 0) // tk if window else 0
    outs = []
    for g in range(G):
        cols = slice(g * HEAD_DIM, (g + 1) * HEAD_DIM)
        qg = q_ref[0, :, g * R * HEAD_DIM:(g + 1) * R * HEAD_DIM] * HEAD_DIM ** -0.5
        q = jnp.concatenate([qg[:, r * HEAD_DIM:(r + 1) * HEAD_DIM] for r in range(R)], axis=0).astype(BF16)
        if use_bias:
            fq = fq_ref[0, :, g:g + 1]
        if use_sel:
            sel = sel_ref[0, g].astype(BF16)
            n_sel = sel.shape[1]

        def body(j, carry):
            m, l, acc = carry
            start = pl.multiple_of(j * tk, tk)
            k = kb_ref[pl.ds(start, tk), cols]
            v = vb_ref[pl.ds(start, tk), cols]
            s = lax.dot_general(q, k, (((1,), (1,)), ((), ())), preferred_element_type=F32)
            k_pos = j * tk + lax.broadcasted_iota(jnp.int32, (1, tk), 1)
            if use_bias:
                s = s + fq - fk_ref[0, g, j]
            s = jnp.where(k_pos <= q_pos, s, NEG_INF)
            if window:
                s = jnp.where(q_pos - k_pos < window, s, NEG_INF)
            if use_sel:
                blk_of_key = (j * tk + lax.broadcasted_iota(jnp.int32, (n_sel, tk), 1)) // SEL_BLK
                expand = jnp.where(blk_of_key == lax.broadcasted_iota(jnp.int32, (n_sel, tk), 0), 1.0, 0.0)
                chosen = jnp.dot(sel, expand.astype(BF16), preferred_element_type=F32)
                chosen = jnp.concatenate([chosen] * R, axis=0)
                s = jnp.where(chosen > 0.5, s, NEG_INF)
            m_new = jnp.maximum(m, jnp.max(s, axis=1, keepdims=True))
            p = jnp.where(s > 0.5 * NEG_INF, jnp.exp(s - m_new), 0.0)
            alpha = jnp.exp(m - m_new)
            l = alpha * l + jnp.sum(p, axis=1, keepdims=True)
            acc = alpha * acc + jnp.dot(p.astype(BF16), v, preferred_element_type=F32)
            return m_new, l, acc

        m0 = jnp.full((rows, 1), NEG_INF, F32)
        l0 = jnp.zeros((rows, 1), F32)
        a0 = jnp.zeros((rows, HEAD_DIM), F32)
        m, l, acc = lax.fori_loop(lo, hi, body, (m0, l0, a0))
        o = acc / l
        outs += [o[r * tq:(r + 1) * tq] for r in range(R)]
    o_ref[0] = jnp.concatenate(outs, axis=1)


def _attn(proj, q_col, k_col, v_col, *, G, R, tq, tk, window=0, fq=None, fk=None, sel=None):
    B, T, _ = proj.shape
    qw, kw = G * R * HEAD_DIM, G * HEAD_DIM
    assert q_col % qw == 0 and k_col % kw == 0 and v_col % kw == 0
    in_specs = [pl.BlockSpec((1, tq, qw), lambda b, i: (b, i, q_col // qw)),
                pl.BlockSpec((1, T, kw), lambda b, i: (b, 0, k_col // kw)),
                pl.BlockSpec((1, T, kw), lambda b, i: (b, 0, v_col // kw))]
    args = [proj, proj, proj]
    if fq is not None:
        in_specs += [pl.BlockSpec((1, tq, G), lambda b, i: (b, i, 0)),
                     pl.BlockSpec((1, G, T // tk, 1, tk), lambda b, i: (b, 0, 0, 0, 0))]
        args += [fq, fk.reshape(B, G, T // tk, 1, tk)]
    if sel is not None:
        in_specs += [pl.BlockSpec((1, G, tq, sel.shape[-1]), lambda b, i: (b, 0, i, 0))]
        args += [sel]
    return pl.pallas_call(
        functools.partial(_attn_kernel, tq=tq, tk=tk, G=G, R=R, window=window,
                          use_bias=fq is not None, use_sel=sel is not None),
        grid=(B, T // tq),
        in_specs=in_specs,
        out_specs=pl.BlockSpec((1, tq, qw), lambda b, i: (b, i, 0)),
        out_shape=jax.ShapeDtypeStruct((B, T, qw), F32),
        scratch_shapes=[pltpu.VMEM((T, kw), BF16), pltpu.VMEM((T, kw), BF16)],
        compiler_params=pltpu.CompilerParams(
            dimension_semantics=("arbitrary", "arbitrary"), vmem_limit_bytes=VMEM_LIMIT),
        name="attn",
    )(*args)


def _decode_attn_kernel(*refs, n_pages):
    pt_ref, q_ref = refs[0], refs[1]
    del pt_ref
    k_refs = refs[2:2 + n_pages]
    v_refs = refs[2 + n_pages:2 + 2 * n_pages]
    n = 2 + 2 * n_pages
    if n_pages:
        bp_ref = refs[n]
        n += 1
    ke_ref, ve_ref, be_ref, o_ref = refs[n:n + 4]
    nt = (((1,), (1,)), ((), ()))
    q = q_ref[0].astype(BF16)
    kw = q.shape[1]
    se = lax.dot_general(q, ke_ref[0].astype(BF16), nt, preferred_element_type=F32) + be_ref[0]
    m = jnp.max(se, axis=1, keepdims=True)
    if n_pages:
        s = jnp.concatenate([jnp.dot(q, k_refs[p][0, 0].reshape(kw, PAGE_SIZE).astype(BF16),
                                     preferred_element_type=F32) for p in range(n_pages)], axis=1) + bp_ref[0]
        m = jnp.maximum(m, jnp.max(s, axis=1, keepdims=True))
    pe = jnp.exp(se - m)
    l = jnp.sum(pe, axis=1, keepdims=True)
    o = jnp.dot(pe.astype(BF16), ve_ref[0].astype(BF16), preferred_element_type=F32)
    if n_pages:
        p = jnp.exp(s - m)
        l = l + jnp.sum(p, axis=1, keepdims=True)
        pb = p.astype(BF16)
        for pg in range(n_pages):
            o = o + lax.dot_general(pb[:, pg * PAGE_SIZE:(pg + 1) * PAGE_SIZE],
                                    v_refs[pg][0, 0].reshape(kw, PAGE_SIZE).astype(BF16), nt,
                                    preferred_element_type=F32)
    o_ref[0] = o / l


def _decode_attn(qblk, k_extra, v_extra, bias_extra, pool_t=None, page_table=None, k_kind=0, v_kind=0,
                 bias_pages=None):
    B, R, KW = qblk.shape
    NE = k_extra.shape[1]
    n_pages = 0 if pool_t is None else page_table.shape[1]
    row3 = lambda b, pt: (b, 0, 0)
    in_specs = [pl.BlockSpec((1, R, KW), row3)]
    args = [qblk]
    if n_pages:
        G = pool_t.shape[2]
        assert G * HEAD_DIM == KW
        for kind in (k_kind, v_kind):
            for p in range(n_pages):
                in_specs.append(pl.BlockSpec((1, 1, G, HEAD_DIM, PAGE_SIZE),
                                             functools.partial(lambda b, pt, p, c: (pt[b, p], c, 0, 0, 0), p=p, c=kind)))
                args.append(pool_t)
        in_specs.append(pl.BlockSpec((1, R, n_pages * PAGE_SIZE), row3))
        args.append(bias_pages)
    else:
        page_table = jnp.zeros((1, 1), jnp.int32)
    be_map = row3 if bias_extra.shape[0] == B else (lambda b, pt: (0, 0, 0))
    in_specs += [pl.BlockSpec((1, NE, KW), row3), pl.BlockSpec((1, NE, KW), row3), pl.BlockSpec((1, R, NE), be_map)]
    args += [k_extra, v_extra, bias_extra]
    grid_spec = pltpu.PrefetchScalarGridSpec(
        num_scalar_prefetch=1, grid=(B,), in_specs=in_specs,
        out_specs=pl.BlockSpec((1, R, KW), row3))
    return pl.pallas_call(
        functools.partial(_decode_attn_kernel, n_pages=n_pages),
        grid_spec=grid_spec,
        out_shape=jax.ShapeDtypeStruct((B, R, KW), F32),
        compiler_params=pltpu.CompilerParams(dimension_semantics=("arbitrary",), vmem_limit_bytes=VMEM_LIMIT),
        name="decode_attn",
    )(page_table, *args)


def _block_diag_queries(q, G):
    B, T, W = q.shape
    H = W // HEAD_DIM
    own = (jnp.arange(H)[:, None] // (H // G) == jnp.arange(G)[None, :]).astype(F32)
    qs = (q * HEAD_DIM ** -0.5).reshape(B, T, H, 1, HEAD_DIM) * own[None, None, :, :, None]
    return qs.reshape(B, T * H, G * HEAD_DIM)


def _own_head_columns(o, T, G):
    B, R, _ = o.shape
    H = R // T
    own = (jnp.arange(H)[:, None] // (H // G) == jnp.arange(G)[None, :]).astype(F32)
    o5 = o.reshape(B, T, H, G, HEAD_DIM) * own[None, None, :, :, None]
    return jnp.sum(o5, axis=3).reshape(B, T, H * HEAD_DIM)


def _moe_kernel(be_ref, nb_ref, x_ref, w1_ref, b1_ref, w2_ref, b2_ref, o_ref, w1b_ref, w2b_ref):
    i = pl.program_id(0)
    prev = be_ref[jnp.maximum(i - 1, 0)]

    @pl.when(jnp.logical_or(i == 0, be_ref[i] != prev))
    def _():
        w1b_ref[...] = w1_ref[0].astype(BF16)
        w2b_ref[...] = w2_ref[0].astype(BF16)

    @pl.when(i < nb_ref[0])
    def _():
        gu = jnp.dot(x_ref[...].astype(BF16), w1b_ref[...], preferred_element_type=F32) + b1_ref[0]
        g = jnp.minimum(gu[:, :D_FF], SWIGLU_LIMIT)
        up = jnp.clip(gu[:, D_FF:], -SWIGLU_LIMIT, SWIGLU_LIMIT)
        h = (up + 1) * g * jax.nn.sigmoid(SWIGLU_ALPHA * g)
        o_ref[...] = jnp.dot(h.astype(BF16), w2b_ref[...], preferred_element_type=F32) + b2_ref[0]

    @pl.when(i >= nb_ref[0])
    def _():
        o_ref[...] = jnp.zeros_like(o_ref)


def _moe_experts(xs, blk_e, n_used, w1, b1, w2, b2):
    n_blocks = xs.shape[0] // MOE_TM
    grid_spec = pltpu.PrefetchScalarGridSpec(
        num_scalar_prefetch=2,
        grid=(n_blocks,),
        in_specs=[pl.BlockSpec((MOE_TM, D_MODEL), lambda i, be, nb: (i, 0)),
                  pl.BlockSpec((1, D_MODEL, 2 * D_FF), lambda i, be, nb: (be[i], 0, 0)),
                  pl.BlockSpec((1, 1, 2 * D_FF), lambda i, be, nb: (be[i], 0, 0)),
                  pl.BlockSpec((1, D_FF, D_MODEL), lambda i, be, nb: (be[i], 0, 0)),
                  pl.BlockSpec((1, 1, D_MODEL), lambda i, be, nb: (be[i], 0, 0))],
        out_specs=pl.BlockSpec((MOE_TM, D_MODEL), lambda i, be, nb: (i, 0)),
        scratch_shapes=[pltpu.VMEM((D_MODEL, 2 * D_FF), BF16), pltpu.VMEM((D_FF, D_MODEL), BF16)],
    )
    return pl.pallas_call(
        _moe_kernel,
        grid_spec=grid_spec,
        out_shape=jax.ShapeDtypeStruct((n_blocks * MOE_TM, D_MODEL), F32),
        compiler_params=pltpu.CompilerParams(
            dimension_semantics=("arbitrary",), vmem_limit_bytes=VMEM_LIMIT),
        name="moe_experts",
    )(blk_e, n_used, xs, w1, b1.reshape(N_EXPERTS, 1, 2 * D_FF), w2, b2.reshape(N_EXPERTS, 1, D_MODEL))


def _moe_combine_kernel(*refs):
    y_refs, g_ref, o_ref = refs[:TOP_K], refs[TOP_K], refs[TOP_K + 1]
    g = g_ref[...]
    acc = g[:, 0:1] * y_refs[0][...]
    for k in range(1, TOP_K):
        acc = acc + g[:, k:k + 1] * y_refs[k][...]
    o_ref[...] = acc


def _moe_combine(yg, gate):
    N = gate.shape[0]
    tn = _pick_tile(N, (512, 256, 128, 64, 32, 16, 8))
    return pl.pallas_call(
        _moe_combine_kernel,
        grid=(N // tn,),
        in_specs=[pl.BlockSpec((tn, D_MODEL), functools.partial(lambda i, k: (k * (N // tn) + i, 0), k=k))
                  for k in range(TOP_K)] + [pl.BlockSpec((tn, TOP_K), lambda i: (i, 0))],
        out_specs=pl.BlockSpec((tn, D_MODEL), lambda i: (i, 0)),
        out_shape=jax.ShapeDtypeStruct((N, D_MODEL), F32),
        compiler_params=pltpu.CompilerParams(dimension_semantics=("arbitrary",), vmem_limit_bytes=VMEM_LIMIT),
        name="moe_combine",
    )(*([yg] * TOP_K), gate)


def _moe(xf, w_r, b_r, w1, b1, w2, b2):
    N = xf.shape[0]
    n_rows = N * TOP_K
    logits = _mm(xf, w_r, b_r, precise=True)
    top_v, top_i = lax.top_k(logits, TOP_K)
    gate = jax.nn.softmax(top_v, axis=-1)
    flat_e = top_i.reshape(-1)
    order = jnp.argsort(flat_e)
    sorted_e = flat_e[order]
    counts = jnp.bincount(flat_e, length=N_EXPERTS)
    padded = (counts + MOE_TM - 1) // MOE_TM * MOE_TM
    pad_end = jnp.cumsum(padded)
    pad_start = pad_end - padded
    start = jnp.cumsum(counts) - counts
    dest = (pad_start[sorted_e] + jnp.arange(n_rows) - start[sorted_e]).astype(jnp.int32)
    n_blocks = -(-(n_rows + N_EXPERTS * (MOE_TM - 1)) // MOE_TM)
    n_used = (pad_end[-1] // MOE_TM).astype(jnp.int32)
    blk = jnp.minimum(jnp.arange(n_blocks), n_used - 1) * MOE_TM
    blk_e = jnp.minimum(jnp.sum(pad_end[None, :] <= blk[:, None], axis=1), N_EXPERTS - 1).astype(jnp.int32)
    row_e = jnp.repeat(blk_e, MOE_TM)
    off = jnp.arange(n_blocks * MOE_TM) - pad_start[row_e]
    src = jnp.clip(start[row_e] + off, 0, n_rows - 1)
    row_tok = jnp.where(off < counts[row_e], (order // TOP_K)[src], 0).astype(jnp.int32)
    xs = xf[row_tok]
    ys = _moe_experts(xs, blk_e, n_used.reshape(1), w1, b1, w2, b2)
    pos = dest[jnp.argsort(order)]
    return _moe_combine(ys[pos.reshape(N, TOP_K).T.reshape(-1)], gate)


def _rmsnorm(x, g):
    y = x * lax.rsqrt(jnp.mean(x * x, axis=-1, keepdims=True) + EPS)
    return y * g


def _modulated_norm(x, g, shift, scale):
    return _rmsnorm(x, g) * (1 + scale[:, None, :]) + shift[:, None, :]


def _split(x, sizes):
    return jnp.split(x, np.cumsum(sizes)[:-1].tolist(), axis=-1)


def _gather_pages(pool, page_table):
    g = pool[page_table]
    return g.reshape((g.shape[0], g.shape[1] * g.shape[2]) + g.shape[3:])


def _attend(q, k, v, mask, bias=None):
    B, Q, H, D = q.shape
    G = k.shape[2]
    s = jnp.einsum('bqgnd,bkgd->bgnqk', q.reshape(B, Q, G, H // G, D), k).astype(F32) * D ** -0.5
    if bias is not None:
        s = s + bias
    p = jax.nn.softmax(jnp.where(mask, s, NEG_INF), axis=-1).astype(v.dtype)
    return jnp.einsum('bgnqk,bkgd->bqgnd', p, v).reshape(B, Q, H, D)


def _window_mask(q_pos, k_pos):
    d = q_pos[:, None] - k_pos[None, :]
    return (d >= 0) & (d < WINDOW) & (k_pos[None, :] >= 0)


def _to_heads(x, G):
    B, T, _ = x.shape
    return x.astype(BF16).reshape(B, T, G, HEAD_DIM).transpose(0, 2, 1, 3)


def _q_to_heads(q, G):
    B, T, W = q.shape
    R = W // HEAD_DIM // G
    return (q * HEAD_DIM ** -0.5).astype(BF16).reshape(B, T, G, R, HEAD_DIM).transpose(0, 2, 3, 1, 4)


def _from_heads(o):
    B, G, R, T, D = o.shape
    return o.transpose(0, 3, 1, 2, 4).reshape(B, T, G * R * D)


CMP_CHUNK_W = CMP_STRIDE * HEAD_DIM
CMP_HID = 256


def _cmp_chunk_proj(tokens, w1):
    assert CMP_BLK == 2 * CMP_STRIDE
    S, T, G, D = tokens.shape
    n = T // CMP_STRIDE
    ch = tokens.reshape(S, n, CMP_STRIDE, G, D).transpose(0, 1, 3, 2, 4).reshape(S * n * G, CMP_CHUNK_W)
    wcat = jnp.concatenate([w1[:CMP_CHUNK_W], w1[CMP_CHUNK_W:]], axis=1)
    return _mm(ch, wcat).reshape(S, n, G, 2 * CMP_HID)


CMP_PAGES = 32


def _cmp_pages_kernel(p_ref, w_ref, o_ref, mt_ref):
    P = p_ref.shape[0]
    chunks = PAGE_SIZE // CMP_STRIDE
    for kind in range(2):
        for g in range(NSA_KV_HEADS):
            for p in range(P):
                mt_ref[p * PAGE_SIZE:(p + 1) * PAGE_SIZE, :] = p_ref[p, kind, g].T
            acc = jnp.zeros((P * chunks, 2 * CMP_HID), F32)
            for j in range(CMP_STRIDE):
                rows = mt_ref[pl.ds(j, P * chunks, stride=CMP_STRIDE), :].astype(BF16)
                acc = acc + jnp.dot(rows, w_ref[kind, j * HEAD_DIM:(j + 1) * HEAD_DIM, :], preferred_element_type=F32)
            o_ref[kind, :, g * 2 * CMP_HID:(g + 1) * 2 * CMP_HID] = acc


def _cmp_pages_proj(pool_t, w1):
    n_phys = pool_t.shape[0]
    chunks = PAGE_SIZE // CMP_STRIDE
    assert n_phys % CMP_PAGES == 0
    wcat = jnp.concatenate([w1[:, :CMP_CHUNK_W], w1[:, CMP_CHUNK_W:]], axis=2).astype(BF16)
    width = NSA_KV_HEADS * 2 * CMP_HID
    out = pl.pallas_call(
        _cmp_pages_kernel,
        grid=(n_phys // CMP_PAGES,),
        in_specs=[pl.BlockSpec((CMP_PAGES, 2, NSA_KV_HEADS, HEAD_DIM, PAGE_SIZE), lambda i: (i, 0, 0, 0, 0)),
                  pl.BlockSpec(wcat.shape, lambda i: (0, 0, 0))],
        out_specs=pl.BlockSpec((2, CMP_PAGES * chunks, width), lambda i: (0, i, 0)),
        out_shape=jax.ShapeDtypeStruct((2, n_phys * chunks, width), F32),
        scratch_shapes=[pltpu.VMEM((CMP_PAGES * PAGE_SIZE, HEAD_DIM), F32)],
        compiler_params=pltpu.CompilerParams(dimension_semantics=("arbitrary",), vmem_limit_bytes=VMEM_LIMIT),
        name="cmp_pages",
    )(pool_t, wcat)
    return out.reshape(2, n_phys, chunks, NSA_KV_HEADS, 2 * CMP_HID)


def _cmp_out_kernel(a_ref, b_ref, bias_ref, w_ref, o_ref):
    hid = jax.nn.gelu(a_ref[...] + b_ref[...] + bias_ref[...])
    o_ref[...] = jnp.dot(hid.astype(BF16), w_ref[...].astype(BF16), preferred_element_type=F32)


def _cmp_finish(proj, pe, w1, w2):
    B, n, G, _ = proj.shape
    first = proj[:, :-1, :, :CMP_HID].reshape(-1, CMP_HID)
    second = proj[:, 1:, :, CMP_HID:].reshape(-1, CMP_HID)
    bias = _mm(pe.reshape(1, CMP_BLK * HEAD_DIM), w1)
    rows = first.shape[0]
    rows_p = _round_up(rows, 512)
    pad = lambda a: jnp.pad(a, ((0, rows_p - rows), (0, 0)))
    w2p = jnp.pad(w2, ((0, 0), (0, LANE - HEAD_DIM)))
    out = pl.pallas_call(
        _cmp_out_kernel,
        grid=(rows_p // 512,),
        in_specs=[pl.BlockSpec((512, CMP_HID), lambda i: (i, 0)), pl.BlockSpec((512, CMP_HID), lambda i: (i, 0)),
                  pl.BlockSpec((1, CMP_HID), lambda i: (0, 0)), pl.BlockSpec((CMP_HID, LANE), lambda i: (0, 0))],
        out_specs=pl.BlockSpec((512, LANE), lambda i: (i, 0)),
        out_shape=jax.ShapeDtypeStruct((rows_p, LANE), F32),
        compiler_params=pltpu.CompilerParams(dimension_semantics=("arbitrary",), vmem_limit_bytes=VMEM_LIMIT),
        name="cmp_out",
    )(pad(first), pad(second), bias, w2p)
    return out[:rows, :HEAD_DIM].reshape(B, n - 1, G, HEAD_DIM)


def _cmp_attention(q, kc, vc, q_pos):
    B, Q, H, D = q.shape
    N, G = kc.shape[1], kc.shape[2]
    c_end = jnp.arange(N) * CMP_STRIDE + CMP_BLK
    valid = c_end[None, :] <= q_pos[:, None] + 1
    s = jnp.einsum('bqgnd,bcgd->bgnqc', q.reshape(B, Q, G, H // G, D), kc).astype(F32) * D ** -0.5
    p = jax.nn.softmax(jnp.where(valid, s, NEG_INF), axis=-1) * valid
    o = jnp.einsum('bgnqc,bcgd->bqgnd', p.astype(vc.dtype), vc).reshape(B, Q, H, D)
    return o, p


def _cmp_to_sel(n_cmp, n_sel):
    c0 = jnp.arange(n_cmp)[:, None] * CMP_STRIDE
    s0 = jnp.arange(n_sel)[None, :] * SEL_BLK
    ov = jnp.minimum(c0 + CMP_BLK, s0 + SEL_BLK) - jnp.maximum(c0, s0)
    return jnp.maximum(ov, 0).astype(F32) / CMP_BLK


def _selection_scores(p_cmp, q_pos, n_sel):
    imp = jnp.einsum('bgnqc,cs->bgqs', p_cmp, _cmp_to_sel(p_cmp.shape[-1], n_sel))
    blk = jnp.arange(n_sel)[None, :]
    cur = (q_pos // SEL_BLK)[:, None]
    valid = blk * SEL_BLK <= q_pos[:, None]
    forced = (blk == 0) | (blk == cur) | (blk == cur - 1)
    return jnp.where(valid, jnp.where(forced, BIG, imp), -BIG)


def _selection_mask(score):
    n_sel = score.shape[-1]
    a = score[..., :, None]
    b = score[..., None, :]
    lower = jnp.arange(n_sel)[None, :] < jnp.arange(n_sel)[:, None]
    beats = (b > a) | ((b == a) & lower)
    rank = jnp.sum(beats, axis=-1)
    return (rank < min(SEL_TOPN, n_sel)).astype(F32)


def _selected_attention_gather(q, k, v, p_cmp, q_pos):
    B, Q, H, D = q.shape
    T_all, G = k.shape[1], k.shape[2]
    n_sel = -(-T_all // SEL_BLK)
    score = _selection_scores(p_cmp, q_pos, n_sel)
    _, idx = lax.top_k(score, min(SEL_TOPN, n_sel))
    tok = (idx[..., None] * SEL_BLK + jnp.arange(SEL_BLK)).reshape(B, G, Q, -1)
    pad = ((0, 0), (0, n_sel * SEL_BLK - T_all), (0, 0), (0, 0))
    kt = jnp.pad(k, pad).transpose(0, 2, 1, 3)
    vt = jnp.pad(v, pad).transpose(0, 2, 1, 3)
    b_ix = jnp.arange(B)[:, None, None]
    g_ix = jnp.arange(G)[None, :, None]
    L = tok.shape[3]
    flat = tok.reshape(B, G, Q * L)
    ks = kt[b_ix, g_ix, flat].reshape(B, G, Q, L, D)
    vs = vt[b_ix, g_ix, flat].reshape(B, G, Q, L, D)
    s = jnp.einsum('bqgnd,bgqld->bgnql', q.reshape(B, Q, G, H // G, D), ks).astype(F32) * D ** -0.5
    mask = (tok <= q_pos[None, None, :, None])[:, :, None]
    p = jax.nn.softmax(jnp.where(mask, s, NEG_INF), axis=-1)
    return jnp.einsum('bgnql,bgqld->bqgnd', p, vs).reshape(B, Q, H, D)


def _gla_kernel(q_ref, k_ref, v_ref, gr_ref, tail_ref, wa_ref, ba_ref, g_ref, s0_ref, o_ref, sT_ref, st_ref,
                *, C, n_chunks, n_valid):
    @pl.when(pl.program_id(1) == 0)
    def _():
        st_ref[...] = s0_ref[0]

    tri = (lax.broadcasted_iota(jnp.int32, (C, C), 0) >= lax.broadcasted_iota(jnp.int32, (C, C), 1))
    tri_f = jnp.where(tri, 1.0, 0.0)
    for c in range(n_chunks):
        rows = pl.ds(c * C, C)
        za = jnp.dot(tail_ref[0, rows, :].astype(BF16), wa_ref[...], preferred_element_type=F32) + ba_ref[...]
        log_a = jax.nn.log_sigmoid(za) / GLA_TAU
        q = q_ref[0, rows, :] * GLA_DK ** -0.5
        k = k_ref[0, rows, :]
        v = v_ref[0, rows, :]
        if n_valid < C:
            live = lax.broadcasted_iota(jnp.int32, (C, 1), 0) < n_valid
            log_a = jnp.where(live, log_a, 0.0)
            q, k, v = jnp.where(live, q, 0.0), jnp.where(live, k, 0.0), jnp.where(live, v, 0.0)
        b = jnp.dot(tri_f, log_a, preferred_element_type=F32, precision=lax.Precision.HIGHEST)
        b_last = b[C - 1:C, :]
        q_dec = (q * jnp.exp(b)).astype(BF16)
        k_inv = (k * jnp.exp(-b)).astype(BF16)
        k_end = (k * jnp.exp(b_last - b)).astype(BF16)
        decay = jnp.exp(b_last)
        v = v.astype(BF16)
        gr = gr_ref[0, rows, :]
        outs = []
        for h in range(GLA_HEADS):
            kh = slice(h * GLA_DK, (h + 1) * GLA_DK)
            vh = slice(h * GLA_DV, (h + 1) * GLA_DV)
            attn = lax.dot_general(q_dec[:, kh], k_inv[:, kh], (((1,), (1,)), ((), ())), preferred_element_type=F32)
            attn = jnp.where(tri, attn, 0.0).astype(BF16)
            sT = st_ref[h]
            o = jnp.dot(attn, v[:, vh], preferred_element_type=F32)
            o = o + lax.dot_general(q_dec[:, kh], sT.astype(BF16), (((1,), (1,)), ((), ())), preferred_element_type=F32)
            st_ref[h] = decay[:, kh] * sT + lax.dot_general(v[:, vh], k_end[:, kh], (((0,), (0,)), ((), ())),
                                                            preferred_element_type=F32)
            o = o * lax.rsqrt(jnp.mean(o * o, axis=-1, keepdims=True) + EPS) * g_ref[...]
            outs.append(o * jax.nn.silu(gr[:, vh]))
        o_ref[0, rows, :] = jnp.concatenate(outs, axis=1)

    @pl.when(pl.program_id(1) == pl.num_programs(1) - 1)
    def _():
        sT_ref[0] = st_ref[...]


def _gla(proj, cols, gla_wa, gla_ba, gla_g, s0, C, n_chunks, n_valid):
    B, T, _ = proj.shape
    tt = C * n_chunks
    s0T = s0.transpose(0, 1, 3, 2)
    cq, ck, cv, cr, ca = cols
    tail0 = ca // GLA_QK * GLA_QK
    assert cq % GLA_QK == 0 and ck % GLA_QK == 0 and cv % GLA_WIDTH == 0 and cr % GLA_WIDTH == 0
    assert ca + GLA_LOWRANK <= tail0 + GLA_QK
    wa_pad = jnp.zeros((GLA_QK, GLA_QK), F32).at[ca - tail0:ca - tail0 + GLA_LOWRANK].set(gla_wa).astype(BF16)
    col_spec = lambda width, col: pl.BlockSpec((1, tt, width), lambda b, i: (b, i, col // width))
    v_spec = pl.BlockSpec((1, tt, GLA_WIDTH), lambda b, i: (b, i, 0))
    st_spec = pl.BlockSpec((1, GLA_HEADS, GLA_DV, GLA_DK), lambda b, i: (b, 0, 0, 0))
    const2 = lambda b, i: (0, 0)
    o, sT = pl.pallas_call(
        functools.partial(_gla_kernel, C=C, n_chunks=n_chunks, n_valid=n_valid),
        grid=(B, T // tt),
        in_specs=[col_spec(GLA_QK, cq), col_spec(GLA_QK, ck), col_spec(GLA_WIDTH, cv), col_spec(GLA_WIDTH, cr),
                  col_spec(GLA_QK, tail0), pl.BlockSpec((GLA_QK, GLA_QK), const2), pl.BlockSpec((1, GLA_QK), const2),
                  pl.BlockSpec((1, GLA_DV), const2), st_spec],
        out_specs=[v_spec, st_spec],
        out_shape=[jax.ShapeDtypeStruct((B, T, GLA_WIDTH), F32),
                   jax.ShapeDtypeStruct((B, GLA_HEADS, GLA_DV, GLA_DK), F32)],
        scratch_shapes=[pltpu.VMEM((GLA_HEADS, GLA_DV, GLA_DK), F32)],
        compiler_params=pltpu.CompilerParams(
            dimension_semantics=("arbitrary", "arbitrary"), vmem_limit_bytes=VMEM_LIMIT),
        name="gla",
    )(proj, proj, proj, proj, proj, wa_pad, gla_ba.reshape(1, GLA_QK), gla_g.reshape(1, GLA_DV), s0T)
    return o, sT.transpose(0, 1, 3, 2)


S5_NK = 4
S5_GPK = S5_GROUPS // S5_NK
S5_HALF = S5_GPK * S5_STATE


def _s5_kernel(u_ref, wb_ref, wc_ref, coef_ref, d_ref, gw_ref, gb_ref, h0_ref, y_ref, hfin_ref, hst_ref, xb_ref,
               *, BT, Tc):
    @pl.when(pl.program_id(0) == 0)
    def _():
        hst_ref[...] = h0_ref[...]

    u = u_ref[...]
    ys = []
    for k in range(S5_NK):
        bu = jnp.dot(u[:, k * LANE:(k + 1) * LANE].astype(BF16), wb_ref[k], preferred_element_type=F32)
        bre, bim = bu[:, :S5_HALF], bu[:, S5_HALF:]
        ar, ai = coef_ref[k, 0:1, :], coef_ref[k, 1:2, :]
        cr, ci = coef_ref[k, 2:3, :], coef_ref[k, 3:4, :]
        xb_ref[:, :S5_HALF] = cr * bre - ci * bim
        xb_ref[:, S5_HALF:] = cr * bim + ci * bre

        def step(t, carry):
            hr, hi = carry
            rows = pl.ds(pl.multiple_of(t * BT, BT), BT)
            hr2 = ar * hr - ai * hi + xb_ref[rows, :S5_HALF]
            hi2 = ar * hi + ai * hr + xb_ref[rows, S5_HALF:]
            xb_ref[rows, :S5_HALF] = hr2
            xb_ref[rows, S5_HALF:] = hi2
            return hr2, hi2

        hr, hi = lax.fori_loop(0, Tc, step, (hst_ref[k, :, :S5_HALF], hst_ref[k, :, S5_HALF:]),
                               unroll=min(Tc, 8))
        hst_ref[k, :, :S5_HALF] = hr
        hst_ref[k, :, S5_HALF:] = hi
        ys.append(jnp.dot(xb_ref[...].astype(BF16), wc_ref[k], preferred_element_type=F32))
    y = jax.nn.gelu(jnp.concatenate(ys, axis=1) + d_ref[...] * u)
    z = jnp.dot(y.astype(BF16), gw_ref[...].astype(BF16), preferred_element_type=F32) + gb_ref[...]
    y_ref[...] = y * jax.nn.sigmoid(z)

    @pl.when(pl.program_id(0) == pl.num_programs(0) - 1)
    def _():
        hfin_ref[...] = hst_ref[...]


def _s5(u, a_re, a_im, b_re, b_im, c_re, c_im, d, log_dt, glu_w, glu_b, h0, Tc):
    B, T, _ = u.shape
    dt = jnp.exp(log_dt)[:, None]
    mag = jnp.exp(a_re * dt)
    ab_re, ab_im = mag * jnp.cos(a_im * dt), mag * jnp.sin(a_im * dt)
    den = a_re * a_re + a_im * a_im
    coef_re = ((ab_re - 1) * a_re + ab_im * a_im) / den
    coef_im = (ab_im * a_re - (ab_re - 1) * a_im) / den
    coefs = jnp.stack([ab_re, ab_im, coef_re, coef_im], axis=0).reshape(4, S5_NK, S5_HALF).transpose(1, 0, 2)
    eye = jnp.eye(S5_GPK, dtype=F32)

    def in_weights(bm):
        bk = bm.reshape(S5_NK, S5_GPK, S5_STATE, S5_CH)
        return jnp.einsum('kgpc,gh->kgchp', bk, eye).reshape(S5_NK, S5_GPK * S5_CH, S5_HALF)

    def out_weights(cm):
        ck = cm.reshape(S5_NK, S5_GPK, S5_CH, S5_STATE)
        return jnp.einsum('kgcp,gh->kgphc', ck, eye).reshape(S5_NK, S5_HALF, S5_GPK * S5_CH)

    wb = jnp.concatenate([in_weights(b_re), in_weights(b_im)], axis=2).astype(BF16)
    wc = jnp.concatenate([out_weights(c_re), -out_weights(c_im)], axis=1).astype(BF16)
    if h0 is None:
        hs0 = jnp.zeros((S5_NK, B, 2 * S5_HALF), F32)
    else:
        hs0 = h0.reshape(B, S5_NK, S5_HALF, 2).transpose(1, 0, 3, 2).reshape(S5_NK, B, 2 * S5_HALF)
    ut = u.transpose(1, 0, 2).reshape(T * B, S5_WIDTH)
    rows = Tc * B
    const2 = lambda i: (0, 0)
    const3 = lambda i: (0, 0, 0)
    y, hfin = pl.pallas_call(
        functools.partial(_s5_kernel, BT=B, Tc=Tc),
        grid=(T // Tc,),
        in_specs=[pl.BlockSpec((rows, S5_WIDTH), lambda i: (i, 0)),
                  pl.BlockSpec(wb.shape, const3), pl.BlockSpec(wc.shape, const3), pl.BlockSpec(coefs.shape, const3),
                  pl.BlockSpec((1, S5_WIDTH), const2), pl.BlockSpec((S5_WIDTH, S5_WIDTH), const2),
                  pl.BlockSpec((1, S5_WIDTH), const2), pl.BlockSpec(hs0.shape, const3)],
        out_specs=[pl.BlockSpec((rows, S5_WIDTH), lambda i: (i, 0)), pl.BlockSpec(hs0.shape, const3)],
        out_shape=[jax.ShapeDtypeStruct((T * B, S5_WIDTH), F32), jax.ShapeDtypeStruct(hs0.shape, F32)],
        scratch_shapes=[pltpu.VMEM(hs0.shape, F32), pltpu.VMEM((rows, 2 * S5_HALF), F32)],
        compiler_params=pltpu.CompilerParams(dimension_semantics=("arbitrary",), vmem_limit_bytes=VMEM_LIMIT),
        name="s5",
    )(ut, wb, wc, coefs, d.reshape(1, S5_WIDTH), glu_w, glu_b.reshape(1, S5_WIDTH), hs0)
    y = y.reshape(T, B, S5_WIDTH).transpose(1, 0, 2)
    hfin = hfin.reshape(S5_NK, B, 2, S5_HALF).transpose(1, 0, 3, 2).reshape(B, S5_GROUPS, S5_STATE, 2)
    return y, hfin


AB_ORDER = (0, 5, 6, 1, 3, 4, 2, 7)
AB_START = dict(zip(AB_ORDER, np.cumsum([0] + [AB_SPLITS[s] for s in AB_ORDER[:-1]]).tolist()))
AB_PERM = np.concatenate([np.arange(AB_SPLITS[s]) + sum(AB_SPLITS[:s]) for s in AB_ORDER])


def _mixer_ab(x, norm, w_in, w_out, cmp_pe, cmp_w1, cmp_w2, gla_wa, gla_ba, gla_g, past):
    B, T, _ = x.shape
    if past is None:
        projp, kv_t = _norm_proj(x, norm[0], norm[1], norm[2], w_in[:, AB_PERM], t_cols=(AB_START[1], AB_SPLITS[1]))
    else:
        projp = _norm_proj(x, norm[0], norm[1], norm[2], w_in[:, AB_PERM])
    seg = lambda s: projp[..., AB_START[s]:AB_START[s] + AB_SPLITS[s]]
    q, kv6, gate_logit = seg(0), seg(1), seg(2)
    kv6 = kv6.reshape(B, T, 6, NSA_KV_HEADS, HEAD_DIM)
    new_nsa, new_win = kv6[:, :, :4], kv6[:, :, 4:]
    s0 = jnp.zeros((B, GLA_HEADS, GLA_DK, GLA_DV), F32)
    if past is None:
        t0 = 0
        kv_full, kv_win = new_nsa, new_win
        cmp_proj = [_cmp_chunk_proj(new_nsa[:, :T // CMP_STRIDE * CMP_STRIDE, c], cmp_w1[c]) for c in range(2)]
    else:
        pool, page_table, win_buf, s0 = past
        pool_t = pool.transpose(0, 2, 3, 4, 1)
        n_pages = page_table.shape[1]
        t0 = n_pages * PAGE_SIZE
        assert PAGE_SIZE % CMP_STRIDE == 0 and T < CMP_STRIDE
        pages_proj = _cmp_pages_proj(pool_t, cmp_w1)
        cmp_proj = [pages_proj[c][page_table].reshape(
            B, n_pages * (PAGE_SIZE // CMP_STRIDE), NSA_KV_HEADS, 2 * CMP_HID) for c in range(2)]
        kv_win = jnp.concatenate([win_buf, new_win], axis=1)
    q_pos = t0 + jnp.arange(T)
    q4 = q.reshape(B, T, NSA_HEADS, HEAD_DIM)
    kc = _cmp_finish(cmp_proj[0], cmp_pe[0], cmp_w1[0], cmp_w2[0])
    vc = _cmp_finish(cmp_proj[1], cmp_pe[1], cmp_w1[1], cmp_w2[1])
    o_cmp, p_cmp = _cmp_attention(q4, kc, vc, q_pos)
    if past is None:
        n_sel = -(-T // SEL_BLK)
        sel = _selection_mask(_selection_scores(p_cmp, q_pos, n_sel))
        kv_col = lambda kind: AB_START[1] + kind * NSA_KVW
        hpg = NSA_HEADS // NSA_KV_HEADS
        o_slc = _attn(projp, 0, kv_col(2), kv_col(3), G=NSA_KV_HEADS, R=hpg, tq=128, tk=512, sel=sel)
        o_win = _attn(projp, 0, kv_col(4), kv_col(5), G=NSA_KV_HEADS, R=hpg, tq=128, tk=512, window=WINDOW)
        o_slc = o_slc.reshape(B, T, NSA_HEADS, HEAD_DIM)
        o_win = o_win.reshape(B, T, NSA_HEADS, HEAD_DIM)
        n_win = min(WINDOW, T)
        rows_out = lambda a, kinds: a.reshape(B, kinds, NSA_KV_HEADS, HEAD_DIM, -1).transpose(0, 4, 1, 2, 3)
        new_nsa = rows_out(kv_t[:, :4 * NSA_KVW], 4)
        win_state = rows_out(kv_t[:, 4 * NSA_KVW:, T - n_win:], 2)
    else:
        assert t0 % SEL_BLK == 0 and T <= SEL_BLK
        n_sel = t0 // SEL_BLK + 1
        hpg = NSA_HEADS // NSA_KV_HEADS
        sel = _selection_mask(_selection_scores(p_cmp, q_pos, n_sel))
        sel_rows = jnp.repeat(sel.transpose(0, 2, 1, 3), hpg, axis=2).reshape(B, T * NSA_HEADS, n_sel)
        row_t = jnp.repeat(jnp.arange(T), NSA_HEADS)
        new_j = jnp.arange(LANE)
        bias_pages = jnp.where(jnp.repeat(sel_rows[:, :, :n_sel - 1], SEL_BLK, axis=2) > 0.5, 0.0, NEG_INF)
        causal_new = (new_j[None, :] <= row_t[:, None]) & (new_j[None, :] < T)
        bias_new = jnp.where((sel_rows[:, :, n_sel - 1:] > 0.5) & causal_new[None], 0.0, NEG_INF)
        pad_rows = lambda a, n: jnp.pad(a.reshape(B, a.shape[1], NSA_KVW), ((0, 0), (0, n - a.shape[1]), (0, 0)))
        qblk = _block_diag_queries(q, NSA_KV_HEADS)
        o_slc = _decode_attn(qblk, pad_rows(new_nsa[:, :, 2], LANE), pad_rows(new_nsa[:, :, 3], LANE), bias_new,
                             pool_t=pool_t, page_table=page_table, k_kind=2, v_kind=3, bias_pages=bias_pages)
        o_slc = _own_head_columns(o_slc, T, NSA_KV_HEADS).reshape(B, T, NSA_HEADS, HEAD_DIM)
        wb = win_buf.shape[1]
        n_win = _round_up(wb + T, LANE)
        k_pos = t0 - wb + jnp.arange(n_win)
        in_win = _window_mask(q_pos, k_pos) & (jnp.arange(n_win) < wb + T)[None, :]
        bias_win = jnp.where(in_win, 0.0, NEG_INF)[row_t][None]
        o_win = _decode_attn(qblk, pad_rows(kv_win[:, :, 0], n_win), pad_rows(kv_win[:, :, 1], n_win), bias_win)
        o_win = _own_head_columns(o_win, T, NSA_KV_HEADS).reshape(B, T, NSA_HEADS, HEAD_DIM)
        win_state = kv_win[:, -wb:]
    g = jax.nn.sigmoid(gate_logit).reshape(B, T, NSA_HEADS, 3, 1)
    o_nsa = (g[:, :, :, 0] * o_cmp + g[:, :, :, 1] * o_slc + g[:, :, :, 2] * o_win).reshape(B, T, NSA_WIDTH)
    gla_cols = (AB_START[3], AB_START[4], AB_START[5], AB_START[6], AB_START[7])
    if T % GLA_CHUNK == 0:
        o_gla, s_new = _gla(projp, gla_cols, gla_wa, gla_ba, gla_g, s0, GLA_CHUNK, 4, GLA_CHUNK)
    else:
        Tp = _round_up(T, 2 * SUBLANE)
        o_gla, s_new = _gla(jnp.pad(projp, ((0, 0), (0, Tp - T), (0, 0))), gla_cols, gla_wa, gla_ba, gla_g, s0,
                            Tp, 1, T)
        o_gla = o_gla[:, :T]
    return _out_proj(o_nsa, o_gla, x, norm[3], w_out), (new_nsa, win_state, s_new)


def _mixer_cd(x, norm, w_in, w_out, b_f, a_re, a_im, b_re, b_im, c_re, c_im, d, log_dt, glu_w, glu_b, past):
    B, T, _ = x.shape
    if past is None:
        projp, kv_t = _norm_proj(x, norm[0], norm[1], norm[2], w_in, t_cols=(FOX_WIDTH, 2 * FOX_WIDTH))
    else:
        projp = _norm_proj(x, norm[0], norm[1], norm[2], w_in)
    q, k, v, f, u = _split(projp[..., :sum(CD_SPLITS)], CD_SPLITS)
    log_f = jax.nn.log_sigmoid(f + b_f)
    if past is None:
        new_kv = kv_t.reshape(B, 2, FOX_HEADS, HEAD_DIM, T).transpose(0, 4, 1, 2, 3)
    else:
        new_kv = jnp.stack([k.reshape(B, T, FOX_HEADS, HEAD_DIM), v.reshape(B, T, FOX_HEADS, HEAD_DIM)], axis=2)
    h0 = None
    if past is None:
        F = jnp.cumsum(log_f, axis=1)
        o_fox = _attn(projp, 0, FOX_WIDTH, 2 * FOX_WIDTH, G=FOX_HEADS, R=1, tq=256, tk=512,
                      fq=F, fk=F.transpose(0, 2, 1))
    else:
        pool_kv, pool_lf, page_table, h0 = past
        t0 = page_table.shape[1] * PAGE_SIZE
        lf_all = jnp.concatenate([_gather_pages(pool_lf, page_table), log_f], axis=1)
        F = jnp.cumsum(lf_all, axis=1)
        Fq = F[:, t0:]
        fq_rows = Fq.reshape(B, T * FOX_HEADS, 1)
        fk_rows = jnp.tile(F.transpose(0, 2, 1), (1, T, 1))
        row_t = jnp.repeat(jnp.arange(T), FOX_HEADS)
        new_j = jnp.arange(LANE)
        bias_pages = fq_rows - fk_rows[:, :, :t0]
        fk_new = jnp.pad(fk_rows[:, :, t0:], ((0, 0), (0, 0), (0, LANE - T)))
        causal_new = (new_j[None, :] <= row_t[:, None]) & (new_j[None, :] < T)
        bias_new = jnp.where(causal_new[None], fq_rows - fk_new, NEG_INF)
        pad_rows = lambda a: jnp.pad(a, ((0, 0), (0, LANE - T), (0, 0)))
        o = _decode_attn(_block_diag_queries(q, FOX_HEADS), pad_rows(k), pad_rows(v), bias_new,
                         pool_t=pool_kv.transpose(0, 2, 3, 4, 1),
                         page_table=page_table, k_kind=0, v_kind=1, bias_pages=bias_pages)
        o_fox = _own_head_columns(o, T, FOX_HEADS)
    y_s5, s5_state = _s5(u, a_re, a_im, b_re, b_im, c_re, c_im, d, log_dt, glu_w, glu_b, h0, min(T, 128))
    return _out_proj(o_fox, y_s5, x, norm[3], w_out), (new_kv, log_f, s5_state)


def kernel(x_prompt, x_sample, cache_ab_nsa_kv, cache_ab_win_kv, state_ab_gla, cache_cd_fox_kv, cache_cd_fox_logf, state_cd_s5, page_table, c_prompt, c_sample, ada_w, ada_b, norm_g, ab_w_in, ab_w_out, nsa_cmp_pe, nsa_cmp_w1, nsa_cmp_w2, gla_w_alpha, gla_b_alpha, gla_norm_g, cd_w_in, cd_w_out, fox_b_f, s5_a_re, s5_a_im, s5_b_re, s5_b_im, s5_c_re, s5_c_im, s5_d, s5_log_dt, s5_glu_w, s5_glu_b, router_w, router_b, moe_w1, moe_b1, moe_w2, moe_b2, final_norm_g):
    xs = [x_prompt, x_sample]
    cs = [c_prompt, c_sample]
    new = [{}, {}]
    n_prompt = x_prompt.shape[0] * x_prompt.shape[1]
    for l in range(DEPTH):
        j = l // 2
        mods = [jnp.split(_mm(jax.nn.silu(c), ada_w[l], ada_b[l]), 6, axis=-1) for c in cs]
        for grp in range(2):
            x = xs[grp]
            sh1, sc1, g1 = mods[grp][:3]
            norm = (norm_g[l, 0], sh1, sc1, g1)
            if l % 2 == 0:
                past = None
                if grp == 1:
                    past = (cache_ab_nsa_kv[j], page_table, cache_ab_win_kv[j], state_ab_gla[j])
                xs[grp], st = _mixer_ab(x, norm, ab_w_in[j], ab_w_out[j], nsa_cmp_pe[j], nsa_cmp_w1[j],
                                        nsa_cmp_w2[j], gla_w_alpha[j], gla_b_alpha[j], gla_norm_g[j], past)
                names = ('nsa_kv', 'win_kv', 'gla')
            else:
                past = None
                if grp == 1:
                    past = (cache_cd_fox_kv[j], cache_cd_fox_logf[j], page_table, state_cd_s5[j])
                xs[grp], st = _mixer_cd(x, norm, cd_w_in[j], cd_w_out[j], fox_b_f[j], s5_a_re[j], s5_a_im[j],
                                        s5_b_re[j], s5_b_im[j], s5_c_re[j], s5_c_im[j], s5_d[j], s5_log_dt[j],
                                        s5_glu_w[j], s5_glu_b[j], past)
                names = ('fox_kv', 'fox_logf', 's5')
            for name, s in zip(names, st):
                new[grp].setdefault(name, []).append(s)
        hn2 = [_modulated_norm(xs[grp], norm_g[l, 1], mods[grp][3], mods[grp][4]).reshape(-1, D_MODEL)
               for grp in range(2)]
        ym = _moe(jnp.concatenate(hn2, axis=0), router_w[l], router_b[l], moe_w1[l], moe_b1[l], moe_w2[l], moe_b2[l])
        yms = [ym[:n_prompt], ym[n_prompt:]]
        for grp in range(2):
            xs[grp] = xs[grp] + mods[grp][5][:, None, :] * yms[grp].reshape(xs[grp].shape)
    ys = [_rmsnorm(x, final_norm_g) for x in xs]
    names = ('nsa_kv', 'win_kv', 'gla', 'fox_kv', 'fox_logf', 's5')
    outs = [jnp.stack(new[grp][name]) for grp in range(2) for name in names]
    return (ys[0], ys[1]) + tuple(outs)
```
